```python
import math
import jax
import jax.numpy as jnp
from jax import lax
import numpy as np

D_MODEL = 1024
BATCH = 2
SEQ = 16384
DEPTH = 2

GRID_W = 64
CTX_LEN = 256
NORM_EPS = 1e-6

D_MIX = D_MODEL
HEAD_DIM = 64
NA_WIDTH = D_MIX // 2
NA_HEADS = NA_WIDTH // HEAD_DIM
NA_WIN_ROWS = 8
NA_WIN_COLS = 16
NA_QBLOCK_ROWS = 2
RW_WIDTH = D_MIX // 4
RW_HEADS = RW_WIDTH // HEAD_DIM
RW_DECAY_LORA = 32
RW_ICLR_LORA = 32
RW_GATE_LORA = 64
RW_GN_EPS = 64e-5
GD_WIDTH = D_MIX - NA_WIDTH - RW_WIDTH
GD_HEADS = GD_WIDTH // HEAD_DIM
GD_CONV = 5
GD_CHUNK = 64
NA_COLS = 3 * NA_WIDTH
RW_COLS = 3 * RW_WIDTH + RW_DECAY_LORA + RW_ICLR_LORA + RW_GATE_LORA
GD_COLS = 4 * GD_WIDTH + 4 * GD_HEADS
P_IN = NA_COLS + RW_COLS + GD_COLS
N_EXPERTS = 32
TOP_K = 4
D_EXPERT = D_MODEL
SWIGLU_LIMIT = 7.0
SWIGLU_ALPHA = 1.702
MOE_BLOCK = 128

kernel_name = 'hybrid_na_rwkv7_gdn_moe_dit'


def rms_norm(x, g):
    xf = x.astype(jnp.float32)
    return xf * lax.rsqrt(jnp.mean(xf * xf, axis=-1, keepdims=True) + NORM_EPS) * g


def l2_norm(x):
    xf = x.astype(jnp.float32)
    return xf * lax.rsqrt(jnp.sum(xf * xf, axis=-1, keepdims=True) + NORM_EPS)


def split_heads(t, n_heads):
    B, T, C = t.shape
    return t.reshape(B, T, n_heads, C // n_heads).transpose(0, 2, 1, 3)


def merge_heads(t):
    B, H, T, dh = t.shape
    return t.transpose(0, 2, 1, 3).reshape(B, T, H * dh)


def centred_token_shift(p, mu):
    zero = jnp.zeros_like(p[:, :1])
    prev = jnp.concatenate([zero, p[:, :-1]], axis=1)
    nxt = jnp.concatenate([p[:, 1:], zero], axis=1)
    return p + mu * (0.5 * (prev + nxt) - p)


def depthwise_conv_centred(x, w):
    K, C = w.shape
    return lax.conv_general_dilated(x, w[:, None, :].astype(x.dtype), window_strides=(1,),
                                    padding=((K // 2, K // 2),),
                                    dimension_numbers=('NWC', 'WIO', 'NWC'),
                                    feature_group_count=C)


def neighbourhood_attention(q, k, v, qc, kc, vc, rpb):
    B, H, S, dh = q.shape
    rows = S // GRID_W
    kr = min(NA_WIN_ROWS, rows)
    kw = NA_WIN_COLS
    qblk = NA_QBLOCK_ROWS * GRID_W
    n_loc = kr * kw
    scale = dh ** -0.5
    col = jnp.arange(GRID_W)
    key_c = jnp.clip(col - kw // 2, 0, GRID_W - kw)[:, None] + jnp.arange(kw)
    dc = key_c - col[:, None] + (NA_WIN_COLS - 1)

    def block(b):
        r = b * NA_QBLOCK_ROWS + jnp.arange(NA_QBLOCK_ROWS)
        key_r = jnp.clip(r - kr // 2, 0, rows - kr)[:, None] + jnp.arange(kr)
        dr = key_r - r[:, None] + (NA_WIN_ROWS - 1)
        kidx = (key_r[:, None, :, None] * GRID_W + key_c[None, :, None, :]).reshape(qblk, n_loc)
        bias = rpb[:, dr[:, None, :, None], dc[None, :, None, :]].reshape(H, qblk, n_loc)
        qb = lax.dynamic_slice_in_dim(q, b * qblk, qblk, axis=2)
        kg = jnp.take(k, kidx, axis=2)
        vg = jnp.take(v, kidx, axis=2)
        s_loc = jnp.einsum('bhqd,bhqkd->bhqk', qb, kg) * scale + bias
        s_ctx = jnp.einsum('bhqd,bhcd->bhqc', qb, kc) * scale
        prob = jax.nn.softmax(jnp.concatenate([s_loc, s_ctx], axis=-1).astype(jnp.float32), axis=-1)
        return (jnp.einsum('bhqk,bhqkd->bhqd', prob[..., :n_loc], vg)
                + jnp.einsum('bhqc,bhcd->bhqd', prob[..., n_loc:], vc))

    out = lax.map(block, jnp.arange(rows // NA_QBLOCK_ROWS))
    return jnp.moveaxis(out, 0, 2).reshape(B, H, S, dh)


def context_attention(qc, kc, vc):
    s = jnp.einsum('bhqd,bhkd->bhqk', qc, kc) * qc.shape[-1] ** -0.5
    prob = jax.nn.softmax(s.astype(jnp.float32), axis=-1)
    return jnp.einsum('bhqk,bhkd->bhqd', prob, vc)


def na_mixer(p, pc, q_gain, k_gain, rpb):
    def qkv(t):
        q, k, v = jnp.split(t, 3, axis=-1)
        return (rms_norm(split_heads(q, NA_HEADS), q_gain),
                rms_norm(split_heads(k, NA_HEADS), k_gain),
                split_heads(v, NA_HEADS))
    q, k, v = qkv(p)
    qc, kc, vc = qkv(pc)
    return (merge_heads(neighbourhood_attention(q, k, v, qc, kc, vc, rpb)),
            merge_heads(context_attention(qc, kc, vc)))


def rwkv7_prepare(p, mu, w0, w_up, a0, a_up, g_up, k_k, k_a):
    B, T, _ = p.shape
    W = RW_WIDTH
    p = centred_token_shift(p, mu)
    r, k, v = p[..., :W], p[..., W:2 * W], p[..., 2 * W:3 * W]
    o = 3 * W
    wd = p[..., o:o + RW_DECAY_LORA]
    o += RW_DECAY_LORA
    ad = p[..., o:o + RW_ICLR_LORA]
    o += RW_ICLR_LORA
    gd = p[..., o:]
    gate = jax.nn.sigmoid(gd) @ g_up
    hd = lambda t: t.reshape(t.shape[:-1] + (RW_HEADS, HEAD_DIM))
    kk = l2_norm(hd(k * k_k))
    w_log = -jax.nn.softplus(-(w0[:, None, None, :] + jnp.einsum('btr,drc->dbtc', jnp.tanh(wd), w_up))) - 0.5
    decay = jnp.exp(-jnp.exp(w_log))
    a = jax.nn.sigmoid(a0[:, None, None, :] + jnp.einsum('btr,drc->dbtc', ad, a_up))
    k_dir = k[None] * (1 + (a - 1) * k_a)
    return hd(r), hd(v), kk, gate, hd(decay), hd(a), hd(k_dir)


def rwkv7_scan(r, decay, k, v, kk, a, s0, reverse):
    xs = tuple(jnp.moveaxis(t.astype(jnp.float32), 1, 0) for t in (r, decay, k, v, kk, a))

    def step(S, inp):
        r_t, w_t, k_t, v_t, kk_t, a_t = inp
        s_kk = jnp.einsum('bhvk,bhk->bhv', S, kk_t)
        S = (S * w_t[:, :, None, :] - s_kk[..., None] * (kk_t * a_t)[:, :, None, :]
             + v_t[..., None] * k_t[:, :, None, :])
        return S, jnp.einsum('bhvk,bhk->bhv', S, r_t)

    S, ys = lax.scan(step, s0, xs, reverse=reverse)
    return jnp.moveaxis(ys, 0, 1), S


def rwkv7_output(y, r, v, k_dir, gate, r_k, ln_w, ln_b):
    B, T, H, N = y.shape
    mean = jnp.mean(y, axis=-1, keepdims=True)
    var = jnp.mean(jnp.square(y - mean), axis=-1, keepdims=True)
    yn = ((y - mean) * lax.rsqrt(var + RW_GN_EPS)).reshape(B, T, H * N) * ln_w + ln_b
    bonus = jnp.sum(jnp.sum(r[None] * k_dir * r_k, axis=-1, keepdims=True), axis=0)
    return (yn + (bonus * v).reshape(B, T, H * N)) * gate


def rwkv7_mixer(p, pc, mu, w0, w_up, a0, a_up, g_up, k_k, k_a, r_k, ln_w, ln_b):
    lat = rwkv7_prepare(p, mu, w0, w_up, a0, a_up, g_up, k_k, k_a)
    ctx = rwkv7_prepare(pc, mu, w0, w_up, a0, a_up, g_up, k_k, k_a)
    B = p.shape[0]
    s0 = jnp.zeros((B, RW_HEADS, HEAD_DIM, HEAD_DIM), jnp.float32)
    y_lat = 0.0
    y_ctx = 0.0
    for d, rev in ((0, False), (1, True)):
        r, v, kk, _, decay, a, k_dir = ctx
        yc, s_ctx = rwkv7_scan(r, decay[d], k_dir[d], v, kk, a[d], s0, rev)
        r, v, kk, _, decay, a, k_dir = lat
        yl, _ = rwkv7_scan(r, decay[d], k_dir[d], v, kk, a[d], s_ctx, rev)
        y_ctx = y_ctx + yc
        y_lat = y_lat + yl
    out_lat = rwkv7_output(y_lat, lat[0], lat[1], lat[6], lat[3], r_k, ln_w, ln_b)
    out_ctx = rwkv7_output(y_ctx, ctx[0], ctx[1], ctx[6], ctx[3], r_k, ln_w, ln_b)
    return out_lat, out_ctx


def gated_delta_chunked(q, k, v, g, beta, s0):
    q, k, v, g, beta, s0 = (t.astype(jnp.float32) for t in (q, k, v, g, beta, s0))
    B, H, T, dk = q.shape
    dv = v.shape[-1]
    C = GD_CHUNK
    n = T // C
    q = (q * dk ** -0.5).reshape(B, H, n, C, dk)
    k = k.reshape(B, H, n, C, dk)
    v = v.reshape(B, H, n, C, dv)
    gc = jnp.cumsum(g.reshape(B, H, n, C), axis=-1)
    beta = beta.reshape(B, H, n, C)
    kb = k * beta[..., None]
    vb = v * beta[..., None]
    incl = jnp.tril(jnp.ones((C, C), bool))
    strict = jnp.tril(jnp.ones((C, C), bool), -1)
    decay_mask = jnp.exp(jnp.where(incl, gc[..., :, None] - gc[..., None, :], -jnp.inf))
    m = jnp.where(strict, jnp.einsum('bhnid,bhnjd->bhnij', kb, k) * decay_mask, 0.0)
    eye = jnp.eye(C, dtype=jnp.float32)
    t_inv = lax.linalg.triangular_solve(m + eye, jnp.broadcast_to(eye, m.shape), left_side=True,
                                        lower=True, unit_diagonal=True)
    u = t_inv @ vb
    w = t_inv @ (kb * jnp.exp(gc)[..., None])
    qk = jnp.einsum('bhnid,bhnjd->bhnij', q, k) * decay_mask
    qg = q * jnp.exp(gc)[..., None]
    kd = k * jnp.exp(gc[..., -1:] - gc)[..., None]
    decay_last = jnp.exp(gc[..., -1])
    xs = tuple(jnp.moveaxis(t, 2, 0) for t in (qg, qk, u, w, kd, decay_last))

    def step(S, inp):
        qg_i, qk_i, u_i, w_i, kd_i, dl_i = inp
        v_new = u_i - w_i @ S
        o = qg_i @ S + qk_i @ v_new
        S = S * dl_i[..., None, None] + jnp.einsum('bhck,bhcv->bhkv', kd_i, v_new)
        return S, o

    S, o = lax.scan(step, s0, xs)
    return jnp.moveaxis(o, 0, 2).reshape(B, H, T, dv), S


def gdn_prepare(p, conv_w, A_log, dt_bias):
    B, T, _ = p.shape
    W = GD_WIDTH
    qkv = jax.nn.silu(depthwise_conv_centred(p[..., :3 * W], conv_w))
    q = l2_norm(split_heads(qkv[..., :W], GD_HEADS))
    k = l2_norm(split_heads(qkv[..., W:2 * W], GD_HEADS))
    v = split_heads(qkv[..., 2 * W:], GD_HEADS)
    z = split_heads(p[..., 3 * W:4 * W], GD_HEADS)
    a = p[..., 4 * W:4 * W + 2 * GD_HEADS].reshape(B, T, 2, GD_HEADS)
    b = p[..., 4 * W + 2 * GD_HEADS:].reshape(B, T, 2, GD_HEADS)
    g = -jnp.exp(A_log) * jax.nn.softplus(a + dt_bias)
    beta = jax.nn.sigmoid(b)
    return q, k, v, z, jnp.transpose(g, (2, 0, 3, 1)), jnp.transpose(beta, (2, 0, 3, 1))


def gdn_mixer(p, pc, conv_w, A_log, dt_bias, norm_g):
    ql, kl, vl, zl, gl, bl = gdn_prepare(p, conv_w, A_log, dt_bias)
    qc, kc, vc, zc, gcx, bcx = gdn_prepare(pc, conv_w, A_log, dt_bias)
    B = p.shape[0]
    s0 = jnp.zeros((B, GD_HEADS, HEAD_DIM, HEAD_DIM), jnp.float32)
    fl = lambda t: jnp.flip(t, axis=2)
    oc_f, s_f = gated_delta_chunked(qc, kc, vc, gcx[0], bcx[0], s0)
    ol_f, _ = gated_delta_chunked(ql, kl, vl, gl[0], bl[0], s_f)
    oc_b, s_b = gated_delta_chunked(fl(qc), fl(kc), fl(vc), fl(gcx[1]), fl(bcx[1]), s0)
    ol_b, _ = gated_delta_chunked(fl(ql), fl(kl), fl(vl), fl(gl[1]), fl(bl[1]), s_b)
    out_lat = merge_heads(rms_norm(ol_f + fl(ol_b), norm_g) * jax.nn.silu(zl))
    out_ctx = merge_heads(rms_norm(oc_f + fl(oc_b), norm_g) * jax.nn.silu(zc))
    return out_lat, out_ctx


def moe_ffn(h, router_w, router_b, w_gu, b_gu, w_down, b_down):
    T, D = h.shape
    logits = (h @ router_w + router_b).astype(jnp.float32)
    top_val, top_idx = lax.top_k(logits, TOP_K)
    gates = jax.nn.softmax(top_val, axis=-1)
    tk = T * TOP_K
    flat_e = top_idx.reshape(tk)
    flat_tok = jnp.arange(tk, dtype=jnp.int32) // TOP_K
    order = jnp.argsort(flat_e)
    se, stok, sgate = flat_e[order], flat_tok[order], gates.reshape(tk)[order]
    counts = jnp.bincount(flat_e, length=N_EXPERTS)
    padded = (counts + MOE_BLOCK - 1) // MOE_BLOCK * MOE_BLOCK
    pad_end = jnp.cumsum(padded)
    pad_start = pad_end - padded
    start = jnp.cumsum(counts) - counts
    dest = pad_start[se] + jnp.arange(tk, dtype=jnp.int32) - start[se]
    n_pad = -(-tk // MOE_BLOCK) * MOE_BLOCK + N_EXPERTS * MOE_BLOCK
    n_blocks = n_pad // MOE_BLOCK
    buf_tok = jnp.full((n_pad,), T, jnp.int32).at[dest].set(stok)
    buf_gate = jnp.zeros((n_pad,), jnp.float32).at[dest].set(sgate)
    blk_e = jnp.minimum(jnp.searchsorted(pad_end, jnp.arange(n_blocks) * MOE_BLOCK, side='right'), N_EXPERTS - 1)
    h_pad = jnp.concatenate([h, jnp.zeros((1, D), h.dtype)], axis=0)
    xb = h_pad[buf_tok].reshape(n_blocks, MOE_BLOCK, D)

    def expert_block(args):
        xk, e = args
        gu = xk @ w_gu[e] + b_gu[e]
        gate = jnp.minimum(gu[:, :D_EXPERT], SWIGLU_LIMIT)
        lin = jnp.clip(gu[:, D_EXPERT:], -SWIGLU_LIMIT, SWIGLU_LIMIT)
        act = gate * jax.nn.sigmoid(SWIGLU_ALPHA * gate) * (lin + 1)
        return act @ w_down[e] + b_down[e]

    yb = lax.map(expert_block, (xb, blk_e))
    y = yb.reshape(n_pad, D) * buf_gate[:, None]
    return jnp.zeros((T + 1, D), y.dtype).at[buf_tok].add(y)[:T]


def setup_inputs(seed: int = 0) -> dict:
    key = jax.random.key(seed)
    keys = iter(jax.random.split(key, 40))

    def nrm(shape, std):
        return std * jax.random.normal(next(keys), shape, jnp.float32)

    def uni(shape, lo, hi):
        return jax.random.uniform(next(keys), shape, jnp.float32, lo, hi)

    L, D = DEPTH, D_MODEL
    x = nrm((BATCH, SEQ, D), 1.0)
    c = nrm((BATCH, D), 1.0)
    ctx = nrm((BATCH, CTX_LEN, D), 1.0)
    c_ctx = nrm((D,), 1.0)
    ada_w = nrm((L, D, 6 * D), 0.5 * D ** -0.5)
    ada_b = nrm((L, 6 * D), 0.02)
    norm_mix_g = 1.0 + nrm((L, D), 0.05)
    norm_ffn_g = 1.0 + nrm((L, D), 0.05)
    w_in = nrm((L, D, P_IN), D ** -0.5)
    w_out = nrm((L, D_MIX, D), D_MIX ** -0.5)
    na_q_gain = 1.0 + nrm((L, HEAD_DIM), 0.05)
    na_k_gain = 1.0 + nrm((L, HEAD_DIM), 0.05)
    na_rpb = nrm((L, NA_HEADS, 2 * NA_WIN_ROWS - 1, 2 * NA_WIN_COLS - 1), 0.1)
    rw_mu = uni((L, RW_COLS), 0.0, 1.0)
    rw_w0 = uni((L, 2, RW_WIDTH), -5.0, 1.0)
    rw_w_up = nrm((L, 2, RW_DECAY_LORA, RW_WIDTH), 0.1)
    rw_a0 = nrm((L, 2, RW_WIDTH), 0.1)
    rw_a_up = nrm((L, 2, RW_ICLR_LORA, RW_WIDTH), RW_ICLR_LORA ** -0.5)
    rw_g_up = nrm((L, RW_GATE_LORA, RW_WIDTH), RW_GATE_LORA ** -0.5)
    rw_k_k = 0.85 + nrm((L, RW_WIDTH), 0.05)
    rw_k_a = 1.0 + nrm((L, RW_WIDTH), 0.05)
    rw_r_k = nrm((L, RW_HEADS, HEAD_DIM), 0.1)
    rw_ln_w = 1.0 + nrm((L, RW_WIDTH), 0.05)
    rw_ln_b = nrm((L, RW_WIDTH), 0.02)
    gd_conv_w = nrm((L, GD_CONV, 3 * GD_WIDTH), GD_CONV ** -0.5)
    gd_A_log = jnp.log(uni((L, 2, GD_HEADS), 1.0, 16.0))
    dt = jnp.exp(uni((L, 2, GD_HEADS), math.log(1e-3), math.log(1e-1)))
    gd_dt_bias = dt + jnp.log(-jnp.expm1(-dt))
    gd_norm_g = 1.0 + nrm((L, HEAD_DIM), 0.05)
    moe_router_w = nrm((L, D, N_EXPERTS), D ** -0.5)
    moe_router_b = nrm((L, N_EXPERTS), 0.01)
    moe_w_gu = nrm((L, N_EXPERTS, D, 2 * D_EXPERT), D ** -0.5)
    moe_b_gu = nrm((L, N_EXPERTS, 2 * D_EXPERT), 0.01)
    moe_w_down = nrm((L, N_EXPERTS, D_EXPERT, D), D_EXPERT ** -0.5)
    moe_b_down = nrm((L, N_EXPERTS, D), 0.01)
    return {'x': x, 'c': c, 'ctx': ctx, 'c_ctx': c_ctx, 'ada_w': ada_w, 'ada_b': ada_b,
            'norm_mix_g': norm_mix_g, 'norm_ffn_g': norm_ffn_g, 'w_in': w_in, 'w_out': w_out,
            'na_q_gain': na_q_gain, 'na_k_gain': na_k_gain, 'na_rpb': na_rpb,
            'rw_mu': rw_mu, 'rw_w0': rw_w0, 'rw_w_up': rw_w_up, 'rw_a0': rw_a0, 'rw_a_up': rw_a_up,
            'rw_g_up': rw_g_up, 'rw_k_k': rw_k_k, 'rw_k_a': rw_k_a, 'rw_r_k': rw_r_k,
            'rw_ln_w': rw_ln_w, 'rw_ln_b': rw_ln_b,
            'gd_conv_w': gd_conv_w, 'gd_A_log': gd_A_log, 'gd_dt_bias': gd_dt_bias, 'gd_norm_g': gd_norm_g,
            'moe_router_w': moe_router_w, 'moe_router_b': moe_router_b, 'moe_w_gu': moe_w_gu,
            'moe_b_gu': moe_b_gu, 'moe_w_down': moe_w_down, 'moe_b_down': moe_b_down}


def reference(x, c, ctx, c_ctx, ada_w, ada_b, norm_mix_g, norm_ffn_g, w_in, w_out,
              na_q_gain, na_k_gain, na_rpb,
              rw_mu, rw_w0, rw_w_up, rw_a0, rw_a_up, rw_g_up, rw_k_k, rw_k_a, rw_r_k, rw_ln_w, rw_ln_b,
              gd_conv_w, gd_A_log, gd_dt_bias, gd_norm_g,
              moe_router_w, moe_router_b, moe_w_gu, moe_b_gu, moe_w_down, moe_b_down):
    B, S, D = x.shape
    n_ctx = ctx.shape[1]
    a_end = NA_COLS
    b_end = NA_COLS + RW_COLS
    for l in range(DEPTH):
        last = l == DEPTH - 1
        sh1, sc1, g1, sh2, sc2, g2 = jnp.split(jax.nn.silu(c) @ ada_w[l] + ada_b[l], 6, axis=-1)
        sh1c, sc1c, g1c, sh2c, sc2c, g2c = jnp.split(jax.nn.silu(c_ctx) @ ada_w[l] + ada_b[l], 6, axis=-1)
        h = rms_norm(x, norm_mix_g[l]) * (1 + sc1[:, None]) + sh1[:, None]
        hc = rms_norm(ctx, norm_mix_g[l]) * (1 + sc1c) + sh1c
        p = h @ w_in[l]
        pc = hc @ w_in[l]
        o_na, oc_na = na_mixer(p[..., :a_end], pc[..., :a_end], na_q_gain[l], na_k_gain[l], na_rpb[l])
        o_rw, oc_rw = rwkv7_mixer(p[..., a_end:b_end], pc[..., a_end:b_end], rw_mu[l], rw_w0[l], rw_w_up[l],
                                  rw_a0[l], rw_a_up[l], rw_g_up[l], rw_k_k[l], rw_k_a[l], rw_r_k[l],
                                  rw_ln_w[l], rw_ln_b[l])
        o_gd, oc_gd = gdn_mixer(p[..., b_end:], pc[..., b_end:], gd_conv_w[l], gd_A_log[l], gd_dt_bias[l],
                                gd_norm_g[l])
        x = x + g1[:, None] * (jnp.concatenate([o_na, o_rw, o_gd], axis=-1) @ w_out[l])
        h2 = rms_norm(x, norm_ffn_g[l]) * (1 + sc2[:, None]) + sh2[:, None]
        if last:
            y = moe_ffn(h2.reshape(B * S, D), moe_router_w[l], moe_router_b[l], moe_w_gu[l], moe_b_gu[l],
                        moe_w_down[l], moe_b_down[l])
            x = x + g2[:, None] * y.reshape(B, S, D)
        else:
            ctx = ctx + g1c * (jnp.concatenate([oc_na, oc_rw, oc_gd], axis=-1) @ w_out[l])
            hc2 = rms_norm(ctx, norm_ffn_g[l]) * (1 + sc2c) + sh2c
            y = moe_ffn(jnp.concatenate([h2.reshape(B * S, D), hc2.reshape(B * n_ctx, D)], axis=0),
                        moe_router_w[l], moe_router_b[l], moe_w_gu[l], moe_b_gu[l],
                        moe_w_down[l], moe_b_down[l])
            x = x + g2[:, None] * y[:B * S].reshape(B, S, D)
            ctx = ctx + g2c * y[B * S:].reshape(B, n_ctx, D)
    return x
```

```python
import functools
import math

import numpy as np
import jax
import jax.numpy as jnp
from jax import lax
from jax.experimental import pallas as pl
from jax.experimental.pallas import tpu as pltpu

F32 = jnp.float32
BF16 = jnp.bfloat16
HIGHEST = lax.Precision.HIGHEST

GRID_W = 64
NORM_EPS = 1e-6
HEAD_DIM = 64
NA_WIN_ROWS = 8
NA_WIN_COLS = 16
RW_DECAY_LORA = 32
RW_ICLR_LORA = 32
RW_GATE_LORA = 64
RW_GN_EPS = 64e-5
GD_CONV = 5
TOP_K = 4
SWIGLU_LIMIT = 7.0
SWIGLU_ALPHA = 1.702

LANES = 128
TM = 256
NA_QROWS = 4
NA_KROWS = NA_QROWS + NA_WIN_ROWS
CHUNK = 64
MASKED = -1e30
VMEM_LIMIT = 48 * 1024 * 1024


def _cparams(sem):
    return pltpu.CompilerParams(dimension_semantics=sem, vmem_limit_bytes=VMEM_LIMIT)


def _dot(a, b, **kw):
    return jnp.dot(a, b, preferred_element_type=F32, **kw)


def _dot_nt(a, b, **kw):
    return lax.dot_general(a, b, (((1,), (1,)), ((), ())), preferred_element_type=F32, **kw)


def _dot_tn(a, b, **kw):
    return lax.dot_general(a, b, (((0,), (0,)), ((), ())), preferred_element_type=F32, **kw)


def _adaln_kernel(c_ref, w_ref, b_ref, o_ref):
    c = c_ref[...]
    s = c * jax.nn.sigmoid(c)
    o_ref[0] = _dot(s.astype(BF16), w_ref[0].astype(BF16)) + b_ref[0]


def _adaln(cvec, ada_w, ada_b):
    depth, d, n = ada_w.shape
    tn = n // 4
    return pl.pallas_call(
        _adaln_kernel,
        grid=(depth, n // tn),
        in_specs=[pl.BlockSpec((8, d), lambda l, j: (0, 0)),
                  pl.BlockSpec((1, d, tn), lambda l, j: (l, 0, j)),
                  pl.BlockSpec((1, 1, tn), lambda l, j: (l, 0, j))],
        out_specs=pl.BlockSpec((1, 8, tn), lambda l, j: (l, 0, j)),
        out_shape=jax.ShapeDtypeStruct((depth, 8, n), F32),
        compiler_params=_cparams(("arbitrary", "arbitrary")),
        name="adaln",
    )(cvec, ada_w, ada_b.reshape(depth, 1, n))


def _mod_index(n_ctx_tiles, n_batch):
    return lambda b, t: (jnp.where(t < n_ctx_tiles, n_batch, b), 0, 0)


def _rms(x):
    return x * lax.rsqrt(jnp.mean(x * x, axis=-1, keepdims=True) + NORM_EPS)


def _inproj_kernel(x_ref, mod_ref, g_ref, w_ref, qkg_ref, avg_ref, na_ref, rw_ref, gd_ref, *, na_w, rw_w):
    mod = mod_ref[0]
    h = _rms(x_ref[0]) * g_ref[...]
    hb = (h * (1.0 + mod[1:2]) + mod[0:1]).astype(BF16)
    pa = _dot(hb, w_ref[:, 0:3 * na_w])
    avg = avg_ref[...]
    qkg = qkg_ref[...]
    q = pa[:, 0:na_w]
    k = pa[:, na_w:2 * na_w]
    qn = q * lax.rsqrt(_dot(q * q, avg) + NORM_EPS) * qkg[0:1]
    kn = k * lax.rsqrt(_dot(k * k, avg) + NORM_EPS) * qkg[1:2]
    na_ref[0, :, 0:na_w] = qn.astype(BF16)
    na_ref[0, :, na_w:2 * na_w] = kn.astype(BF16)
    na_ref[0, :, 2 * na_w:3 * na_w] = pa[:, 2 * na_w:3 * na_w].astype(BF16)
    rw_ref[0] = _dot(hb, w_ref[:, 3 * na_w:3 * na_w + rw_w])
    gd_ref[0] = _dot(hb, w_ref[:, 3 * na_w + rw_w:])


def _inproj(xa, mods, g, w_all, qkg, avg, n_ctx, na_w, rw_w, gd_w):
    b, l, d = xa.shape
    nt = l // TM
    kern = functools.partial(_inproj_kernel, na_w=na_w, rw_w=rw_w)
    return pl.pallas_call(
        kern,
        grid=(b, nt),
        in_specs=[pl.BlockSpec((1, TM, d), lambda i, t: (i, t, 0)),
                  pl.BlockSpec((1, 6, d), _mod_index(n_ctx // TM, b)),
                  pl.BlockSpec((1, d), lambda i, t: (0, 0)),
                  pl.BlockSpec(w_all.shape, lambda i, t: (0, 0)),
                  pl.BlockSpec(qkg.shape, lambda i, t: (0, 0)),
                  pl.BlockSpec(avg.shape, lambda i, t: (0, 0))],
        out_specs=[pl.BlockSpec((1, TM, 3 * na_w), lambda i, t: (i, t, 0)),
                   pl.BlockSpec((1, TM, rw_w), lambda i, t: (i, t, 0)),
                   pl.BlockSpec((1, TM, gd_w), lambda i, t: (i, t, 0))],
        out_shape=[jax.ShapeDtypeStruct((b, l, 3 * na_w), BF16),
                   jax.ShapeDtypeStruct((b, l, rw_w), F32),
                   jax.ShapeDtypeStruct((b, l, gd_w), F32)],
        compiler_params=_cparams(("arbitrary", "arbitrary")),
        name="inproj",
    )(xa, mods, g, w_all, qkg, avg)


def _na_bias_tables(rpb, rows):
    n_heads = rpb.shape[0]
    nq, nk = NA_QROWS * GRID_W, NA_KROWS * GRID_W
    qi = np.arange(nq)
    ki = np.arange(nk)
    qc = (qi % GRID_W)[:, None]
    kc = (ki % GRID_W)[None, :]
    c0 = np.clip(qc - NA_WIN_COLS // 2, 0, GRID_W - NA_WIN_COLS)
    col_ok = (kc >= c0) & (kc < c0 + NA_WIN_COLS)
    dc = np.clip(kc - qc + NA_WIN_COLS - 1, 0, 2 * NA_WIN_COLS - 2)
    tables = []
    for r0 in (0, NA_QROWS, rows - NA_QROWS):
        ks = int(np.clip(r0 - NA_WIN_ROWS // 2, 0, rows - NA_KROWS))
        qr = (r0 + qi // GRID_W)[:, None]
        kr = (ks + ki // GRID_W)[None, :]
        s0 = np.clip(qr - NA_WIN_ROWS // 2, 0, rows - NA_WIN_ROWS)
        ok = col_ok & (kr >= s0) & (kr < s0 + NA_WIN_ROWS)
        dr = np.clip(kr - qr + NA_WIN_ROWS - 1, 0, 2 * NA_WIN_ROWS - 2)
        tables.append(jnp.where(ok[None], rpb[:, dr, dc], MASKED))
    tables.append(jnp.full((n_heads, nq, nk), MASKED, F32))
    return jnp.stack(tables).reshape(4, n_heads // 2, 2, nq, nk)


def _na_kernel(q_ref, k_ref, v_ref, bias_ref, o_ref, *, n_ctx, rows):
    j = pl.program_id(2)
    nk = NA_KROWS * GRID_W
    ks_row = jnp.clip((j - 1) * NA_QROWS - NA_WIN_ROWS // 2, 0, rows - NA_KROWS)
    kstart = pl.multiple_of(n_ctx + ks_row * GRID_W, GRID_W)
    q = q_ref[0]
    kl = k_ref[0, pl.ds(kstart, nk), :]
    vl = v_ref[0, pl.ds(kstart, nk), :]
    kc = k_ref[0, 0:n_ctx, :]
    vc = v_ref[0, 0:n_ctx, :]
    lane = lax.broadcasted_iota(jnp.int32, q.shape, 1)
    outs = []
    for h in range(2):
        in_head = (lane >= h * HEAD_DIM) & (lane < (h + 1) * HEAD_DIM)
        qh = jnp.where(in_head, q, jnp.zeros_like(q))
        s_loc = _dot_nt(qh, kl) + bias_ref[0, 0, h]
        s_ctx = _dot_nt(qh, kc)
        m = jnp.maximum(jnp.max(s_loc, axis=-1, keepdims=True), jnp.max(s_ctx, axis=-1, keepdims=True))
        p_loc = jnp.exp(s_loc - m)
        p_ctx = jnp.exp(s_ctx - m)
        den = jnp.sum(p_loc, axis=-1, keepdims=True) + jnp.sum(p_ctx, axis=-1, keepdims=True)
        o = _dot(p_loc.astype(BF16), vl) + _dot(p_ctx.astype(BF16), vc)
        outs.append(o / den)
    o_ref[0] = jnp.where(lane < HEAD_DIM, outs[0], outs[1]).astype(o_ref.dtype)


def _na_attention(p_na, bias, n_ctx, na_w):
    b, l, _ = p_na.shape
    rows = (l - n_ctx) // GRID_W
    nq = NA_QROWS * GRID_W
    assert n_ctx == nq and rows >= NA_KROWS and rows % NA_QROWS == 0
    n_pairs = na_w // LANES
    nblk = l // nq
    kern = functools.partial(_na_kernel, n_ctx=n_ctx, rows=rows)

    def bias_index(i, hp, j):
        pat = jnp.where(j == 0, 3, jnp.where(j == 1, 0, jnp.where(j == nblk - 1, 2, 1)))
        return (pat, hp, 0, 0, 0)

    return pl.pallas_call(
        kern,
        grid=(b, n_pairs, nblk),
        in_specs=[pl.BlockSpec((1, nq, LANES), lambda i, hp, j: (i, j, hp)),
                  pl.BlockSpec((1, l, LANES), lambda i, hp, j: (i, 0, n_pairs + hp)),
                  pl.BlockSpec((1, l, LANES), lambda i, hp, j: (i, 0, 2 * n_pairs + hp)),
                  pl.BlockSpec((1, 1, 2) + bias.shape[3:], bias_index)],
        out_specs=pl.BlockSpec((1, nq, LANES), lambda i, hp, j: (i, j, hp)),
        out_shape=jax.ShapeDtypeStruct((b, l, na_w), BF16),
        compiler_params=_cparams(("arbitrary", "arbitrary", "arbitrary")),
        name="na_attention",
    )(p_na, p_na, p_na, bias)


def _tri_masks(reverse):
    t = lax.broadcasted_iota(jnp.int32, (CHUNK, CHUNK), 0)
    s = lax.broadcasted_iota(jnp.int32, (CHUNK, CHUNK), 1)
    earlier_or_same = (s >= t) if reverse else (s <= t)
    earlier = (s > t) if reverse else (s < t)
    return earlier_or_same, earlier, t == s


def _unit_tri_inverse(a):
    eye = (lax.broadcasted_iota(jnp.int32, a.shape, 0) == lax.broadcasted_iota(jnp.int32, a.shape, 1)).astype(F32)
    x = -a
    t = eye + x
    n = 2
    while n < CHUNK:
        x = _dot(x, x)
        t = t + _dot(t, x)
        n *= 2
    return t


def _head_lanes(width):
    lane = lax.broadcasted_iota(jnp.int32, (1, width), 1)
    return [(lane >= h * HEAD_DIM) & (lane < (h + 1) * HEAD_DIM) for h in range(width // HEAD_DIM)]


def _chunk_tail(a_ka, a_kk, a_ra, a_rk, kt, rt, v, kd, ad, p_end, heads):
    width = v.shape[1]
    w1 = jnp.zeros((CHUNK, width), F32)
    u0 = jnp.zeros((CHUNK, width), F32)
    w2 = jnp.zeros((CHUNK, width), F32)
    y1 = jnp.zeros((CHUNK, width), F32)
    for h, in_head in enumerate(heads):
        t_inv = _unit_tri_inverse(a_ka[h])
        w1_h = _dot(t_inv, kt)
        u0_h = _dot(t_inv, _dot(a_kk[h], v))
        w2_h = rt - _dot(a_ra[h], w1_h)
        y1_h = _dot(a_rk[h], v) - _dot(a_ra[h], u0_h)
        w1 = jnp.where(in_head, w1_h, w1)
        u0 = jnp.where(in_head, u0_h, u0)
        w2 = jnp.where(in_head, w2_h, w2)
        y1 = jnp.where(in_head, y1_h, y1)
    mm = _dot_tn(ad, w1)
    nn = _dot_tn(kd, v) - _dot_tn(ad, u0)
    eye = (lax.broadcasted_iota(jnp.int32, (HEAD_DIM, HEAD_DIM), 0)
           == lax.broadcasted_iota(jnp.int32, (HEAD_DIM, HEAD_DIM), 1))
    m_blocks, n_blocks = [], []
    for h in range(len(heads)):
        lo, hi = h * HEAD_DIM, (h + 1) * HEAD_DIM
        m_blocks.append(jnp.where(eye, p_end[:, lo:hi], 0.0) - mm[lo:hi, lo:hi])
        n_blocks.append(nn[lo:hi, lo:hi])
    return w2, y1, m_blocks, n_blocks


def _store_chunk(refs, d, c0, w2, y1, m_blocks, n_blocks):
    w2_ref, y1_ref, m_ref, n_ref = refs
    rows = pl.ds(c0, CHUNK)
    for h in range(len(m_blocks)):
        lo, hi = h * HEAD_DIM, (h + 1) * HEAD_DIM
        w2_ref[d, 0, h, rows, :] = w2[:, lo:hi]
        y1_ref[d, 0, h, rows, :] = y1[:, lo:hi]
        m_ref[d, 0, h, rows, :] = m_blocks[h]
        n_ref[d, 0, h, rows, :] = n_blocks[h]


def _scan_out_specs(b, n_heads, l):
    shape = jax.ShapeDtypeStruct((2, b, n_heads, l, HEAD_DIM), F32)
    spec = pl.BlockSpec((2, 1, n_heads, TM, HEAD_DIM), lambda i, t: (0, i, 0, t, 0))
    return [shape] * 4, [spec] * 4


def _seq_edges(t, n_ctx_tiles, n_tiles):
    first = (t == 0) | (t == n_ctx_tiles)
    last = (t == n_ctx_tiles - 1) | (t == n_tiles - 1)
    return first, last


def _fill_halo(ext_ref, x_ref, prev_ref, next_ref, first, last):
    ext_ref[0:8, :] = jnp.where(first, 0.0, prev_ref[0])
    ext_ref[8:8 + TM, :] = x_ref[0]
    ext_ref[8 + TM:16 + TM, :] = jnp.where(last, 0.0, next_ref[0])


def _halo_specs(width, n_tiles):
    per = TM // 8
    return [pl.BlockSpec((1, TM, width), lambda i, t: (i, t, 0)),
            pl.BlockSpec((1, 8, width), lambda i, t: (i, jnp.maximum(t * per - 1, 0), 0)),
            pl.BlockSpec((1, 8, width), lambda i, t: (i, jnp.minimum((t + 1) * per, n_tiles * per - 1), 0))]


def _full_spec(a):
    return pl.BlockSpec(a.shape, lambda i, t: (0,) * a.ndim)


def _rwkv_prep_kernel(x_ref, prev_ref, next_ref, mu_ref, w0_ref, wup_ref, a0_ref, aup_ref, gup_ref,
                      kk_ref, ka_ref, rk_ref, ones_ref,
                      w2_ref, y1_ref, m_ref, n_ref, bv_ref, gate_ref,
                      ext_ref, tok_ref, *, n_ctx_tiles, n_tiles, width):
    t = pl.program_id(1)
    first, last = _seq_edges(t, n_ctx_tiles, n_tiles)
    _fill_halo(ext_ref, x_ref, prev_ref, next_ref, first, last)
    p = ext_ref[8:8 + TM, :]
    prev = ext_ref[7:7 + TM, :]
    nxt = ext_ref[9:9 + TM, :]
    ps = p + mu_ref[...] * (0.5 * (prev + nxt) - p)
    r = ps[:, 0:width]
    k = ps[:, width:2 * width]
    v = ps[:, 2 * width:3 * width]
    lo = ps[:, 3 * width:3 * width + LANES]
    ones_bd = ones_ref[...]
    gate_ref[0] = _dot(jax.nn.sigmoid(lo), gup_ref[...])
    kq = k * kk_ref[...]
    kk = kq * lax.rsqrt(_dot(kq * kq, ones_bd, precision=HIGHEST) + NORM_EPS)
    tanh_lo = jnp.tanh(lo)
    tok_ref[0] = r
    tok_ref[1] = kk
    tok_ref[2] = v
    k_sum = jnp.zeros_like(k)
    for d in range(2):
        z = w0_ref[d:d + 1, :] + _dot(tanh_lo, wup_ref[d])
        w_log = -jax.nn.softplus(-z) - 0.5
        a = jax.nn.sigmoid(a0_ref[d:d + 1, :] + _dot(lo, aup_ref[d]))
        k_dir = k * (1.0 + (a - 1.0) * ka_ref[...])
        k_sum = k_sum + k_dir
        tok_ref[3 + 3 * d] = -jnp.exp(w_log)
        tok_ref[4 + 3 * d] = kk * a
        tok_ref[5 + 3 * d] = k_dir
    bonus = _dot(r * k_sum * rk_ref[...], ones_bd, precision=HIGHEST)
    bv_ref[0] = bonus * v

    heads = _head_lanes(width)
    out_refs = (w2_ref, y1_ref, m_ref, n_ref)

    def chunk_body(c, carry):
        c0 = pl.multiple_of(c * CHUNK, CHUNK)
        rows = pl.ds(c0, CHUNK)
        r_c, kk_c, v_c = tok_ref[0, rows, :], tok_ref[1, rows, :], tok_ref[2, rows, :]
        for d in range(2):
            logw, alpha, k_c = tok_ref[3 + 3 * d, rows, :], tok_ref[4 + 3 * d, rows, :], tok_ref[5 + 3 * d, rows, :]
            incl, strict, _ = _tri_masks(reverse=(d == 1))
            ci = _dot(incl.astype(F32), logw, precision=HIGHEST)
            ce = ci - logw
            e_ci, e_nci, e_ce = jnp.exp(ci), jnp.exp(-ci), jnp.exp(ce)
            rt = r_c * e_ci
            kt = kk_c * e_ce
            kh = k_c * e_nci
            ah = alpha * e_nci
            p_end = e_ci[0:1, :] if d == 1 else e_ci[CHUNK - 1:CHUNK, :]
            a_ka, a_kk, a_ra, a_rk = [], [], [], []
            for in_head in heads:
                kt_h = jnp.where(in_head, kt, 0.0)
                rt_h = jnp.where(in_head, rt, 0.0)
                a_ka.append(jnp.where(strict, _dot_nt(kt_h, ah), 0.0))
                a_kk.append(jnp.where(strict, _dot_nt(kt_h, kh), 0.0))
                a_ra.append(jnp.where(incl, _dot_nt(rt_h, ah), 0.0))
                a_rk.append(jnp.where(incl, _dot_nt(rt_h, kh), 0.0))
            res = _chunk_tail(a_ka, a_kk, a_ra, a_rk, kt, rt, v_c, kh * p_end, ah * p_end, p_end, heads)
            _store_chunk(out_refs, d, c0, *res)
        return carry

    lax.fori_loop(0, TM // CHUNK, chunk_body, 0)


def _rwkv_prepare(p_rw, pr, n_ctx):
    b, l, pw = p_rw.shape
    width = pr['rw_k_k'].shape[-1]
    n_heads = width // HEAD_DIM
    nt = l // TM
    kern = functools.partial(_rwkv_prep_kernel, n_ctx_tiles=n_ctx // TM, n_tiles=nt, width=width)
    params = [pr['rw_mu_p'], pr['rw_w0'], pr['rw_wup_p'], pr['rw_a0'], pr['rw_aup_p'], pr['rw_gup_p'],
              pr['rw_k_k'].reshape(1, width), pr['rw_k_a'].reshape(1, width), pr['rw_r_k'].reshape(1, width),
              pr['ones_bd']]
    scan_shapes, scan_specs = _scan_out_specs(b, n_heads, l)
    tok_spec = pl.BlockSpec((1, TM, width), lambda i, t: (i, t, 0))
    return pl.pallas_call(
        kern,
        grid=(b, nt),
        in_specs=_halo_specs(pw, nt) + [_full_spec(a) for a in params],
        out_specs=scan_specs + [tok_spec, tok_spec],
        out_shape=scan_shapes + [jax.ShapeDtypeStruct((b, l, width), F32)] * 2,
        scratch_shapes=[pltpu.VMEM((TM + 16, pw), F32), pltpu.VMEM((9, TM, width), F32)],
        compiler_params=_cparams(("arbitrary", "arbitrary")),
        name="rwkv_prepare",
    )(p_rw, p_rw, p_rw, *params)


def _scan_kernel(w2f_ref, y1f_ref, mf_ref, nf_ref, w2b_ref, y1b_ref, mb_ref, nb_ref, yf_ref, yb_ref, h_ref):
    @pl.when(pl.program_id(0) == 0)
    def _():
        h_ref[...] = jnp.zeros_like(h_ref)

    n_batch, n_heads = h_ref.shape[1], h_ref.shape[2]
    for d, (w2_ref, y1_ref, m_ref, n_ref, y_ref) in enumerate(((w2f_ref, y1f_ref, mf_ref, nf_ref, yf_ref),
                                                               (w2b_ref, y1b_ref, mb_ref, nb_ref, yb_ref))):
        for i in range(n_batch):
            for h in range(n_heads):
                state = h_ref[d, i, h]
                y_ref[i, h] = _dot(w2_ref[0, i, h], state) + y1_ref[0, i, h]
                h_ref[d, i, h] = _dot(m_ref[0, i, h], state, precision=HIGHEST) + n_ref[0, i, h]


def _scan(w2, y1, m, n, n_ctx):
    _, b, n_heads, l, _ = w2.shape
    nc = l // CHUNK
    ncc = n_ctx // CHUNK

    def bwd_chunk(i):
        return jnp.where(i < ncc, ncc - 1 - i, nc - 1 + ncc - i)

    blk = (1, b, n_heads, CHUNK, HEAD_DIM)
    fwd = pl.BlockSpec(blk, lambda i: (0, 0, 0, i, 0))
    bwd = pl.BlockSpec(blk, lambda i: (1, 0, 0, bwd_chunk(i), 0))
    out_shape = jax.ShapeDtypeStruct((b, n_heads, l, HEAD_DIM), F32)
    return pl.pallas_call(
        _scan_kernel,
        grid=(nc,),
        in_specs=[fwd] * 4 + [bwd] * 4,
        out_specs=[pl.BlockSpec(blk[1:], lambda i: (0, 0, i, 0)),
                   pl.BlockSpec(blk[1:], lambda i: (0, 0, bwd_chunk(i), 0))],
        out_shape=[out_shape, out_shape],
        scratch_shapes=[pltpu.VMEM((2, b, n_heads, HEAD_DIM, HEAD_DIM), F32)],
        compiler_params=_cparams(("arbitrary",)),
        name="chunk_scan",
    )(w2, y1, m, n, w2, y1, m, n)


def _gdn_prep_kernel(x_ref, prev_ref, next_ref, cw_ref, alog_ref, dtb_ref, ones_ref,
                     w2_ref, y1_ref, m_ref, n_ref,
                     ext_ref, tok_ref, *, n_ctx_tiles, n_tiles, width):
    t = pl.program_id(1)
    first, last = _seq_edges(t, n_ctx_tiles, n_tiles)
    _fill_halo(ext_ref, x_ref, prev_ref, next_ref, first, last)
    half = GD_CONV // 2
    conv = jnp.zeros((TM, 3 * width), F32)
    for j in range(GD_CONV):
        conv = conv + cw_ref[j:j + 1, :] * ext_ref[8 + j - half:8 + j - half + TM, 0:3 * width]
    qkv = conv * jax.nn.sigmoid(conv)
    ones_bd = ones_ref[...]

    def l2n(u):
        return u * lax.rsqrt(_dot(u * u, ones_bd, precision=HIGHEST) + NORM_EPS)

    q = l2n(qkv[:, 0:width]) * HEAD_DIM ** -0.5
    k = l2n(qkv[:, width:2 * width])
    v = qkv[:, 2 * width:3 * width]
    ab = ext_ref[8:8 + TM, 4 * width:4 * width + LANES]
    g_all = -jnp.exp(alog_ref[...]) * jax.nn.softplus(ab + dtb_ref[...])
    beta_all = jax.nn.sigmoid(ab)
    heads = _head_lanes(width)
    n_heads = len(heads)
    tok_ref[0] = q
    tok_ref[1] = k
    tok_ref[2] = v
    for d in range(2):
        g = jnp.zeros((TM, width), F32)
        beta = jnp.zeros((TM, width), F32)
        for h, in_head in enumerate(heads):
            ja = d * n_heads + h
            jb = 2 * n_heads + ja
            g = jnp.where(in_head, g_all[:, ja:ja + 1], g)
            beta = jnp.where(in_head, beta_all[:, jb:jb + 1], beta)
        tok_ref[3 + 2 * d] = g
        tok_ref[4 + 2 * d] = k * beta
    out_refs = (w2_ref, y1_ref, m_ref, n_ref)

    def chunk_body(c, carry):
        c0 = pl.multiple_of(c * CHUNK, CHUNK)
        rows = pl.ds(c0, CHUNK)
        q_c, k_c, v_c = tok_ref[0, rows, :], tok_ref[1, rows, :], tok_ref[2, rows, :]
        for d in range(2):
            g, kb = tok_ref[3 + 2 * d, rows, :], tok_ref[4 + 2 * d, rows, :]
            incl, strict, _ = _tri_masks(reverse=(d == 1))
            incl_t, strict_t, _ = _tri_masks(reverse=(d == 0))
            ci = _dot(incl.astype(F32), g, precision=HIGHEST)
            ce = ci - g
            ci_t = _dot_tn(g, incl_t.astype(F32), precision=HIGHEST)
            ce_t = _dot_tn(g, strict_t.astype(F32), precision=HIGHEST)
            ci_end = ci[0:1, :] if d == 1 else ci[CHUNK - 1:CHUNK, :]
            rt = q_c * jnp.exp(ci)
            kt = k_c * jnp.exp(ce)
            kd = kb * jnp.exp(ci_end - ci)
            ad = kb * jnp.exp(ci_end - ce)
            p_end = jnp.exp(ci_end)
            a_ka, a_kk, a_ra, a_rk = [], [], [], []
            for h, in_head in enumerate(heads):
                lo = h * HEAD_DIM
                kkb = _dot_nt(jnp.where(in_head, k_c, 0.0), kb)
                qkb = _dot_nt(jnp.where(in_head, q_c, 0.0), kb)
                ci_col, ce_col = ci[:, lo:lo + 1], ce[:, lo:lo + 1]
                ci_row, ce_row = ci_t[lo:lo + 1, :], ce_t[lo:lo + 1, :]
                a_ka.append(kkb * jnp.exp(jnp.where(strict, ce_col - ce_row, MASKED)))
                a_kk.append(kkb * jnp.exp(jnp.where(strict, ce_col - ci_row, MASKED)))
                a_ra.append(qkb * jnp.exp(jnp.where(incl, ci_col - ce_row, MASKED)))
                a_rk.append(qkb * jnp.exp(jnp.where(incl, ci_col - ci_row, MASKED)))
            res = _chunk_tail(a_ka, a_kk, a_ra, a_rk, kt, rt, v_c, kd, ad, p_end, heads)
            _store_chunk(out_refs, d, c0, *res)
        return carry

    lax.fori_loop(0, TM // CHUNK, chunk_body, 0)


def _gdn_prepare(p_gd, pr, n_ctx):
    b, l, pw = p_gd.shape
    width = pr['gd_width']
    n_heads = width // HEAD_DIM
    nt = l // TM
    kern = functools.partial(_gdn_prep_kernel, n_ctx_tiles=n_ctx // TM, n_tiles=nt, width=width)
    params = [pr['gd_conv_w'], pr['gd_alog_p'], pr['gd_dtb_p'], pr['ones_bd']]
    scan_shapes, scan_specs = _scan_out_specs(b, n_heads, l)
    return pl.pallas_call(
        kern,
        grid=(b, nt),
        in_specs=_halo_specs(pw, nt) + [_full_spec(a) for a in params],
        out_specs=scan_specs,
        out_shape=scan_shapes,
        scratch_shapes=[pltpu.VMEM((TM + 16, pw), F32), pltpu.VMEM((7, TM, width), F32)],
        compiler_params=_cparams(("arbitrary", "arbitrary")),
        name="gdn_prepare",
    )(p_gd, p_gd, p_gd, *params)


def _outproj_kernel(x_ref, mod_ref, na_ref, ryf_ref, ryb_ref, bv_ref, gate_ref, gyf_ref, gyb_ref, z_ref,
                    lnw_ref, lnb_ref, gng_ref, wo_ref, g2_ref, rw_ref, rb_ref,
                    x1_ref, h2_ref, topi_ref, topg_ref, *, na_w, rw_w):
    n_heads = ryf_ref.shape[1]
    rw_parts, gd_parts = [], []
    for h in range(n_heads):
        y = ryf_ref[0, h] + ryb_ref[0, h]
        yc = y - jnp.mean(y, axis=-1, keepdims=True)
        rw_parts.append(yc * lax.rsqrt(jnp.mean(yc * yc, axis=-1, keepdims=True) + RW_GN_EPS))
        gd_parts.append(_rms(gyf_ref[0, h] + gyb_ref[0, h]) * gng_ref[...])
    yn = jnp.concatenate(rw_parts, axis=-1)
    o_rw = (yn * lnw_ref[...] + lnb_ref[...] + bv_ref[0]) * gate_ref[0]
    z = z_ref[0]
    o_gd = jnp.concatenate(gd_parts, axis=-1) * (z * jax.nn.sigmoid(z))
    o = (_dot(na_ref[0], wo_ref[0:na_w, :])
         + _dot(o_rw.astype(BF16), wo_ref[na_w:na_w + rw_w, :])
         + _dot(o_gd.astype(BF16), wo_ref[na_w + rw_w:, :]))
    mod = mod_ref[0]
    x1 = x_ref[0] + mod[2:3] * o
    x1_ref[0] = x1
    h2 = _rms(x1) * g2_ref[...] * (1.0 + mod[4:5]) + mod[3:4]
    h2_ref[0] = h2
    logits = _dot(h2, rw_ref[...], precision=HIGHEST) + rb_ref[...]
    lane = lax.broadcasted_iota(jnp.int32, logits.shape, 1)
    topi = jnp.zeros(logits.shape, jnp.int32)
    topv = jnp.zeros(logits.shape, F32)
    top_max = None
    for kk in range(TOP_K):
        m = jnp.max(logits, axis=-1, keepdims=True)
        idx = jnp.min(jnp.where(logits == m, lane, LANES), axis=-1, keepdims=True)
        top_max = m if top_max is None else top_max
        topi = jnp.where(lane == kk, idx, topi)
        topv = jnp.where(lane == kk, jnp.exp(m - top_max), topv)
        logits = jnp.where(lane == idx, -jnp.inf, logits)
    topi_ref[0] = topi
    topg_ref[0] = topv / jnp.sum(topv, axis=-1, keepdims=True)


def _outproj(xa, mods, o_na, ry, bv, gate, gy, p_gd, pr, n_ctx):
    b, l, d = xa.shape
    nt = l // TM
    na_w = o_na.shape[-1]
    n_heads = ry[0].shape[1]
    rw_w = n_heads * HEAD_DIM
    kern = functools.partial(_outproj_kernel, na_w=na_w, rw_w=rw_w)
    tok = lambda w: pl.BlockSpec((1, TM, w), lambda i, t: (i, t, 0))
    head_major = pl.BlockSpec((1, n_heads, TM, HEAD_DIM), lambda i, t: (i, 0, t, 0))
    params = [pr['rw_ln_w'].reshape(1, rw_w), pr['rw_ln_b'].reshape(1, rw_w), pr['gd_norm_g'].reshape(1, HEAD_DIM),
              pr['w_out_bf'], pr['norm_ffn_g'], pr['router_w_p'], pr['router_b_p']]
    return pl.pallas_call(
        kern,
        grid=(b, nt),
        in_specs=[tok(d), pl.BlockSpec((1, 6, d), _mod_index(n_ctx // TM, b)), tok(na_w),
                  head_major, head_major, tok(rw_w), tok(rw_w), head_major, head_major,
                  pl.BlockSpec((1, TM, rw_w), lambda i, t: (i, t, 3))]
                 + [_full_spec(a) for a in params],
        out_specs=[tok(d), tok(d), tok(LANES), tok(LANES)],
        out_shape=[jax.ShapeDtypeStruct((b, l, d), F32), jax.ShapeDtypeStruct((b, l, d), F32),
                   jax.ShapeDtypeStruct((b, l, LANES), jnp.int32), jax.ShapeDtypeStruct((b, l, LANES), F32)],
        compiler_params=_cparams(("arbitrary", "arbitrary")),
        name="outproj_router",
    )(xa, mods, o_na, ry[0], ry[1], bv, gate, gy[0], gy[1], p_gd, *params)


MOE_BM = 256
MOE_TC = 128


def _row_copy(src_hbm, row, dst, slot, r, sem):
    return pltpu.make_async_copy(src_hbm.at[pl.ds(row, 1), :], dst.at[slot, pl.ds(r, 1), :], sem.at[slot])


def _gather_start(src_hbm, idx_ref, dst, slot, sem, n_rows):
    def body(r, carry):
        _row_copy(src_hbm, idx_ref[0, 0, r], dst, slot, r, sem).start()
        return carry
    lax.fori_loop(0, n_rows, body, 0)


def _gather_wait(src_hbm, dst, slot, sem, n_rows):
    def body(r, carry):
        _row_copy(src_hbm, 0, dst, slot, r, sem).wait()
        return carry
    lax.fori_loop(0, n_rows, body, 0)


def _pipelined_gather(src_hbm, idx_ref, idx_next_ref, dst, sem, n_rows):
    i = pl.program_id(0)
    slot = i % 2

    @pl.when(i == 0)
    def _():
        _gather_start(src_hbm, idx_ref, dst, 0, sem, n_rows)

    @pl.when(i + 1 < pl.num_programs(0))
    def _():
        _gather_start(src_hbm, idx_next_ref, dst, 1 - slot, sem, n_rows)

    _gather_wait(src_hbm, dst, slot, sem, n_rows)
    return slot


def _expert_kernel(blk_e_ref, idx_ref, idx_next_ref, h_hbm, wgu_ref, bgu_ref, wd_ref, bd_ref, y_ref, xbuf, sem):
    del blk_e_ref
    slot = _pipelined_gather(h_hbm, idx_ref, idx_next_ref, xbuf, sem, MOE_BM)
    x = xbuf[slot].astype(BF16)
    d_exp = wd_ref.shape[1]
    gu = _dot(x, wgu_ref[0]) + bgu_ref[0]
    gate = jnp.minimum(gu[:, :d_exp], SWIGLU_LIMIT)
    lin = jnp.clip(gu[:, d_exp:], -SWIGLU_LIMIT, SWIGLU_LIMIT)
    act = gate * jax.nn.sigmoid(SWIGLU_ALPHA * gate) * (lin + 1.0)
    y_ref[...] = _dot(act.astype(BF16), wd_ref[0]) + bd_ref[0]


def _expert_blocks(h2, blk_e, src_tok, w_gu, b_gu, w_down, b_down):
    t, d = h2.shape
    n_blocks = blk_e.shape[0]
    n_exp, _, d_gu = w_gu.shape
    idx = src_tok.reshape(n_blocks, 1, MOE_BM)
    idx_spec = lambda f: pl.BlockSpec((1, 1, MOE_BM), f, memory_space=pltpu.SMEM)
    grid_spec = pltpu.PrefetchScalarGridSpec(
        num_scalar_prefetch=1,
        grid=(n_blocks,),
        in_specs=[idx_spec(lambda i, be: (i, 0, 0)),
                  idx_spec(lambda i, be: (jnp.minimum(i + 1, n_blocks - 1), 0, 0)),
                  pl.BlockSpec(memory_space=pl.ANY),
                  pl.BlockSpec((1, d, d_gu), lambda i, be: (be[i], 0, 0)),
                  pl.BlockSpec((1, 1, d_gu), lambda i, be: (be[i], 0, 0)),
                  pl.BlockSpec((1, d_gu // 2, d), lambda i, be: (be[i], 0, 0)),
                  pl.BlockSpec((1, 1, d), lambda i, be: (be[i], 0, 0))],
        out_specs=pl.BlockSpec((MOE_BM, d), lambda i, be: (i, 0)),
        scratch_shapes=[pltpu.VMEM((2, MOE_BM, d), F32), pltpu.SemaphoreType.DMA((2,))])
    return pl.pallas_call(
        _expert_kernel,
        grid_spec=grid_spec,
        out_shape=jax.ShapeDtypeStruct((n_blocks * MOE_BM, d), F32),
        compiler_params=_cparams(("arbitrary",)),
        name="moe_experts",
    )(blk_e, idx, idx, h2, w_gu, b_gu.reshape(n_exp, 1, d_gu), w_down, b_down.reshape(n_exp, 1, d))


def _combine_kernel(pos_ref, pos_next_ref, y_hbm, x_ref, mod_ref, gate_ref, o_ref, ybuf, sem):
    slot = _pipelined_gather(y_hbm, pos_ref, pos_next_ref, ybuf, sem, TOP_K * MOE_TC)
    gates = gate_ref[...]
    acc = jnp.zeros(x_ref.shape, F32)
    for k in range(TOP_K):
        acc = acc + gates[:, k:k + 1] * ybuf[slot, k * MOE_TC:(k + 1) * MOE_TC, :]
    o_ref[...] = x_ref[...] + mod_ref[0, 5:6, :] * acc


def _combine(yb, pos, x1, mods, topg, n_batch, l, n_ctx):
    t, d = x1.shape
    nt = t // MOE_TC
    per_sample = l // MOE_TC
    n_ctx_tiles = n_ctx // MOE_TC
    idx = pos.reshape(nt, MOE_TC, TOP_K).transpose(0, 2, 1).reshape(nt, 1, TOP_K * MOE_TC)
    idx_spec = lambda f: pl.BlockSpec((1, 1, TOP_K * MOE_TC), f, memory_space=pltpu.SMEM)

    def mod_index(i):
        return (jnp.where(i % per_sample < n_ctx_tiles, n_batch, i // per_sample), 0, 0)

    return pl.pallas_call(
        _combine_kernel,
        grid=(nt,),
        in_specs=[idx_spec(lambda i: (i, 0, 0)),
                  idx_spec(lambda i: (jnp.minimum(i + 1, nt - 1), 0, 0)),
                  pl.BlockSpec(memory_space=pl.ANY),
                  pl.BlockSpec((MOE_TC, d), lambda i: (i, 0)),
                  pl.BlockSpec((1, 6, d), mod_index),
                  pl.BlockSpec((MOE_TC, LANES), lambda i: (i, 0))],
        out_specs=pl.BlockSpec((MOE_TC, d), lambda i: (i, 0)),
        out_shape=jax.ShapeDtypeStruct((t, d), F32),
        scratch_shapes=[pltpu.VMEM((2, TOP_K * MOE_TC, d), F32), pltpu.SemaphoreType.DMA((2,))],
        compiler_params=_cparams(("arbitrary",)),
        name="moe_combine",
    )(idx, idx, yb, x1, mods, topg)


def _route(topi, n_exp):
    tk = topi.shape[0] * TOP_K
    flat_e = topi.reshape(tk)
    order = jnp.argsort(flat_e)
    se = flat_e[order]
    counts = jnp.bincount(flat_e, length=n_exp)
    padded = (counts + MOE_BM - 1) // MOE_BM * MOE_BM
    pad_end = jnp.cumsum(padded)
    pad_start = pad_end - padded
    start = jnp.cumsum(counts) - counts
    dest = (pad_start[se] + jnp.arange(tk, dtype=jnp.int32) - start[se]).astype(jnp.int32)
    n_pad = _round_up(tk, MOE_BM) + n_exp * MOE_BM
    n_blocks = n_pad // MOE_BM
    src_tok = jnp.zeros((n_pad,), jnp.int32).at[dest].set((order // TOP_K).astype(jnp.int32))
    pos = jnp.zeros((tk,), jnp.int32).at[order].set(dest).reshape(-1, TOP_K)
    blk_e = jnp.minimum(jnp.searchsorted(pad_end, jnp.arange(n_blocks) * MOE_BM, side='right'),
                        n_exp - 1).astype(jnp.int32)
    return src_tok, pos, blk_e


def _channel_sublayer(x1, h2, topi, topg, mods, pr, n_ctx):
    b, l, d = x1.shape
    t = b * l
    n_exp = pr['moe_w_gu_bf'].shape[0]
    src_tok, pos, blk_e = _route(topi.reshape(t, LANES)[:, :TOP_K], n_exp)
    yb = _expert_blocks(h2.reshape(t, d), blk_e, src_tok, pr['moe_w_gu_bf'], pr['moe_b_gu'],
                        pr['moe_w_down_bf'], pr['moe_b_down'])
    x2 = _combine(yb, pos, x1.reshape(t, d), mods, topg.reshape(t, LANES), b, l, n_ctx)
    return x2.reshape(b, l, d)


def _pad_cols(w, n):
    return jnp.pad(w, ((0, 0), (0, n - w.shape[1])))


def _round_up(n, m):
    return -(-n // m) * m


def _prep_layer_params(pr):
    d = pr['w_in'].shape[0]
    na_w = d // 2
    rw_w = d // 4
    gd_w = d - na_w - rw_w
    na_cols = 3 * na_w
    rw_cols = 3 * rw_w + RW_DECAY_LORA + RW_ICLR_LORA + RW_GATE_LORA
    rw_pad = _round_up(rw_cols, LANES)
    gd_cols = pr['w_in'].shape[1] - na_cols - rw_cols
    gd_pad = _round_up(gd_cols, LANES)
    w_in = pr['w_in']
    out = dict(pr)
    out['w_in'] = jnp.concatenate([w_in[:, :na_cols],
                                   _pad_cols(w_in[:, na_cols:na_cols + rw_cols], rw_pad),
                                   _pad_cols(w_in[:, na_cols + rw_cols:], gd_pad)], axis=1).astype(BF16)
    out['rw_w'] = rw_pad
    out['gd_w'] = gd_pad
    n_heads = na_w // HEAD_DIM
    out['qkg'] = jnp.stack([jnp.tile(pr['na_q_gain'], n_heads) * HEAD_DIM ** -0.5,
                            jnp.tile(pr['na_k_gain'], n_heads)])
    head_of = np.arange(na_w) // HEAD_DIM
    out['avg'] = jnp.asarray((head_of[:, None] == head_of[None, :]) / HEAD_DIM, BF16)
    out['norm_mix_g'] = pr['norm_mix_g'].reshape(1, d)
    out['norm_ffn_g'] = pr['norm_ffn_g'].reshape(1, d)
    out['rw_mu_p'] = jnp.pad(pr['rw_mu'], (0, rw_pad - rw_cols)).reshape(1, rw_pad)
    o1, o2 = RW_DECAY_LORA, RW_DECAY_LORA + RW_ICLR_LORA
    out['rw_wup_p'] = jnp.pad(pr['rw_w_up'], ((0, 0), (0, LANES - o1), (0, 0)))
    out['rw_aup_p'] = jnp.pad(pr['rw_a_up'], ((0, 0), (o1, LANES - o2), (0, 0)))
    out['rw_gup_p'] = jnp.pad(pr['rw_g_up'], ((o2, LANES - o2 - RW_GATE_LORA), (0, 0)))
    head_of = np.arange(rw_w) // HEAD_DIM
    out['ones_bd'] = jnp.asarray(head_of[:, None] == head_of[None, :], F32)
    out['gd_width'] = gd_w
    n_ab = pr['gd_A_log'].size
    out['gd_alog_p'] = jnp.pad(pr['gd_A_log'].reshape(-1), (0, LANES - n_ab)).reshape(1, LANES)
    out['gd_dtb_p'] = jnp.pad(pr['gd_dt_bias'].reshape(-1), (0, LANES - n_ab)).reshape(1, LANES)
    out['w_out_bf'] = pr['w_out'].astype(BF16)
    n_exp = pr['moe_router_w'].shape[1]
    out['router_w_p'] = _pad_cols(pr['moe_router_w'], LANES)
    out['router_b_p'] = jnp.concatenate([pr['moe_router_b'],
                                         jnp.full((LANES - n_exp,), MASKED, F32)]).reshape(1, LANES)
    out['moe_w_gu_bf'] = pr['moe_w_gu'].astype(BF16)
    out['moe_w_down_bf'] = pr['moe_w_down'].astype(BF16)
    return out


def _mixing_sublayer(xa, mods, pr, n_ctx):
    b, l, d = xa.shape
    na_w = d // 2
    p_na, p_rw, p_gd = _inproj(xa, mods, pr['norm_mix_g'], pr['w_in'], pr['qkg'], pr['avg'], n_ctx,
                               na_w, pr['rw_w'], pr['gd_w'])
    o_na = _na_attention(p_na, pr['na_bias'], n_ctx, na_w)
    rw2, ry1, rm, rn, bv, gate = _rwkv_prepare(p_rw, pr, n_ctx)
    ry = _scan(rw2, ry1, rm, rn, n_ctx)
    gy = _scan(*_gdn_prepare(p_gd, pr, n_ctx), n_ctx)
    return _outproj(xa, mods, o_na, ry, bv, gate, gy, p_gd, pr, n_ctx)


_LAYER_PARAMS = ('norm_mix_g', 'norm_ffn_g', 'w_in', 'w_out', 'na_q_gain', 'na_k_gain', 'na_rpb',
                 'rw_mu', 'rw_w0', 'rw_w_up', 'rw_a0', 'rw_a_up', 'rw_g_up', 'rw_k_k', 'rw_k_a', 'rw_r_k',
                 'rw_ln_w', 'rw_ln_b', 'gd_conv_w', 'gd_A_log', 'gd_dt_bias', 'gd_norm_g',
                 'moe_router_w', 'moe_router_b', 'moe_w_gu', 'moe_b_gu', 'moe_w_down', 'moe_b_down')


def kernel(x, c, ctx, c_ctx, ada_w, ada_b, norm_mix_g, norm_ffn_g, w_in, w_out, na_q_gain, na_k_gain, na_rpb, rw_mu, rw_w0, rw_w_up, rw_a0, rw_a_up, rw_g_up, rw_k_k, rw_k_a, rw_r_k, rw_ln_w, rw_ln_b, gd_conv_w, gd_A_log, gd_dt_bias, gd_norm_g, moe_router_w, moe_router_b, moe_w_gu, moe_b_gu, moe_w_down, moe_b_down):
    stacked = dict(zip(_LAYER_PARAMS, (norm_mix_g, norm_ffn_g, w_in, w_out, na_q_gain, na_k_gain, na_rpb,
                                       rw_mu, rw_w0, rw_w_up, rw_a0, rw_a_up, rw_g_up, rw_k_k, rw_k_a, rw_r_k,
                                       rw_ln_w, rw_ln_b, gd_conv_w, gd_A_log, gd_dt_bias, gd_norm_g,
                                       moe_router_w, moe_router_b, moe_w_gu, moe_b_gu, moe_w_down, moe_b_down)))
    b, s, d = x.shape
    n_ctx = ctx.shape[1]
    depth = ada_w.shape[0]
    assert b + 1 <= 8 and n_ctx % TM == 0 and s % TM == 0 and s % GRID_W == 0
    cvec = jnp.zeros((8, d), F32).at[:b].set(c).at[b].set(c_ctx)
    mods_all = _adaln(cvec, ada_w, ada_b).reshape(depth, 8, 6, d)[:, :b + 1]
    xa = jnp.concatenate([ctx, x], axis=1)
    for layer in range(depth):
        pr = _prep_layer_params({name: value[layer] for name, value in stacked.items()})
        pr['na_bias'] = _na_bias_tables(pr['na_rpb'], s // GRID_W)
        mods = mods_all[layer]
        x1, h2, topi, topg = _mixing_sublayer(xa, mods, pr, n_ctx)
        xa = _channel_sublayer(x1, h2, topi, topg, mods, pr, n_ctx)
    return xa[:, n_ctx:]
```

```python
import functools
import math

import numpy as np
import jax
import jax.numpy as jnp
from jax import lax
from jax.experimental import pallas as pl
from jax.experimental.pallas import tpu as pltpu

F32 = jnp.float32
BF16 = jnp.bfloat16
HIGHEST = lax.Precision.HIGHEST

GRID_W = 64
NORM_EPS = 1e-6
HEAD_DIM = 64
NA_WIN_ROWS = 8
NA_WIN_COLS = 16
RW_DECAY_LORA = 32
RW_ICLR_LORA = 32
RW_GATE_LORA = 64
RW_GN_EPS = 64e-5
GD_CONV = 5
TOP_K = 4
SWIGLU_LIMIT = 7.0
SWIGLU_ALPHA = 1.702

LANES = 128
TM = 256
NA_QROWS = 4
NA_KROWS = NA_QROWS + NA_WIN_ROWS
CHUNK = 64
MASKED = -1e30
VMEM_LIMIT = 48 * 1024 * 1024


def _cparams(sem):
    return pltpu.CompilerParams(dimension_semantics=sem, vmem_limit_bytes=VMEM_LIMIT)


def _dot(a, b, **kw):
    return jnp.dot(a, b, preferred_element_type=F32, **kw)


def _dot_nt(a, b, **kw):
    return lax.dot_general(a, b, (((1,), (1,)), ((), ())), preferred_element_type=F32, **kw)


def _dot_tn(a, b, **kw):
    return lax.dot_general(a, b, (((0,), (0,)), ((), ())), preferred_element_type=F32, **kw)


def _adaln_kernel(c_ref, w_ref, b_ref, o_ref):
    c = c_ref[...]
    s = c * jax.nn.sigmoid(c)
    o_ref[0] = _dot(s.astype(BF16), w_ref[0].astype(BF16)) + b_ref[0]


def _adaln(cvec, ada_w, ada_b):
    depth, d, n = ada_w.shape
    tn = n // 4
    return pl.pallas_call(
        _adaln_kernel,
        grid=(depth, n // tn),
        in_specs=[pl.BlockSpec((8, d), lambda l, j: (0, 0)),
                  pl.BlockSpec((1, d, tn), lambda l, j: (l, 0, j)),
                  pl.BlockSpec((1, 1, tn), lambda l, j: (l, 0, j))],
        out_specs=pl.BlockSpec((1, 8, tn), lambda l, j: (l, 0, j)),
        out_shape=jax.ShapeDtypeStruct((depth, 8, n), F32),
        compiler_params=_cparams(("arbitrary", "arbitrary")),
        name="adaln",
    )(cvec, ada_w, ada_b.reshape(depth, 1, n))


def _mod_index(n_ctx_tiles, n_batch):
    return lambda b, t: (jnp.where(t < n_ctx_tiles, n_batch, b), 0, 0)


def _rms(x):
    return x * lax.rsqrt(jnp.mean(x * x, axis=-1, keepdims=True) + NORM_EPS)


def _inproj_kernel(x_ref, mod_ref, g_ref, w_ref, qkg_ref, avg_ref, na_ref, rw_ref, gd_ref, *, na_w, rw_w):
    mod = mod_ref[0]
    h = _rms(x_ref[0]) * g_ref[...]
    hb = (h * (1.0 + mod[1:2]) + mod[0:1]).astype(BF16)
    pa = _dot(hb, w_ref[:, 0:3 * na_w])
    avg = avg_ref[...]
    qkg = qkg_ref[...]
    q = pa[:, 0:na_w]
    k = pa[:, na_w:2 * na_w]
    qn = q * lax.rsqrt(_dot(q * q, avg) + NORM_EPS) * qkg[0:1]
    kn = k * lax.rsqrt(_dot(k * k, avg) + NORM_EPS) * qkg[1:2]
    na_ref[0, :, 0:na_w] = qn.astype(BF16)
    na_ref[0, :, na_w:2 * na_w] = kn.astype(BF16)
    na_ref[0, :, 2 * na_w:3 * na_w] = pa[:, 2 * na_w:3 * na_w].astype(BF16)
    rw_ref[0] = _dot(hb, w_ref[:, 3 * na_w:3 * na_w + rw_w])
    gd_ref[0] = _dot(hb, w_ref[:, 3 * na_w + rw_w:])


def _inproj(xa, mods, g, w_all, qkg, avg, n_ctx, na_w, rw_w, gd_w):
    b, l, d = xa.shape
    nt = l // TM
    kern = functools.partial(_inproj_kernel, na_w=na_w, rw_w=rw_w)
    return pl.pallas_call(
        kern,
        grid=(b, nt),
        in_specs=[pl.BlockSpec((1, TM, d), lambda i, t: (i, t, 0)),
                  pl.BlockSpec((1, 6, d), _mod_index(n_ctx // TM, b)),
                  pl.BlockSpec((1, d), lambda i, t: (0, 0)),
                  pl.BlockSpec(w_all.shape, lambda i, t: (0, 0)),
                  pl.BlockSpec(qkg.shape, lambda i, t: (0, 0)),
                  pl.BlockSpec(avg.shape, lambda i, t: (0, 0))],
        out_specs=[pl.BlockSpec((1, TM, 3 * na_w), lambda i, t: (i, t, 0)),
                   pl.BlockSpec((1, TM, rw_w), lambda i, t: (i, t, 0)),
                   pl.BlockSpec((1, TM, gd_w), lambda i, t: (i, t, 0))],
        out_shape=[jax.ShapeDtypeStruct((b, l, 3 * na_w), BF16),
                   jax.ShapeDtypeStruct((b, l, rw_w), F32),
                   jax.ShapeDtypeStruct((b, l, gd_w), F32)],
        compiler_params=_cparams(("arbitrary", "arbitrary")),
        name="inproj",
    )(xa, mods, g, w_all, qkg, avg)


def _na_bias_tables(rpb, rows):
    n_heads = rpb.shape[0]
    nq, nk = NA_QROWS * GRID_W, NA_KROWS * GRID_W
    qi = np.arange(nq)
    ki = np.arange(nk)
    qc = (qi % GRID_W)[:, None]
    kc = (ki % GRID_W)[None, :]
    c0 = np.clip(qc - NA_WIN_COLS // 2, 0, GRID_W - NA_WIN_COLS)
    col_ok = (kc >= c0) & (kc < c0 + NA_WIN_COLS)
    dc = np.clip(kc - qc + NA_WIN_COLS - 1, 0, 2 * NA_WIN_COLS - 2)
    tables = []
    for r0 in (0, NA_QROWS, rows - NA_QROWS):
        ks = int(np.clip(r0 - NA_WIN_ROWS // 2, 0, rows - NA_KROWS))
        qr = (r0 + qi // GRID_W)[:, None]
        kr = (ks + ki // GRID_W)[None, :]
        s0 = np.clip(qr - NA_WIN_ROWS // 2, 0, rows - NA_WIN_ROWS)
        ok = col_ok & (kr >= s0) & (kr < s0 + NA_WIN_ROWS)
        dr = np.clip(kr - qr + NA_WIN_ROWS - 1, 0, 2 * NA_WIN_ROWS - 2)
        tables.append(jnp.where(ok[None], rpb[:, dr, dc], MASKED))
    tables.append(jnp.full((n_heads, nq, nk), MASKED, F32))
    return jnp.stack(tables).reshape(4, n_heads // 2, 2, nq, nk)


def _na_kernel(q_ref, k_ref, v_ref, bias_ref, o_ref, *, n_ctx, rows):
    j = pl.program_id(2)
    nk = NA_KROWS * GRID_W
    ks_row = jnp.clip((j - 1) * NA_QROWS - NA_WIN_ROWS // 2, 0, rows - NA_KROWS)
    kstart = pl.multiple_of(n_ctx + ks_row * GRID_W, GRID_W)
    q = q_ref[0]
    kl = k_ref[0, pl.ds(kstart, nk), :]
    vl = v_ref[0, pl.ds(kstart, nk), :]
    kc = k_ref[0, 0:n_ctx, :]
    vc = v_ref[0, 0:n_ctx, :]
    lane = lax.broadcasted_iota(jnp.int32, q.shape, 1)
    outs = []
    for h in range(2):
        in_head = (lane >= h * HEAD_DIM) & (lane < (h + 1) * HEAD_DIM)
        qh = jnp.where(in_head, q, jnp.zeros_like(q))
        s_loc = _dot_nt(qh, kl) + bias_ref[0, 0, h]
        s_ctx = _dot_nt(qh, kc)
        m = jnp.maximum(jnp.max(s_loc, axis=-1, keepdims=True), jnp.max(s_ctx, axis=-1, keepdims=True))
        p_loc = jnp.exp(s_loc - m)
        p_ctx = jnp.exp(s_ctx - m)
        den = jnp.sum(p_loc, axis=-1, keepdims=True) + jnp.sum(p_ctx, axis=-1, keepdims=True)
        o = _dot(p_loc.astype(BF16), vl) + _dot(p_ctx.astype(BF16), vc)
        outs.append(o / den)
    o_ref[0] = jnp.where(lane < HEAD_DIM, outs[0], outs[1]).astype(o_ref.dtype)


def _na_attention(p_na, bias, n_ctx, na_w):
    b, l, _ = p_na.shape
    rows = (l - n_ctx) // GRID_W
    nq = NA_QROWS * GRID_W
    assert n_ctx == nq and rows >= NA_KROWS and rows % NA_QROWS == 0
    n_pairs = na_w // LANES
    nblk = l // nq
    kern = functools.partial(_na_kernel, n_ctx=n_ctx, rows=rows)

    def bias_index(i, hp, j):
        pat = jnp.where(j == 0, 3, jnp.where(j == 1, 0, jnp.where(j == nblk - 1, 2, 1)))
        return (pat, hp, 0, 0, 0)

    return pl.pallas_call(
        kern,
        grid=(b, n_pairs, nblk),
        in_specs=[pl.BlockSpec((1, nq, LANES), lambda i, hp, j: (i, j, hp)),
                  pl.BlockSpec((1, l, LANES), lambda i, hp, j: (i, 0, n_pairs + hp)),
                  pl.BlockSpec((1, l, LANES), lambda i, hp, j: (i, 0, 2 * n_pairs + hp)),
                  pl.BlockSpec((1, 1, 2) + bias.shape[3:], bias_index)],
        out_specs=pl.BlockSpec((1, nq, LANES), lambda i, hp, j: (i, j, hp)),
        out_shape=jax.ShapeDtypeStruct((b, l, na_w), BF16),
        compiler_params=_cparams(("arbitrary", "arbitrary", "arbitrary")),
        name="na_attention",
    )(p_na, p_na, p_na, bias)


def _tri_masks(reverse):
    t = lax.broadcasted_iota(jnp.int32, (CHUNK, CHUNK), 0)
    s = lax.broadcasted_iota(jnp.int32, (CHUNK, CHUNK), 1)
    earlier_or_same = (s >= t) if reverse else (s <= t)
    earlier = (s > t) if reverse else (s < t)
    return earlier_or_same, earlier, t == s


def _each(fn, *lists):
    return [fn(*args) for args in zip(*lists)]


def _unit_tri_inverse(a, eye):
    assert CHUNK == 64
    x1 = [-u for u in a]
    x2 = _each(_dot, x1, x1)
    x4 = _each(_dot, x2, x2)
    x3 = _each(_dot, x1, x2)
    x8 = _each(_dot, x4, x4)
    p1 = _each(lambda u1, u2, u3: eye + u1 + u2 + u3, x1, x2, x3)
    x12 = _each(_dot, x4, x8)
    x16 = _each(_dot, x8, x8)
    p2 = _each(lambda u4, u8, u12: eye + u4 + u8 + u12, x4, x8, x12)
    p12 = _each(_dot, p1, p2)
    x32 = _each(_dot, x16, x16)
    x48 = _each(_dot, x16, x32)
    p3 = _each(lambda u16, u32, u48: eye + u16 + u32 + u48, x16, x32, x48)
    return _each(_dot, p12, p3)


def _head_lanes(width):
    lane = lax.broadcasted_iota(jnp.int32, (1, width), 1)
    return [(lane >= h * HEAD_DIM) & (lane < (h + 1) * HEAD_DIM) for h in range(width // HEAD_DIM)]


def _stack_heads(x, heads):
    return jnp.concatenate([jnp.where(in_head, x, 0.0) for in_head in heads], axis=0)


def _stacked_masks(n, reverse):
    row = lax.broadcasted_iota(jnp.int32, (n, n), 0)
    col = lax.broadcasted_iota(jnp.int32, (n, n), 1)
    t, s = row % CHUNK, col % CHUNK
    same_head = (row // CHUNK) == (col // CHUNK)
    earlier = ((s > t) if reverse else (s < t)) & same_head
    earlier_or_same = earlier | (row == col)
    return earlier, earlier_or_same, row == col


def _chunk_matrices(a_ka, a_kk, a_ra, a_rk, ks, rs, vs, kds, ads, eye):
    t_inv = _unit_tri_inverse(a_ka, eye.astype(F32))
    av = _each(_dot, a_kk, vs)
    w1 = _each(_dot, t_inv, ks)
    u0 = _each(_dot, t_inv, av)
    rkv = _each(_dot, a_rk, vs)
    w2 = _each(lambda r, a, w: r - _dot(a, w), rs, a_ra, w1)
    y1 = _each(lambda y, a, u: y - _dot(a, u), rkv, a_ra, u0)
    mm = _each(_dot_tn, ads, w1)
    nn = _each(lambda kd, v, ad, u: _dot_tn(kd, v) - _dot_tn(ad, u), kds, vs, ads, u0)
    return list(zip(w2, y1, mm, nn))


def _store_chunk(refs, d, c0, n_heads, w2, y1, mm, nn, p_end):
    w2_ref, y1_ref, m_ref, n_ref = refs
    rows = pl.ds(c0, CHUNK)
    eye = (lax.broadcasted_iota(jnp.int32, (HEAD_DIM, HEAD_DIM), 0)
           == lax.broadcasted_iota(jnp.int32, (HEAD_DIM, HEAD_DIM), 1))
    for h in range(n_heads):
        lo, hi = h * HEAD_DIM, (h + 1) * HEAD_DIM
        r_lo, r_hi = h * CHUNK, (h + 1) * CHUNK
        w2_ref[d, 0, h, rows, :] = w2[r_lo:r_hi, lo:hi]
        y1_ref[d, 0, h, rows, :] = y1[r_lo:r_hi, lo:hi]
        m_ref[d, 0, h, rows, :] = jnp.where(eye, p_end[:, lo:hi], 0.0) - mm[lo:hi, lo:hi]
        n_ref[d, 0, h, rows, :] = nn[lo:hi, lo:hi]


def _scan_out_specs(b, n_heads, l):
    shape = jax.ShapeDtypeStruct((2, b, n_heads, l, HEAD_DIM), F32)
    spec = pl.BlockSpec((2, 1, n_heads, TM, HEAD_DIM), lambda i, t: (0, i, 0, t, 0))
    return [shape] * 4, [spec] * 4


def _seq_edges(t, n_ctx_tiles, n_tiles):
    first = (t == 0) | (t == n_ctx_tiles)
    last = (t == n_ctx_tiles - 1) | (t == n_tiles - 1)
    return first, last


def _fill_halo(ext_ref, x_ref, prev_ref, next_ref, first, last):
    ext_ref[0:8, :] = jnp.where(first, 0.0, prev_ref[0])
    ext_ref[8:8 + TM, :] = x_ref[0]
    ext_ref[8 + TM:16 + TM, :] = jnp.where(last, 0.0, next_ref[0])


def _halo_specs(width, n_tiles):
    per = TM // 8
    return [pl.BlockSpec((1, TM, width), lambda i, t: (i, t, 0)),
            pl.BlockSpec((1, 8, width), lambda i, t: (i, jnp.maximum(t * per - 1, 0), 0)),
            pl.BlockSpec((1, 8, width), lambda i, t: (i, jnp.minimum((t + 1) * per, n_tiles * per - 1), 0))]


def _full_spec(a):
    return pl.BlockSpec(a.shape, lambda i, t: (0,) * a.ndim)


def _rwkv_prep_kernel(x_ref, prev_ref, next_ref, mu_ref, w0_ref, wup_ref, a0_ref, aup_ref, gup_ref,
                      kk_ref, ka_ref, rk_ref, ones_ref,
                      w2_ref, y1_ref, m_ref, n_ref, bv_ref, gate_ref,
                      ext_ref, tok_ref, *, n_ctx_tiles, n_tiles, width):
    t = pl.program_id(1)
    first, last = _seq_edges(t, n_ctx_tiles, n_tiles)
    _fill_halo(ext_ref, x_ref, prev_ref, next_ref, first, last)
    p = ext_ref[8:8 + TM, :]
    prev = ext_ref[7:7 + TM, :]
    nxt = ext_ref[9:9 + TM, :]
    ps = p + mu_ref[...] * (0.5 * (prev + nxt) - p)
    r = ps[:, 0:width]
    k = ps[:, width:2 * width]
    v = ps[:, 2 * width:3 * width]
    lo = ps[:, 3 * width:3 * width + LANES]
    ones_bd = ones_ref[...]
    gate_ref[0] = _dot(jax.nn.sigmoid(lo), gup_ref[...])
    kq = k * kk_ref[...]
    kk = kq * lax.rsqrt(_dot(kq * kq, ones_bd, precision=HIGHEST) + NORM_EPS)
    tanh_lo = jnp.tanh(lo)
    tok_ref[0] = r
    tok_ref[1] = kk
    tok_ref[2] = v
    k_sum = jnp.zeros_like(k)
    for d in range(2):
        z = w0_ref[d:d + 1, :] + _dot(tanh_lo, wup_ref[d])
        w_log = -jax.nn.softplus(-z) - 0.5
        a = jax.nn.sigmoid(a0_ref[d:d + 1, :] + _dot(lo, aup_ref[d]))
        k_dir = k * (1.0 + (a - 1.0) * ka_ref[...])
        k_sum = k_sum + k_dir
        tok_ref[3 + 3 * d] = -jnp.exp(w_log)
        tok_ref[4 + 3 * d] = kk * a
        tok_ref[5 + 3 * d] = k_dir
    bonus = _dot(r * k_sum * rk_ref[...], ones_bd, precision=HIGHEST)
    bv_ref[0] = bonus * v

    heads = _head_lanes(width)
    out_refs = (w2_ref, y1_ref, m_ref, n_ref)

    def chunk_body(c, carry):
        c0 = pl.multiple_of(c * CHUNK, CHUNK)
        rows = pl.ds(c0, CHUNK)
        r_c, kk_c, v_c = tok_ref[0, rows, :], tok_ref[1, rows, :], tok_ref[2, rows, :]
        vs = _stack_heads(v_c, heads)
        args = [[] for _ in range(9)]
        p_ends = []
        for d in range(2):
            logw, alpha, k_c = tok_ref[3 + 3 * d, rows, :], tok_ref[4 + 3 * d, rows, :], tok_ref[5 + 3 * d, rows, :]
            incl, _, _ = _tri_masks(reverse=(d == 1))
            ci = _dot(incl.astype(F32), logw, precision=HIGHEST)
            e_ci, e_nci, e_ce = jnp.exp(ci), jnp.exp(-ci), jnp.exp(ci - logw)
            p_end = e_ci[0:1, :] if d == 1 else e_ci[CHUNK - 1:CHUNK, :]
            ks, rs, khs, ahs = (_stack_heads(u, heads) for u in (kk_c * e_ce, r_c * e_ci, k_c * e_nci, alpha * e_nci))
            earlier, earlier_or_same, eye = _stacked_masks(ks.shape[0], reverse=(d == 1))
            a_ka = jnp.where(earlier, _dot_nt(ks, ahs), 0.0)
            a_kk = jnp.where(earlier, _dot_nt(ks, khs), 0.0)
            a_ra = jnp.where(earlier_or_same, _dot_nt(rs, ahs), 0.0)
            a_rk = jnp.where(earlier_or_same, _dot_nt(rs, khs), 0.0)
            for slot, value in zip(args, (a_ka, a_kk, a_ra, a_rk, ks, rs, vs, khs * p_end, ahs * p_end)):
                slot.append(value)
            p_ends.append(p_end)
        for d, res in enumerate(_chunk_matrices(*args, eye)):
            _store_chunk(out_refs, d, c0, len(heads), *res, p_ends[d])
        return carry

    lax.fori_loop(0, TM // CHUNK, chunk_body, 0)


def _rwkv_prepare(p_rw, pr, n_ctx):
    b, l, pw = p_rw.shape
    width = pr['rw_k_k'].shape[-1]
    n_heads = width // HEAD_DIM
    nt = l // TM
    kern = functools.partial(_rwkv_prep_kernel, n_ctx_tiles=n_ctx // TM, n_tiles=nt, width=width)
    params = [pr['rw_mu_p'], pr['rw_w0'], pr['rw_wup_p'], pr['rw_a0'], pr['rw_aup_p'], pr['rw_gup_p'],
              pr['rw_k_k'].reshape(1, width), pr['rw_k_a'].reshape(1, width), pr['rw_r_k'].reshape(1, width),
              pr['ones_bd']]
    scan_shapes, scan_specs = _scan_out_specs(b, n_heads, l)
    tok_spec = pl.BlockSpec((1, TM, width), lambda i, t: (i, t, 0))
    return pl.pallas_call(
        kern,
        grid=(b, nt),
        in_specs=_halo_specs(pw, nt) + [_full_spec(a) for a in params],
        out_specs=scan_specs + [tok_spec, tok_spec],
        out_shape=scan_shapes + [jax.ShapeDtypeStruct((b, l, width), F32)] * 2,
        scratch_shapes=[pltpu.VMEM((TM + 16, pw), F32), pltpu.VMEM((9, TM, width), F32)],
        compiler_params=_cparams(("arbitrary", "arbitrary")),
        name="rwkv_prepare",
    )(p_rw, p_rw, p_rw, *params)


def _scan_kernel(w2f_ref, y1f_ref, mf_ref, nf_ref, w2b_ref, y1b_ref, mb_ref, nb_ref, yf_ref, yb_ref, h_ref):
    @pl.when(pl.program_id(0) == 0)
    def _():
        h_ref[...] = jnp.zeros_like(h_ref)

    n_batch, n_heads = h_ref.shape[1], h_ref.shape[2]
    for d, (w2_ref, y1_ref, m_ref, n_ref, y_ref) in enumerate(((w2f_ref, y1f_ref, mf_ref, nf_ref, yf_ref),
                                                               (w2b_ref, y1b_ref, mb_ref, nb_ref, yb_ref))):
        for i in range(n_batch):
            for h in range(n_heads):
                state = h_ref[d, i, h]
                y_ref[i, h] = _dot(w2_ref[0, i, h], state) + y1_ref[0, i, h]
                h_ref[d, i, h] = _dot(m_ref[0, i, h], state, precision=HIGHEST) + n_ref[0, i, h]


def _scan(w2, y1, m, n, n_ctx):
    _, b, n_heads, l, _ = w2.shape
    nc = l // CHUNK
    ncc = n_ctx // CHUNK

    def bwd_chunk(i):
        return jnp.where(i < ncc, ncc - 1 - i, nc - 1 + ncc - i)

    blk = (1, b, n_heads, CHUNK, HEAD_DIM)
    fwd = pl.BlockSpec(blk, lambda i: (0, 0, 0, i, 0))
    bwd = pl.BlockSpec(blk, lambda i: (1, 0, 0, bwd_chunk(i), 0))
    out_shape = jax.ShapeDtypeStruct((b, n_heads, l, HEAD_DIM), F32)
    return pl.pallas_call(
        _scan_kernel,
        grid=(nc,),
        in_specs=[fwd] * 4 + [bwd] * 4,
        out_specs=[pl.BlockSpec(blk[1:], lambda i: (0, 0, i, 0)),
                   pl.BlockSpec(blk[1:], lambda i: (0, 0, bwd_chunk(i), 0))],
        out_shape=[out_shape, out_shape],
        scratch_shapes=[pltpu.VMEM((2, b, n_heads, HEAD_DIM, HEAD_DIM), F32)],
        compiler_params=_cparams(("arbitrary",)),
        name="chunk_scan",
    )(w2, y1, m, n, w2, y1, m, n)


def _gdn_prep_kernel(x_ref, prev_ref, next_ref, cw_ref, alog_ref, dtb_ref, ones_ref,
                     w2_ref, y1_ref, m_ref, n_ref,
                     ext_ref, tok_ref, *, n_ctx_tiles, n_tiles, width):
    t = pl.program_id(1)
    first, last = _seq_edges(t, n_ctx_tiles, n_tiles)
    _fill_halo(ext_ref, x_ref, prev_ref, next_ref, first, last)
    half = GD_CONV // 2
    conv = jnp.zeros((TM, 3 * width), F32)
    for j in range(GD_CONV):
        conv = conv + cw_ref[j:j + 1, :] * ext_ref[8 + j - half:8 + j - half + TM, 0:3 * width]
    qkv = conv * jax.nn.sigmoid(conv)
    ones_bd = ones_ref[...]

    def l2n(u):
        return u * lax.rsqrt(_dot(u * u, ones_bd, precision=HIGHEST) + NORM_EPS)

    q = l2n(qkv[:, 0:width]) * HEAD_DIM ** -0.5
    k = l2n(qkv[:, width:2 * width])
    v = qkv[:, 2 * width:3 * width]
    ab = ext_ref[8:8 + TM, 4 * width:4 * width + LANES]
    g_all = -jnp.exp(alog_ref[...]) * jax.nn.softplus(ab + dtb_ref[...])
    beta_all = jax.nn.sigmoid(ab)
    heads = _head_lanes(width)
    n_heads = len(heads)
    tok_ref[0] = q
    tok_ref[1] = k
    tok_ref[2] = v
    for d in range(2):
        g = jnp.zeros((TM, width), F32)
        beta = jnp.zeros((TM, width), F32)
        for h, in_head in enumerate(heads):
            ja = d * n_heads + h
            jb = 2 * n_heads + ja
            g = jnp.where(in_head, g_all[:, ja:ja + 1], g)
            beta = jnp.where(in_head, beta_all[:, jb:jb + 1], beta)
        tok_ref[3 + 2 * d] = g
        tok_ref[4 + 2 * d] = k * beta
    out_refs = (w2_ref, y1_ref, m_ref, n_ref)

    def chunk_body(c, carry):
        c0 = pl.multiple_of(c * CHUNK, CHUNK)
        rows = pl.ds(c0, CHUNK)
        q_c, k_c, v_c = tok_ref[0, rows, :], tok_ref[1, rows, :], tok_ref[2, rows, :]
        k_stacked, q_stacked = _stack_heads(k_c, heads), _stack_heads(q_c, heads)
        args = [[] for _ in range(9)]
        p_ends = []
        for d in range(2):
            g, kb = tok_ref[3 + 2 * d, rows, :], tok_ref[4 + 2 * d, rows, :]
            incl, _, _ = _tri_masks(reverse=(d == 1))
            ci = _dot(incl.astype(F32), g, precision=HIGHEST)
            ce = ci - g
            ci_end = ci[0:1, :] if d == 1 else ci[CHUNK - 1:CHUNK, :]
            p_end = jnp.exp(ci_end)
            ks, rs, kbs, vs, kds, ads = (_stack_heads(u, heads) for u in (
                k_c * jnp.exp(ce), q_c * jnp.exp(ci), kb, v_c,
                kb * jnp.exp(ci_end - ci), kb * jnp.exp(ci_end - ce)))
            earlier, earlier_or_same, eye = _stacked_masks(ks.shape[0], reverse=(d == 1))
            ci_col = jnp.concatenate([ci[:, h * HEAD_DIM:h * HEAD_DIM + 1] for h in range(n_heads)], axis=0)
            ce_col = jnp.concatenate([ce[:, h * HEAD_DIM:h * HEAD_DIM + 1] for h in range(n_heads)], axis=0)
            ci_row = jnp.sum(jnp.where(eye, ci_col, 0.0), axis=0, keepdims=True)
            ce_row = jnp.sum(jnp.where(eye, ce_col, 0.0), axis=0, keepdims=True)
            kkb = _dot_nt(k_stacked, kbs)
            qkb = _dot_nt(q_stacked, kbs)
            a_ka = kkb * jnp.exp(jnp.where(earlier, ce_col - ce_row, MASKED))
            a_kk = kkb * jnp.exp(jnp.where(earlier, ce_col - ci_row, MASKED))
            a_ra = qkb * jnp.exp(jnp.where(earlier_or_same, ci_col - ce_row, MASKED))
            a_rk = qkb * jnp.exp(jnp.where(earlier_or_same, ci_col - ci_row, MASKED))
            for slot, value in zip(args, (a_ka, a_kk, a_ra, a_rk, ks, rs, vs, kds, ads)):
                slot.append(value)
            p_ends.append(p_end)
        for d, res in enumerate(_chunk_matrices(*args, eye)):
            _store_chunk(out_refs, d, c0, n_heads, *res, p_ends[d])
        return carry

    lax.fori_loop(0, TM // CHUNK, chunk_body, 0)


def _gdn_prepare(p_gd, pr, n_ctx):
    b, l, pw = p_gd.shape
    width = pr['gd_width']
    n_heads = width // HEAD_DIM
    nt = l // TM
    kern = functools.partial(_gdn_prep_kernel, n_ctx_tiles=n_ctx // TM, n_tiles=nt, width=width)
    params = [pr['gd_conv_w'], pr['gd_alog_p'], pr['gd_dtb_p'], pr['ones_bd']]
    scan_shapes, scan_specs = _scan_out_specs(b, n_heads, l)
    return pl.pallas_call(
        kern,
        grid=(b, nt),
        in_specs=_halo_specs(pw, nt) + [_full_spec(a) for a in params],
        out_specs=scan_specs,
        out_shape=scan_shapes,
        scratch_shapes=[pltpu.VMEM((TM + 16, pw), F32), pltpu.VMEM((7, TM, width), F32)],
        compiler_params=_cparams(("arbitrary", "arbitrary")),
        name="gdn_prepare",
    )(p_gd, p_gd, p_gd, *params)


def _outproj_kernel(x_ref, mod_ref, na_ref, ryf_ref, ryb_ref, bv_ref, gate_ref, gyf_ref, gyb_ref, z_ref,
                    lnw_ref, lnb_ref, gng_ref, wo_ref, g2_ref, rw_ref, rb_ref,
                    x1_ref, h2_ref, route_ref, meta_ref, *, na_w, rw_w):
    n_heads = ryf_ref.shape[1]
    rw_parts, gd_parts = [], []
    for h in range(n_heads):
        y = ryf_ref[0, h] + ryb_ref[0, h]
        yc = y - jnp.mean(y, axis=-1, keepdims=True)
        rw_parts.append(yc * lax.rsqrt(jnp.mean(yc * yc, axis=-1, keepdims=True) + RW_GN_EPS))
        gd_parts.append(_rms(gyf_ref[0, h] + gyb_ref[0, h]) * gng_ref[...])
    yn = jnp.concatenate(rw_parts, axis=-1)
    o_rw = (yn * lnw_ref[...] + lnb_ref[...] + bv_ref[0]) * gate_ref[0]
    z = z_ref[0]
    o_gd = jnp.concatenate(gd_parts, axis=-1) * (z * jax.nn.sigmoid(z))
    o = (_dot(na_ref[0], wo_ref[0:na_w, :])
         + _dot(o_rw.astype(BF16), wo_ref[na_w:na_w + rw_w, :])
         + _dot(o_gd.astype(BF16), wo_ref[na_w + rw_w:, :]))
    mod = mod_ref[0]
    x1 = x_ref[0] + mod[2:3] * o
    x1_ref[0] = x1
    h2 = _rms(x1) * g2_ref[...] * (1.0 + mod[4:5]) + mod[3:4]
    h2_ref[0] = h2
    logits = _dot(h2, rw_ref[...], precision=HIGHEST) + rb_ref[...]
    lane = lax.broadcasted_iota(jnp.int32, logits.shape, 1)
    chosen, weights = [], []
    onehot = jnp.zeros(logits.shape, F32)
    for kk in range(TOP_K):
        m = jnp.max(logits, axis=-1, keepdims=True)
        idx = jnp.min(jnp.where(logits == m, lane, LANES), axis=-1, keepdims=True)
        chosen.append(idx)
        top_max = m if kk == 0 else top_max
        weights.append(jnp.exp(m - top_max))
        onehot = jnp.where(lane == idx, 1.0, onehot)
        logits = jnp.where(lane == idx, -jnp.inf, logits)
    denom = weights[0] + weights[1] + weights[2] + weights[3]
    tok_i = lax.broadcasted_iota(jnp.int32, (TM, TM), 0)
    tok_j = lax.broadcasted_iota(jnp.int32, (TM, TM), 1)
    before = jnp.where(tok_j < tok_i, 1.0, 0.0).astype(BF16)
    rank = _dot(before, onehot.astype(BF16))
    counts = jnp.sum(onehot, axis=0, keepdims=True)
    count8 = jnp.floor((counts + 7.0) * 0.125) * 8.0
    exp_i = lax.broadcasted_iota(jnp.int32, (LANES, LANES), 0)
    exp_j = lax.broadcasted_iota(jnp.int32, (LANES, LANES), 1)
    seg_start = _dot(jnp.broadcast_to(count8, (8, LANES)), jnp.where(exp_i < exp_j, 1.0, 0.0),
                     precision=HIGHEST)[0:1]
    slot_of = seg_start + rank
    route = jnp.zeros(logits.shape, F32)
    for kk in range(TOP_K):
        slot = jnp.sum(jnp.where(lane == chosen[kk], slot_of, 0.0), axis=-1, keepdims=True)
        route = jnp.where(lane == kk, weights[kk] / denom, route)
        route = jnp.where(lane == TOP_K + kk, slot, route)
    route_ref[0] = route
    row8 = lax.broadcasted_iota(jnp.int32, (8, LANES), 0)
    meta_ref[0, 0] = jnp.where(row8 == 0, count8, jnp.where(row8 == 1, seg_start, 0.0))


def _outproj(xa, mods, o_na, ry, bv, gate, gy, p_gd, pr, n_ctx):
    b, l, d = xa.shape
    nt = l // TM
    na_w = o_na.shape[-1]
    n_heads = ry[0].shape[1]
    rw_w = n_heads * HEAD_DIM
    kern = functools.partial(_outproj_kernel, na_w=na_w, rw_w=rw_w)
    tok = lambda w: pl.BlockSpec((1, TM, w), lambda i, t: (i, t, 0))
    head_major = pl.BlockSpec((1, n_heads, TM, HEAD_DIM), lambda i, t: (i, 0, t, 0))
    params = [pr['rw_ln_w'].reshape(1, rw_w), pr['rw_ln_b'].reshape(1, rw_w), pr['gd_norm_g'].reshape(1, HEAD_DIM),
              pr['w_out_bf'], pr['norm_ffn_g'], pr['router_w_p'], pr['router_b_p']]
    return pl.pallas_call(
        kern,
        grid=(b, nt),
        in_specs=[tok(d), pl.BlockSpec((1, 6, d), _mod_index(n_ctx // TM, b)), tok(na_w),
                  head_major, head_major, tok(rw_w), tok(rw_w), head_major, head_major,
                  pl.BlockSpec((1, TM, rw_w), lambda i, t: (i, t, 3))]
                 + [_full_spec(a) for a in params],
        out_specs=[tok(d), tok(d), tok(LANES), pl.BlockSpec((1, 1, 8, LANES), lambda i, t: (i, t, 0, 0))],
        out_shape=[jax.ShapeDtypeStruct((b, l, d), F32), jax.ShapeDtypeStruct((b, l, d), F32),
                   jax.ShapeDtypeStruct((b, l, LANES), F32), jax.ShapeDtypeStruct((b, nt, 8, LANES), F32)],
        compiler_params=_cparams(("arbitrary", "arbitrary")),
        name="outproj_router",
    )(xa, mods, o_na, ry[0], ry[1], bv, gate, gy[0], gy[1], p_gd, *params)


MOE_BM = 256
SEG_ALIGN = 8
SEG_BITS = (32, 16, 8, 4, 2, 1)


def _n_slots(n_exp):
    return _round_up(TOP_K * TM + n_exp * (SEG_ALIGN - 1), LANES)


def _segment_dma(segs_ref, buf_ref, hbm_ref, sem, n_exp, to_hbm, wait):
    def body(e, carry):
        start = segs_ref[0, 0, e]
        units = segs_ref[0, 0, n_exp + e]
        offset = segs_ref[0, 0, 2 * n_exp + e]
        for bit in SEG_BITS:
            done = (units & ~(2 * bit - 1)) * SEG_ALIGN
            rows = bit * SEG_ALIGN

            @pl.when((units & bit) != 0)
            def _():
                in_buf = buf_ref.at[pl.ds(pl.multiple_of(start + done, SEG_ALIGN), rows), :]
                in_hbm = hbm_ref.at[pl.ds(pl.multiple_of(offset + done, SEG_ALIGN), rows), :]
                copy = (pltpu.make_async_copy(in_buf, in_hbm, sem) if to_hbm
                        else pltpu.make_async_copy(in_hbm, in_buf, sem))
                if wait:
                    copy.wait()
                else:
                    copy.start()
        return carry
    lax.fori_loop(0, n_exp, body, 0)


def _slot_lanes(route, n_slots):
    lane = lax.broadcasted_iota(jnp.int32, (route.shape[0], n_slots), 1)
    slots = [route[:, TOP_K + k:TOP_K + k + 1].astype(jnp.int32) for k in range(TOP_K)]
    return lane, slots


def _dispatch_kernel(segs_ref, h_ref, route_ref, xb_in_ref, xb_ref, xs_ref, sem, *, n_exp):
    del xb_in_ref
    lane, slots = _slot_lanes(route_ref[...], xs_ref.shape[0])
    picked = lane == slots[0]
    for k in range(1, TOP_K):
        picked = picked | (lane == slots[k])
    onehot = jnp.where(picked, 1.0, 0.0).astype(BF16)
    xs_ref[...] = _dot_tn(onehot, h_ref[...].astype(BF16))
    _segment_dma(segs_ref, xs_ref, xb_ref, sem, n_exp, to_hbm=True, wait=False)
    _segment_dma(segs_ref, xs_ref, xb_ref, sem, n_exp, to_hbm=True, wait=True)


def _dispatch(h2, route, segs, n_pad, n_exp):
    t, d = h2.shape
    nt = t // TM
    n_slots = _n_slots(n_exp)
    kern = functools.partial(_dispatch_kernel, n_exp=n_exp)
    return pl.pallas_call(
        kern,
        grid=(nt,),
        in_specs=[pl.BlockSpec((1, 1, LANES), lambda i: (i, 0, 0), memory_space=pltpu.SMEM),
                  pl.BlockSpec((TM, d), lambda i: (i, 0)),
                  pl.BlockSpec((TM, LANES), lambda i: (i, 0)),
                  pl.BlockSpec(memory_space=pl.ANY)],
        out_specs=pl.BlockSpec(memory_space=pl.ANY),
        out_shape=jax.ShapeDtypeStruct((n_pad, d), F32),
        input_output_aliases={3: 0},
        scratch_shapes=[pltpu.VMEM((n_slots, d), F32), pltpu.SemaphoreType.DMA(())],
        compiler_params=_cparams(("arbitrary",)),
        name="moe_dispatch",
    )(segs, h2, route, jnp.zeros((n_pad, d), F32))


def _expert_kernel(blk_e_ref, x_ref, wgu_ref, bgu_ref, wd_ref, bd_ref, y_ref):
    del blk_e_ref
    x = x_ref[...].astype(BF16)
    d_exp = wd_ref.shape[1]
    gu = _dot(x, wgu_ref[0]) + bgu_ref[0]
    gate = jnp.minimum(gu[:, :d_exp], SWIGLU_LIMIT)
    lin = jnp.clip(gu[:, d_exp:], -SWIGLU_LIMIT, SWIGLU_LIMIT)
    act = gate * jax.nn.sigmoid(SWIGLU_ALPHA * gate) * (lin + 1.0)
    y_ref[...] = _dot(act.astype(BF16), wd_ref[0]) + bd_ref[0]


def _expert_blocks(xb, blk_e, w_gu, b_gu, w_down, b_down):
    n_pad, d = xb.shape
    n_blocks = n_pad // MOE_BM
    n_exp, _, d_gu = w_gu.shape
    grid_spec = pltpu.PrefetchScalarGridSpec(
        num_scalar_prefetch=1,
        grid=(n_blocks,),
        in_specs=[pl.BlockSpec((MOE_BM, d), lambda i, be: (i, 0)),
                  pl.BlockSpec((1, d, d_gu), lambda i, be: (be[i], 0, 0)),
                  pl.BlockSpec((1, 1, d_gu), lambda i, be: (be[i], 0, 0)),
                  pl.BlockSpec((1, d_gu // 2, d), lambda i, be: (be[i], 0, 0)),
                  pl.BlockSpec((1, 1, d), lambda i, be: (be[i], 0, 0))],
        out_specs=pl.BlockSpec((MOE_BM, d), lambda i, be: (i, 0)))
    return pl.pallas_call(
        _expert_kernel,
        grid_spec=grid_spec,
        out_shape=jax.ShapeDtypeStruct((n_pad, d), F32),
        compiler_params=_cparams(("arbitrary",)),
        name="moe_experts",
    )(blk_e, xb, w_gu, b_gu.reshape(n_exp, 1, d_gu), w_down, b_down.reshape(n_exp, 1, d))


def _combine_kernel(segs_ref, y_hbm, x_ref, mod_ref, route_ref, o_ref, ybuf, sem, *, n_exp):
    @pl.when(pl.program_id(0) == 0)
    def _():
        ybuf[...] = jnp.zeros_like(ybuf)

    _segment_dma(segs_ref, ybuf, y_hbm, sem, n_exp, to_hbm=False, wait=False)
    _segment_dma(segs_ref, ybuf, y_hbm, sem, n_exp, to_hbm=False, wait=True)
    y = ybuf[...].astype(BF16)
    route = route_ref[...]
    lane, slots = _slot_lanes(route, ybuf.shape[0])
    weights = jnp.zeros(lane.shape, F32)
    for k in range(TOP_K):
        weights = jnp.where(lane == slots[k], route[:, k:k + 1], weights)
    o_ref[...] = x_ref[...] + mod_ref[0, 5:6, :] * _dot(weights.astype(BF16), y)


def _combine(yb, segs, x1, mods, route, n_batch, l, n_ctx, n_exp):
    t, d = x1.shape
    nt = t // TM
    per_sample = l // TM
    n_ctx_tiles = n_ctx // TM
    kern = functools.partial(_combine_kernel, n_exp=n_exp)

    def mod_index(i):
        return (jnp.where(i % per_sample < n_ctx_tiles, n_batch, i // per_sample), 0, 0)

    return pl.pallas_call(
        kern,
        grid=(nt,),
        in_specs=[pl.BlockSpec((1, 1, LANES), lambda i: (i, 0, 0), memory_space=pltpu.SMEM),
                  pl.BlockSpec(memory_space=pl.ANY),
                  pl.BlockSpec((TM, d), lambda i: (i, 0)),
                  pl.BlockSpec((1, 6, d), mod_index),
                  pl.BlockSpec((TM, LANES), lambda i: (i, 0))],
        out_specs=pl.BlockSpec((TM, d), lambda i: (i, 0)),
        out_shape=jax.ShapeDtypeStruct((t, d), F32),
        scratch_shapes=[pltpu.VMEM((_n_slots(n_exp), d), F32), pltpu.SemaphoreType.DMA(())],
        compiler_params=_cparams(("arbitrary",)),
        name="moe_combine",
    )(segs, yb, x1, mods, route)


def _route(meta, n_exp, n_tokens):
    nt = meta.shape[0]
    count8 = meta[:, 0, :n_exp].astype(jnp.int32)
    seg_start = meta[:, 1, :n_exp].astype(jnp.int32)
    per_expert = jnp.sum(count8, axis=0)
    padded = (per_expert + MOE_BM - 1) // MOE_BM * MOE_BM
    pad_end = jnp.cumsum(padded)
    offset = (pad_end - padded)[None, :] + jnp.cumsum(count8, axis=0) - count8
    n_pad = _round_up(n_tokens * TOP_K + nt * n_exp * (SEG_ALIGN - 1), MOE_BM) + n_exp * MOE_BM
    block_row = jnp.arange(n_pad // MOE_BM, dtype=jnp.int32) * MOE_BM
    blk_e = jnp.minimum(jnp.sum(block_row[:, None] >= pad_end[None, :], axis=1), n_exp - 1).astype(jnp.int32)
    segs = jnp.concatenate([seg_start, count8 // SEG_ALIGN, offset,
                            jnp.zeros((nt, LANES - 3 * n_exp), jnp.int32)], axis=1)
    return segs.reshape(nt, 1, LANES), blk_e, n_pad


def _channel_sublayer(x1, h2, route, meta, mods, pr, n_ctx):
    b, l, d = x1.shape
    t = b * l
    n_exp = pr['moe_w_gu_bf'].shape[0]
    assert 3 * n_exp <= LANES
    route = route.reshape(t, LANES)
    segs, blk_e, n_pad = _route(meta.reshape(t // TM, 8, LANES), n_exp, t)
    xb = _dispatch(h2.reshape(t, d), route, segs, n_pad, n_exp)
    yb = _expert_blocks(xb, blk_e, pr['moe_w_gu_bf'], pr['moe_b_gu'], pr['moe_w_down_bf'], pr['moe_b_down'])
    x2 = _combine(yb, segs, x1.reshape(t, d), mods, route, b, l, n_ctx, n_exp)
    return x2.reshape(b, l, d)


def _pad_cols(w, n):
    return jnp.pad(w, ((0, 0), (0, n - w.shape[1])))


def _round_up(n, m):
    return -(-n // m) * m


def _prep_layer_params(pr):
    d = pr['w_in'].shape[0]
    na_w = d // 2
    rw_w = d // 4
    gd_w = d - na_w - rw_w
    na_cols = 3 * na_w
    rw_cols = 3 * rw_w + RW_DECAY_LORA + RW_ICLR_LORA + RW_GATE_LORA
    rw_pad = _round_up(rw_cols, LANES)
    gd_cols = pr['w_in'].shape[1] - na_cols - rw_cols
    gd_pad = _round_up(gd_cols, LANES)
    w_in = pr['w_in']
    out = dict(pr)
    out['w_in'] = jnp.concatenate([w_in[:, :na_cols],
                                   _pad_cols(w_in[:, na_cols:na_cols + rw_cols], rw_pad),
                                   _pad_cols(w_in[:, na_cols + rw_cols:], gd_pad)], axis=1).astype(BF16)
    out['rw_w'] = rw_pad
    out['gd_w'] = gd_pad
    n_heads = na_w // HEAD_DIM
    out['qkg'] = jnp.stack([jnp.tile(pr['na_q_gain'], n_heads) * HEAD_DIM ** -0.5,
                            jnp.tile(pr['na_k_gain'], n_heads)])
    head_of = np.arange(na_w) // HEAD_DIM
    out['avg'] = jnp.asarray((head_of[:, None] == head_of[None, :]) / HEAD_DIM, BF16)
    out['norm_mix_g'] = pr['norm_mix_g'].reshape(1, d)
    out['norm_ffn_g'] = pr['norm_ffn_g'].reshape(1, d)
    out['rw_mu_p'] = jnp.pad(pr['rw_mu'], (0, rw_pad - rw_cols)).reshape(1, rw_pad)
    o1, o2 = RW_DECAY_LORA, RW_DECAY_LORA + RW_ICLR_LORA
    out['rw_wup_p'] = jnp.pad(pr['rw_w_up'], ((0, 0), (0, LANES - o1), (0, 0)))
    out['rw_aup_p'] = jnp.pad(pr['rw_a_up'], ((0, 0), (o1, LANES - o2), (0, 0)))
    out['rw_gup_p'] = jnp.pad(pr['rw_g_up'], ((o2, LANES - o2 - RW_GATE_LORA), (0, 0)))
    head_of = np.arange(rw_w) // HEAD_DIM
    out['ones_bd'] = jnp.asarray(head_of[:, None] == head_of[None, :], F32)
    out['gd_width'] = gd_w
    n_ab = pr['gd_A_log'].size
    out['gd_alog_p'] = jnp.pad(pr['gd_A_log'].reshape(-1), (0, LANES - n_ab)).reshape(1, LANES)
    out['gd_dtb_p'] = jnp.pad(pr['gd_dt_bias'].reshape(-1), (0, LANES - n_ab)).reshape(1, LANES)
    out['w_out_bf'] = pr['w_out'].astype(BF16)
    n_exp = pr['moe_router_w'].shape[1]
    out['router_w_p'] = _pad_cols(pr['moe_router_w'], LANES)
    out['router_b_p'] = jnp.concatenate([pr['moe_router_b'],
                                         jnp.full((LANES - n_exp,), MASKED, F32)]).reshape(1, LANES)
    out['moe_w_gu_bf'] = pr['moe_w_gu'].astype(BF16)
    out['moe_w_down_bf'] = pr['moe_w_down'].astype(BF16)
    return out


def _mixing_sublayer(xa, mods, pr, n_ctx):
    b, l, d = xa.shape
    na_w = d // 2
    p_na, p_rw, p_gd = _inproj(xa, mods, pr['norm_mix_g'], pr['w_in'], pr['qkg'], pr['avg'], n_ctx,
                               na_w, pr['rw_w'], pr['gd_w'])
    o_na = _na_attention(p_na, pr['na_bias'], n_ctx, na_w)
    rw2, ry1, rm, rn, bv, gate = _rwkv_prepare(p_rw, pr, n_ctx)
    ry = _scan(rw2, ry1, rm, rn, n_ctx)
    gy = _scan(*_gdn_prepare(p_gd, pr, n_ctx), n_ctx)
    return _outproj(xa, mods, o_na, ry, bv, gate, gy, p_gd, pr, n_ctx)


_LAYER_PARAMS = ('norm_mix_g', 'norm_ffn_g', 'w_in', 'w_out', 'na_q_gain', 'na_k_gain', 'na_rpb',
                 'rw_mu', 'rw_w0', 'rw_w_up', 'rw_a0', 'rw_a_up', 'rw_g_up', 'rw_k_k', 'rw_k_a', 'rw_r_k',
                 'rw_ln_w', 'rw_ln_b', 'gd_conv_w', 'gd_A_log', 'gd_dt_bias', 'gd_norm_g',
                 'moe_router_w', 'moe_router_b', 'moe_w_gu', 'moe_b_gu', 'moe_w_down', 'moe_b_down')


def kernel(x, c, ctx, c_ctx, ada_w, ada_b, norm_mix_g, norm_ffn_g, w_in, w_out, na_q_gain, na_k_gain, na_rpb, rw_mu, rw_w0, rw_w_up, rw_a0, rw_a_up, rw_g_up, rw_k_k, rw_k_a, rw_r_k, rw_ln_w, rw_ln_b, gd_conv_w, gd_A_log, gd_dt_bias, gd_norm_g, moe_router_w, moe_router_b, moe_w_gu, moe_b_gu, moe_w_down, moe_b_down):
    stacked = dict(zip(_LAYER_PARAMS, (norm_mix_g, norm_ffn_g, w_in, w_out, na_q_gain, na_k_gain, na_rpb,
                                       rw_mu, rw_w0, rw_w_up, rw_a0, rw_a_up, rw_g_up, rw_k_k, rw_k_a, rw_r_k,
                                       rw_ln_w, rw_ln_b, gd_conv_w, gd_A_log, gd_dt_bias, gd_norm_g,
                                       moe_router_w, moe_router_b, moe_w_gu, moe_b_gu, moe_w_down, moe_b_down)))
    b, s, d = x.shape
    n_ctx = ctx.shape[1]
    depth = ada_w.shape[0]
    assert b + 1 <= 8 and n_ctx % TM == 0 and s % TM == 0 and s % GRID_W == 0
    cvec = jnp.zeros((8, d), F32).at[:b].set(c).at[b].set(c_ctx)
    mods_all = _adaln(cvec, ada_w, ada_b).reshape(depth, 8, 6, d)[:, :b + 1]
    xa = jnp.concatenate([ctx, x], axis=1)
    for layer in range(depth):
        pr = _prep_layer_params({name: value[layer] for name, value in stacked.items()})
        pr['na_bias'] = _na_bias_tables(pr['na_rpb'], s // GRID_W)
        mods = mods_all[layer]
        x1, h2, route, meta = _mixing_sublayer(xa, mods, pr, n_ctx)
        xa = _channel_sublayer(x1, h2, route, meta, mods, pr, n_ctx)
    return xa[:, n_ctx:]
```

```python
import functools
import math

import numpy as np
import jax
import jax.numpy as jnp
from jax import lax
from jax.experimental import pallas as pl
from jax.experimental.pallas import tpu as pltpu

F32 = jnp.float32
BF16 = jnp.bfloat16
HIGHEST = lax.Precision.HIGHEST

GRID_W = 64
NORM_EPS = 1e-6
HEAD_DIM = 64
NA_WIN_ROWS = 8
NA_WIN_COLS = 16
RW_DECAY_LORA = 32
RW_ICLR_LORA = 32
RW_GATE_LORA = 64
RW_GN_EPS = 64e-5
GD_CONV = 5
TOP_K = 4
SWIGLU_LIMIT = 7.0
SWIGLU_ALPHA = 1.702

LANES = 128
TM = 256
NA_QROWS = 4
NA_KROWS = NA_QROWS + NA_WIN_ROWS
CHUNK = 64
CHUNKS_PER_STEP = 2
MASKED = -1e30
VMEM_LIMIT = 48 * 1024 * 1024


def _cparams(sem):
    return pltpu.CompilerParams(dimension_semantics=sem, vmem_limit_bytes=VMEM_LIMIT)


def _dot(a, b, **kw):
    return jnp.dot(a, b, preferred_element_type=F32, **kw)


def _dot_nt(a, b, **kw):
    return lax.dot_general(a, b, (((1,), (1,)), ((), ())), preferred_element_type=F32, **kw)


def _dot_tn(a, b, **kw):
    return lax.dot_general(a, b, (((0,), (0,)), ((), ())), preferred_element_type=F32, **kw)


def _adaln_kernel(c_ref, w_ref, b_ref, o_ref):
    c = c_ref[...]
    s = c * jax.nn.sigmoid(c)
    o_ref[0] = _dot(s.astype(BF16), w_ref[0].astype(BF16)) + b_ref[0]


def _adaln(cvec, ada_w, ada_b):
    depth, d, n = ada_w.shape
    tn = n // 4
    return pl.pallas_call(
        _adaln_kernel,
        grid=(depth, n // tn),
        in_specs=[pl.BlockSpec((8, d), lambda l, j: (0, 0)),
                  pl.BlockSpec((1, d, tn), lambda l, j: (l, 0, j)),
                  pl.BlockSpec((1, 1, tn), lambda l, j: (l, 0, j))],
        out_specs=pl.BlockSpec((1, 8, tn), lambda l, j: (l, 0, j)),
        out_shape=jax.ShapeDtypeStruct((depth, 8, n), F32),
        compiler_params=_cparams(("arbitrary", "arbitrary")),
        name="adaln",
    )(cvec, ada_w, ada_b.reshape(depth, 1, n))


def _mod_index(n_ctx_tiles, n_batch):
    return lambda b, t: (jnp.where(t < n_ctx_tiles, n_batch, b), 0, 0)


def _rms(x):
    return x * lax.rsqrt(jnp.mean(x * x, axis=-1, keepdims=True) + NORM_EPS)


def _inproj_kernel(x_ref, mod_ref, g_ref, w_ref, qkg_ref, avg_ref, na_ref, rw_ref, gd_ref, *, na_w, rw_w):
    mod = mod_ref[0]
    h = _rms(x_ref[0]) * g_ref[...]
    hb = (h * (1.0 + mod[1:2]) + mod[0:1]).astype(BF16)
    pa = _dot(hb, w_ref[:, 0:3 * na_w])
    avg = avg_ref[...]
    qkg = qkg_ref[...]
    q = pa[:, 0:na_w]
    k = pa[:, na_w:2 * na_w]
    qn = q * lax.rsqrt(_dot(q * q, avg) + NORM_EPS) * qkg[0:1]
    kn = k * lax.rsqrt(_dot(k * k, avg) + NORM_EPS) * qkg[1:2]
    na_ref[0, :, 0:na_w] = qn.astype(BF16)
    na_ref[0, :, na_w:2 * na_w] = kn.astype(BF16)
    na_ref[0, :, 2 * na_w:3 * na_w] = pa[:, 2 * na_w:3 * na_w].astype(BF16)
    rw_ref[0] = _dot(hb, w_ref[:, 3 * na_w:3 * na_w + rw_w])
    gd_ref[0] = _dot(hb, w_ref[:, 3 * na_w + rw_w:])


def _inproj(xa, mods, g, w_all, qkg, avg, n_ctx, na_w, rw_w, gd_w):
    b, l, d = xa.shape
    nt = l // TM
    kern = functools.partial(_inproj_kernel, na_w=na_w, rw_w=rw_w)
    return pl.pallas_call(
        kern,
        grid=(b, nt),
        in_specs=[pl.BlockSpec((1, TM, d), lambda i, t: (i, t, 0)),
                  pl.BlockSpec((1, 6, d), _mod_index(n_ctx // TM, b)),
                  pl.BlockSpec((1, d), lambda i, t: (0, 0)),
                  pl.BlockSpec(w_all.shape, lambda i, t: (0, 0)),
                  pl.BlockSpec(qkg.shape, lambda i, t: (0, 0)),
                  pl.BlockSpec(avg.shape, lambda i, t: (0, 0))],
        out_specs=[pl.BlockSpec((1, TM, 3 * na_w), lambda i, t: (i, t, 0)),
                   pl.BlockSpec((1, TM, rw_w), lambda i, t: (i, t, 0)),
                   pl.BlockSpec((1, TM, gd_w), lambda i, t: (i, t, 0))],
        out_shape=[jax.ShapeDtypeStruct((b, l, 3 * na_w), BF16),
                   jax.ShapeDtypeStruct((b, l, rw_w), F32),
                   jax.ShapeDtypeStruct((b, l, gd_w), F32)],
        compiler_params=_cparams(("arbitrary", "arbitrary")),
        name="inproj",
    )(xa, mods, g, w_all, qkg, avg)


def _na_bias_tables(rpb, rows):
    n_heads = rpb.shape[0]
    nq, nk = NA_QROWS * GRID_W, NA_KROWS * GRID_W
    qc = np.arange(GRID_W)[:, None]
    kc = np.arange(GRID_W)[None, :]
    c0 = np.clip(qc - NA_WIN_COLS // 2, 0, GRID_W - NA_WIN_COLS)
    col_ok = (kc >= c0) & (kc < c0 + NA_WIN_COLS)
    dc = kc - qc + NA_WIN_COLS - 1
    pick_col = (dc[None] == np.arange(2 * NA_WIN_COLS - 1)[:, None, None]).astype(np.float32)
    by_col = jnp.einsum('hrd,dqk->hrqk', rpb, pick_col, precision=HIGHEST)
    tables = []
    for r0 in (0, NA_QROWS, rows - NA_QROWS):
        ks = int(np.clip(r0 - NA_WIN_ROWS // 2, 0, rows - NA_KROWS))
        qr = (r0 + np.arange(NA_QROWS))[:, None]
        kr = (ks + np.arange(NA_KROWS))[None, :]
        s0 = np.clip(qr - NA_WIN_ROWS // 2, 0, rows - NA_WIN_ROWS)
        row_ok = (kr >= s0) & (kr < s0 + NA_WIN_ROWS)
        dr = kr - qr + NA_WIN_ROWS - 1
        pick_row = ((dr[None] == np.arange(2 * NA_WIN_ROWS - 1)[:, None, None]) & row_ok[None]).astype(np.float32)
        bias = jnp.einsum('rab,hrqk->haqbk', pick_row, by_col, precision=HIGHEST)
        ok = row_ok[:, None, :, None] & col_ok[None, :, None, :]
        tables.append(jnp.where(ok[None], bias, MASKED).reshape(n_heads, nq, nk))
    tables.append(jnp.full((n_heads, nq, nk), MASKED, F32))
    return jnp.stack(tables).reshape(4, n_heads // 2, 2, nq, nk)


def _na_kernel(q_ref, k_ref, v_ref, bias_ref, o_ref, *, n_ctx, rows):
    j = pl.program_id(2)
    nk = NA_KROWS * GRID_W
    ks_row = jnp.clip((j - 1) * NA_QROWS - NA_WIN_ROWS // 2, 0, rows - NA_KROWS)
    kstart = pl.multiple_of(n_ctx + ks_row * GRID_W, GRID_W)
    q = q_ref[0]
    kl = k_ref[0, pl.ds(kstart, nk), :]
    vl = v_ref[0, pl.ds(kstart, nk), :]
    kc = k_ref[0, 0:n_ctx, :]
    vc = v_ref[0, 0:n_ctx, :]
    lane = lax.broadcasted_iota(jnp.int32, q.shape, 1)
    outs = []
    for h in range(2):
        in_head = (lane >= h * HEAD_DIM) & (lane < (h + 1) * HEAD_DIM)
        qh = jnp.where(in_head, q, jnp.zeros_like(q))
        s_loc = _dot_nt(qh, kl) + bias_ref[0, 0, h]
        s_ctx = _dot_nt(qh, kc)
        m = jnp.maximum(jnp.max(s_loc, axis=-1, keepdims=True), jnp.max(s_ctx, axis=-1, keepdims=True))
        p_loc = jnp.exp(s_loc - m)
        p_ctx = jnp.exp(s_ctx - m)
        den = jnp.sum(p_loc, axis=-1, keepdims=True) + jnp.sum(p_ctx, axis=-1, keepdims=True)
        o = _dot(p_loc.astype(BF16), vl) + _dot(p_ctx.astype(BF16), vc)
        outs.append(o / den)
    o_ref[0] = jnp.where(lane < HEAD_DIM, outs[0], outs[1]).astype(o_ref.dtype)


def _na_attention(p_na, bias, n_ctx, na_w):
    b, l, _ = p_na.shape
    rows = (l - n_ctx) // GRID_W
    nq = NA_QROWS * GRID_W
    assert n_ctx == nq and rows >= NA_KROWS and rows % NA_QROWS == 0
    n_pairs = na_w // LANES
    nblk = l // nq
    kern = functools.partial(_na_kernel, n_ctx=n_ctx, rows=rows)

    def bias_index(i, hp, j):
        pat = jnp.where(j == 0, 3, jnp.where(j == 1, 0, jnp.where(j == nblk - 1, 2, 1)))
        return (pat, hp, 0, 0, 0)

    return pl.pallas_call(
        kern,
        grid=(b, n_pairs, nblk),
        in_specs=[pl.BlockSpec((1, nq, LANES), lambda i, hp, j: (i, j, hp)),
                  pl.BlockSpec((1, l, LANES), lambda i, hp, j: (i, 0, n_pairs + hp)),
                  pl.BlockSpec((1, l, LANES), lambda i, hp, j: (i, 0, 2 * n_pairs + hp)),
                  pl.BlockSpec((1, 1, 2) + bias.shape[3:], bias_index)],
        out_specs=pl.BlockSpec((1, nq, LANES), lambda i, hp, j: (i, j, hp)),
        out_shape=jax.ShapeDtypeStruct((b, l, na_w), BF16),
        compiler_params=_cparams(("arbitrary", "arbitrary", "arbitrary")),
        name="na_attention",
    )(p_na, p_na, p_na, bias)


def _tri_masks(reverse):
    t = lax.broadcasted_iota(jnp.int32, (CHUNK, CHUNK), 0)
    s = lax.broadcasted_iota(jnp.int32, (CHUNK, CHUNK), 1)
    earlier_or_same = (s >= t) if reverse else (s <= t)
    earlier = (s > t) if reverse else (s < t)
    return earlier_or_same, earlier, t == s


def _each(fn, *lists):
    return [fn(*args) for args in zip(*lists)]


def _bf16(values):
    return [u.astype(BF16) for u in values]


def _unit_tri_inverse(a, eye):
    assert CHUNK == 64
    x1 = [-u for u in a]
    x1b = _bf16(x1)
    x2 = _each(_dot, x1b, x1b)
    x2b = _bf16(x2)
    x4 = _each(_dot, x2b, x2b)
    x3 = _each(_dot, x1b, x2b)
    x4b = _bf16(x4)
    x8 = _each(_dot, x4b, x4b)
    p1 = _each(lambda u1, u2, u3: eye + u1 + u2 + u3, x1, x2, x3)
    x8b = _bf16(x8)
    x12 = _each(_dot, x4b, x8b)
    x16 = _each(_dot, x8b, x8b)
    p2 = _each(lambda u4, u8, u12: eye + u4 + u8 + u12, x4, x8, x12)
    p12 = _each(_dot, _bf16(p1), _bf16(p2))
    x16b = _bf16(x16)
    x32 = _each(_dot, x16b, x16b)
    x48 = _each(_dot, x16b, _bf16(x32))
    p3 = _each(lambda u16, u32, u48: eye + u16 + u32 + u48, x16, x32, x48)
    return _each(_dot, _bf16(p12), _bf16(p3))


def _head_lanes(width):
    lane = lax.broadcasted_iota(jnp.int32, (1, width), 1)
    return [(lane >= h * HEAD_DIM) & (lane < (h + 1) * HEAD_DIM) for h in range(width // HEAD_DIM)]


def _stack_heads(x, heads):
    return jnp.concatenate([jnp.where(in_head, x, 0.0) for in_head in heads], axis=0)


def _stacked_masks(n, reverse):
    row = lax.broadcasted_iota(jnp.int32, (n, n), 0)
    col = lax.broadcasted_iota(jnp.int32, (n, n), 1)
    t, s = row % CHUNK, col % CHUNK
    same_head = (row // CHUNK) == (col // CHUNK)
    earlier = ((s > t) if reverse else (s < t)) & same_head
    earlier_or_same = earlier | (row == col)
    return earlier, earlier_or_same, row == col


def _chunk_matrices(a_ka, a_kk, a_ra, a_rk, ks, rs, vs, kds, ads, eye):
    t_inv = _bf16(_unit_tri_inverse(a_ka, eye.astype(F32)))
    vsb, adsb, a_rab = _bf16(vs), _bf16(ads), _bf16(a_ra)
    av = _each(_dot, _bf16(a_kk), vsb)
    w1 = _bf16(_each(_dot, t_inv, _bf16(ks)))
    u0 = _bf16(_each(_dot, t_inv, _bf16(av)))
    rkv = _each(_dot, _bf16(a_rk), vsb)
    w2 = _each(lambda r, a, w: r - _dot(a, w), rs, a_rab, w1)
    y1 = _each(lambda y, a, u: y - _dot(a, u), rkv, a_rab, u0)
    mm = _each(_dot_tn, adsb, w1)
    nn = _each(lambda kd, v, ad, u: _dot_tn(kd, v) - _dot_tn(ad, u), _bf16(kds), vsb, adsb, u0)
    return list(zip(w2, y1, mm, nn))


def _store_chunk(refs, d, c0, n_heads, w2, y1, mm, nn, p_end):
    w2_ref, y1_ref, m_ref, n_ref = refs
    rows = pl.ds(c0, CHUNK)
    eye = (lax.broadcasted_iota(jnp.int32, (HEAD_DIM, HEAD_DIM), 0)
           == lax.broadcasted_iota(jnp.int32, (HEAD_DIM, HEAD_DIM), 1))
    for h in range(n_heads):
        lo, hi = h * HEAD_DIM, (h + 1) * HEAD_DIM
        r_lo, r_hi = h * CHUNK, (h + 1) * CHUNK
        w2_ref[d, 0, h, rows, :] = w2[r_lo:r_hi, lo:hi]
        y1_ref[d, 0, h, rows, :] = y1[r_lo:r_hi, lo:hi]
        m_ref[d, 0, h, rows, :] = jnp.where(eye, p_end[:, lo:hi], 0.0) - mm[lo:hi, lo:hi]
        n_ref[d, 0, h, rows, :] = nn[lo:hi, lo:hi]


def _scan_out_specs(b, n_heads, l):
    shape = jax.ShapeDtypeStruct((2, b, n_heads, l, HEAD_DIM), F32)
    spec = pl.BlockSpec((2, 1, n_heads, TM, HEAD_DIM), lambda i, t: (0, i, 0, t, 0))
    return [shape] * 4, [spec] * 4


def _seq_edges(t, n_ctx_tiles, n_tiles):
    first = (t == 0) | (t == n_ctx_tiles)
    last = (t == n_ctx_tiles - 1) | (t == n_tiles - 1)
    return first, last


def _fill_halo(ext_ref, x_ref, prev_ref, next_ref, first, last):
    ext_ref[0:8, :] = jnp.where(first, 0.0, prev_ref[0])
    ext_ref[8:8 + TM, :] = x_ref[0]
    ext_ref[8 + TM:16 + TM, :] = jnp.where(last, 0.0, next_ref[0])


def _halo_specs(width, n_tiles):
    per = TM // 8
    return [pl.BlockSpec((1, TM, width), lambda i, t: (i, t, 0)),
            pl.BlockSpec((1, 8, width), lambda i, t: (i, jnp.maximum(t * per - 1, 0), 0)),
            pl.BlockSpec((1, 8, width), lambda i, t: (i, jnp.minimum((t + 1) * per, n_tiles * per - 1), 0))]


def _full_spec(a):
    return pl.BlockSpec(a.shape, lambda i, t: (0,) * a.ndim)


def _rwkv_prep_kernel(x_ref, prev_ref, next_ref, mu_ref, w0_ref, wup_ref, a0_ref, aup_ref, gup_ref,
                      kk_ref, ka_ref, rk_ref, ones_ref,
                      w2_ref, y1_ref, m_ref, n_ref, bv_ref, gate_ref,
                      ext_ref, tok_ref, *, n_ctx_tiles, n_tiles, width):
    t = pl.program_id(1)
    first, last = _seq_edges(t, n_ctx_tiles, n_tiles)
    _fill_halo(ext_ref, x_ref, prev_ref, next_ref, first, last)
    p = ext_ref[8:8 + TM, :]
    prev = ext_ref[7:7 + TM, :]
    nxt = ext_ref[9:9 + TM, :]
    ps = p + mu_ref[...] * (0.5 * (prev + nxt) - p)
    r = ps[:, 0:width]
    k = ps[:, width:2 * width]
    v = ps[:, 2 * width:3 * width]
    lo = ps[:, 3 * width:3 * width + LANES]
    ones_bd = ones_ref[...]
    gate_ref[0] = _dot(jax.nn.sigmoid(lo), gup_ref[...])
    kq = k * kk_ref[...]
    kk = kq * lax.rsqrt(_dot(kq * kq, ones_bd, precision=HIGHEST) + NORM_EPS)
    tanh_lo = jnp.tanh(lo)
    tok_ref[0] = r
    tok_ref[1] = kk
    tok_ref[2] = v
    k_sum = jnp.zeros_like(k)
    for d in range(2):
        z = w0_ref[d:d + 1, :] + _dot(tanh_lo, wup_ref[d])
        w_log = -jax.nn.softplus(-z) - 0.5
        a = jax.nn.sigmoid(a0_ref[d:d + 1, :] + _dot(lo, aup_ref[d]))
        k_dir = k * (1.0 + (a - 1.0) * ka_ref[...])
        k_sum = k_sum + k_dir
        tok_ref[3 + 3 * d] = -jnp.exp(w_log)
        tok_ref[4 + 3 * d] = kk * a
        tok_ref[5 + 3 * d] = k_dir
    bonus = _dot(r * k_sum * rk_ref[...], ones_bd, precision=HIGHEST)
    bv_ref[0] = bonus * v

    heads = _head_lanes(width)
    out_refs = (w2_ref, y1_ref, m_ref, n_ref)

    def chunk_body(c, carry):
        args = [[] for _ in range(9)]
        where = []
        for cc in range(CHUNKS_PER_STEP):
            c0 = pl.multiple_of((c * CHUNKS_PER_STEP + cc) * CHUNK, CHUNK)
            rows = pl.ds(c0, CHUNK)
            r_c, kk_c, v_c = tok_ref[0, rows, :], tok_ref[1, rows, :], tok_ref[2, rows, :]
            vs = _stack_heads(v_c, heads)
            for d in range(2):
                logw, alpha, k_c = (tok_ref[3 + 3 * d, rows, :], tok_ref[4 + 3 * d, rows, :],
                                    tok_ref[5 + 3 * d, rows, :])
                incl, _, _ = _tri_masks(reverse=(d == 1))
                ci = _dot(incl.astype(F32), logw, precision=HIGHEST)
                e_ci, e_nci, e_ce = jnp.exp(ci), jnp.exp(-ci), jnp.exp(ci - logw)
                p_end = e_ci[0:1, :] if d == 1 else e_ci[CHUNK - 1:CHUNK, :]
                ks, rs, khs, ahs = (_stack_heads(u, heads)
                                    for u in (kk_c * e_ce, r_c * e_ci, k_c * e_nci, alpha * e_nci))
                earlier, earlier_or_same, eye = _stacked_masks(ks.shape[0], reverse=(d == 1))
                ksb, rsb, khsb, ahsb = _bf16((ks, rs, khs, ahs))
                a_ka = jnp.where(earlier, _dot_nt(ksb, ahsb), 0.0)
                a_kk = jnp.where(earlier, _dot_nt(ksb, khsb), 0.0)
                a_ra = jnp.where(earlier_or_same, _dot_nt(rsb, ahsb), 0.0)
                a_rk = jnp.where(earlier_or_same, _dot_nt(rsb, khsb), 0.0)
                for slot, value in zip(args, (a_ka, a_kk, a_ra, a_rk, ks, rs, vs, khs * p_end, ahs * p_end)):
                    slot.append(value)
                where.append((d, c0, p_end))
        for (d, c0, p_end), res in zip(where, _chunk_matrices(*args, eye)):
            _store_chunk(out_refs, d, c0, len(heads), *res, p_end)
        return carry

    lax.fori_loop(0, TM // CHUNK // CHUNKS_PER_STEP, chunk_body, 0)


def _rwkv_prepare(p_rw, pr, n_ctx):
    b, l, pw = p_rw.shape
    width = pr['rw_k_k'].shape[-1]
    n_heads = width // HEAD_DIM
    nt = l // TM
    kern = functools.partial(_rwkv_prep_kernel, n_ctx_tiles=n_ctx // TM, n_tiles=nt, width=width)
    params = [pr['rw_mu_p'], pr['rw_w0'], pr['rw_wup_p'], pr['rw_a0'], pr['rw_aup_p'], pr['rw_gup_p'],
              pr['rw_k_k'].reshape(1, width), pr['rw_k_a'].reshape(1, width), pr['rw_r_k'].reshape(1, width),
              pr['ones_bd']]
    scan_shapes, scan_specs = _scan_out_specs(b, n_heads, l)
    tok_spec = pl.BlockSpec((1, TM, width), lambda i, t: (i, t, 0))
    return pl.pallas_call(
        kern,
        grid=(b, nt),
        in_specs=_halo_specs(pw, nt) + [_full_spec(a) for a in params],
        out_specs=scan_specs + [tok_spec, tok_spec],
        out_shape=scan_shapes + [jax.ShapeDtypeStruct((b, l, width), F32)] * 2,
        scratch_shapes=[pltpu.VMEM((TM + 16, pw), F32), pltpu.VMEM((9, TM, width), F32)],
        compiler_params=_cparams(("arbitrary", "arbitrary")),
        name="rwkv_prepare",
    )(p_rw, p_rw, p_rw, *params)


def _scan_kernel(w2f_ref, y1f_ref, mf_ref, nf_ref, w2b_ref, y1b_ref, mb_ref, nb_ref, yf_ref, yb_ref, h_ref):
    @pl.when(pl.program_id(0) == 0)
    def _():
        h_ref[...] = jnp.zeros_like(h_ref)

    n_batch, n_heads = h_ref.shape[1], h_ref.shape[2]
    n_sub = yf_ref.shape[2] // CHUNK
    dirs = ((w2f_ref, y1f_ref, mf_ref, nf_ref, yf_ref), (w2b_ref, y1b_ref, mb_ref, nb_ref, yb_ref))
    chains = [(d, i, h) for d in range(2) for i in range(n_batch) for h in range(n_heads)]
    states = [h_ref[d, i, h] for d, i, h in chains]
    for j in range(n_sub):
        for c, (d, i, h) in enumerate(chains):
            w2_ref, y1_ref, m_ref, n_ref, y_ref = dirs[d]
            jj = j if d == 0 else n_sub - 1 - j
            rows = slice(jj * CHUNK, (jj + 1) * CHUNK)
            y_ref[i, h, rows, :] = _dot(w2_ref[0, i, h, rows, :], states[c]) + y1_ref[0, i, h, rows, :]
            states[c] = _dot(m_ref[0, i, h, rows, :], states[c], precision=HIGHEST) + n_ref[0, i, h, rows, :]
    for c, (d, i, h) in enumerate(chains):
        h_ref[d, i, h] = states[c]


def _scan(w2, y1, m, n, n_ctx):
    _, b, n_heads, l, _ = w2.shape
    nc = l // TM
    ncc = n_ctx // TM

    def bwd_chunk(i):
        return jnp.where(i < ncc, ncc - 1 - i, nc - 1 + ncc - i)

    blk = (1, b, n_heads, TM, HEAD_DIM)
    fwd = pl.BlockSpec(blk, lambda i: (0, 0, 0, i, 0))
    bwd = pl.BlockSpec(blk, lambda i: (1, 0, 0, bwd_chunk(i), 0))
    out_shape = jax.ShapeDtypeStruct((b, n_heads, l, HEAD_DIM), F32)
    return pl.pallas_call(
        _scan_kernel,
        grid=(nc,),
        in_specs=[fwd] * 4 + [bwd] * 4,
        out_specs=[pl.BlockSpec(blk[1:], lambda i: (0, 0, i, 0)),
                   pl.BlockSpec(blk[1:], lambda i: (0, 0, bwd_chunk(i), 0))],
        out_shape=[out_shape, out_shape],
        scratch_shapes=[pltpu.VMEM((2, b, n_heads, HEAD_DIM, HEAD_DIM), F32)],
        compiler_params=_cparams(("arbitrary",)),
        name="chunk_scan",
    )(w2, y1, m, n, w2, y1, m, n)


def _gdn_prep_kernel(x_ref, prev_ref, next_ref, cw_ref, alog_ref, dtb_ref, ones_ref,
                     w2_ref, y1_ref, m_ref, n_ref,
                     ext_ref, tok_ref, *, n_ctx_tiles, n_tiles, width):
    t = pl.program_id(1)
    first, last = _seq_edges(t, n_ctx_tiles, n_tiles)
    _fill_halo(ext_ref, x_ref, prev_ref, next_ref, first, last)
    half = GD_CONV // 2
    conv = jnp.zeros((TM, 3 * width), F32)
    for j in range(GD_CONV):
        conv = conv + cw_ref[j:j + 1, :] * ext_ref[8 + j - half:8 + j - half + TM, 0:3 * width]
    qkv = conv * jax.nn.sigmoid(conv)
    ones_bd = ones_ref[...]

    def l2n(u):
        return u * lax.rsqrt(_dot(u * u, ones_bd, precision=HIGHEST) + NORM_EPS)

    q = l2n(qkv[:, 0:width]) * HEAD_DIM ** -0.5
    k = l2n(qkv[:, width:2 * width])
    v = qkv[:, 2 * width:3 * width]
    ab = ext_ref[8:8 + TM, 4 * width:4 * width + LANES]
    g_all = -jnp.exp(alog_ref[...]) * jax.nn.softplus(ab + dtb_ref[...])
    beta_all = jax.nn.sigmoid(ab)
    heads = _head_lanes(width)
    n_heads = len(heads)
    tok_ref[0] = q
    tok_ref[1] = k
    tok_ref[2] = v
    for d in range(2):
        g = jnp.zeros((TM, width), F32)
        beta = jnp.zeros((TM, width), F32)
        for h, in_head in enumerate(heads):
            ja = d * n_heads + h
            jb = 2 * n_heads + ja
            g = jnp.where(in_head, g_all[:, ja:ja + 1], g)
            beta = jnp.where(in_head, beta_all[:, jb:jb + 1], beta)
        tok_ref[3 + 2 * d] = g
        tok_ref[4 + 2 * d] = k * beta
    out_refs = (w2_ref, y1_ref, m_ref, n_ref)

    def chunk_inputs(c0, d):
        rows = pl.ds(c0, CHUNK)
        q_c, k_c, v_c = tok_ref[0, rows, :], tok_ref[1, rows, :], tok_ref[2, rows, :]
        g, kb = tok_ref[3 + 2 * d, rows, :], tok_ref[4 + 2 * d, rows, :]
        incl, _, _ = _tri_masks(reverse=(d == 1))
        ci = _dot(incl.astype(F32), g, precision=HIGHEST)
        ce = ci - g
        ci_end = ci[0:1, :] if d == 1 else ci[CHUNK - 1:CHUNK, :]
        ks, rs, kbs, vs, kds, ads = (_stack_heads(u, heads) for u in (
            k_c * jnp.exp(ce), q_c * jnp.exp(ci), kb, v_c,
            kb * jnp.exp(ci_end - ci), kb * jnp.exp(ci_end - ce)))
        earlier, earlier_or_same, eye = _stacked_masks(ks.shape[0], reverse=(d == 1))
        ci_col = jnp.concatenate([ci[:, h * HEAD_DIM:h * HEAD_DIM + 1] for h in range(n_heads)], axis=0)
        ce_col = jnp.concatenate([ce[:, h * HEAD_DIM:h * HEAD_DIM + 1] for h in range(n_heads)], axis=0)
        ci_row = jnp.sum(jnp.where(eye, ci_col, 0.0), axis=0, keepdims=True)
        ce_row = jnp.sum(jnp.where(eye, ce_col, 0.0), axis=0, keepdims=True)
        kbsb = kbs.astype(BF16)
        kkb = _dot_nt(_stack_heads(k_c, heads).astype(BF16), kbsb)
        qkb = _dot_nt(_stack_heads(q_c, heads).astype(BF16), kbsb)
        a_ka = kkb * jnp.exp(jnp.where(earlier, ce_col - ce_row, MASKED))
        a_kk = kkb * jnp.exp(jnp.where(earlier, ce_col - ci_row, MASKED))
        a_ra = qkb * jnp.exp(jnp.where(earlier_or_same, ci_col - ce_row, MASKED))
        a_rk = qkb * jnp.exp(jnp.where(earlier_or_same, ci_col - ci_row, MASKED))
        return (a_ka, a_kk, a_ra, a_rk, ks, rs, vs, kds, ads), jnp.exp(ci_end), eye

    def chunk_body(c, carry):
        args = [[] for _ in range(9)]
        where = []
        for cc in range(CHUNKS_PER_STEP):
            c0 = pl.multiple_of((c * CHUNKS_PER_STEP + cc) * CHUNK, CHUNK)
            for d in range(2):
                values, p_end, eye = chunk_inputs(c0, d)
                for slot, value in zip(args, values):
                    slot.append(value)
                where.append((d, c0, p_end))
        for (d, c0, p_end), res in zip(where, _chunk_matrices(*args, eye)):
            _store_chunk(out_refs, d, c0, n_heads, *res, p_end)
        return carry

    lax.fori_loop(0, TM // CHUNK // CHUNKS_PER_STEP, chunk_body, 0)


def _gdn_prepare(p_gd, pr, n_ctx):
    b, l, pw = p_gd.shape
    width = pr['gd_width']
    n_heads = width // HEAD_DIM
    nt = l // TM
    kern = functools.partial(_gdn_prep_kernel, n_ctx_tiles=n_ctx // TM, n_tiles=nt, width=width)
    params = [pr['gd_conv_w'], pr['gd_alog_p'], pr['gd_dtb_p'], pr['ones_bd']]
    scan_shapes, scan_specs = _scan_out_specs(b, n_heads, l)
    return pl.pallas_call(
        kern,
        grid=(b, nt),
        in_specs=_halo_specs(pw, nt) + [_full_spec(a) for a in params],
        out_specs=scan_specs,
        out_shape=scan_shapes,
        scratch_shapes=[pltpu.VMEM((TM + 16, pw), F32), pltpu.VMEM((7, TM, width), F32)],
        compiler_params=_cparams(("arbitrary", "arbitrary")),
        name="gdn_prepare",
    )(p_gd, p_gd, p_gd, *params)


def _outproj_kernel(x_ref, mod_ref, na_ref, ryf_ref, ryb_ref, bv_ref, gate_ref, gyf_ref, gyb_ref, z_ref,
                    lnw_ref, lnb_ref, gng_ref, wo_ref, g2_ref, rw_ref, rb_ref,
                    x1_ref, h2_ref, route_ref, meta_ref, *, na_w, rw_w):
    n_heads = ryf_ref.shape[1]
    rw_parts, gd_parts = [], []
    for h in range(n_heads):
        y = ryf_ref[0, h] + ryb_ref[0, h]
        yc = y - jnp.mean(y, axis=-1, keepdims=True)
        rw_parts.append(yc * lax.rsqrt(jnp.mean(yc * yc, axis=-1, keepdims=True) + RW_GN_EPS))
        gd_parts.append(_rms(gyf_ref[0, h] + gyb_ref[0, h]) * gng_ref[...])
    yn = jnp.concatenate(rw_parts, axis=-1)
    o_rw = (yn * lnw_ref[...] + lnb_ref[...] + bv_ref[0]) * gate_ref[0]
    z = z_ref[0]
    o_gd = jnp.concatenate(gd_parts, axis=-1) * (z * jax.nn.sigmoid(z))
    o = (_dot(na_ref[0], wo_ref[0:na_w, :])
         + _dot(o_rw.astype(BF16), wo_ref[na_w:na_w + rw_w, :])
         + _dot(o_gd.astype(BF16), wo_ref[na_w + rw_w:, :]))
    mod = mod_ref[0]
    x1 = x_ref[0] + mod[2:3] * o
    x1_ref[0] = x1
    h2 = _rms(x1) * g2_ref[...] * (1.0 + mod[4:5]) + mod[3:4]
    h2_ref[0] = h2
    logits = _dot(h2, rw_ref[...], precision=HIGHEST) + rb_ref[...]
    lane = lax.broadcasted_iota(jnp.int32, logits.shape, 1)
    chosen, weights = [], []
    onehot = jnp.zeros(logits.shape, F32)
    for kk in range(TOP_K):
        m = jnp.max(logits, axis=-1, keepdims=True)
        idx = jnp.min(jnp.where(logits == m, lane, LANES), axis=-1, keepdims=True)
        chosen.append(idx)
        top_max = m if kk == 0 else top_max
        weights.append(jnp.exp(m - top_max))
        onehot = jnp.where(lane == idx, 1.0, onehot)
        logits = jnp.where(lane == idx, -jnp.inf, logits)
    denom = weights[0] + weights[1] + weights[2] + weights[3]
    tok_i = lax.broadcasted_iota(jnp.int32, (TM, TM), 0)
    tok_j = lax.broadcasted_iota(jnp.int32, (TM, TM), 1)
    before = jnp.where(tok_j < tok_i, 1.0, 0.0).astype(BF16)
    rank = _dot(before, onehot.astype(BF16))
    counts = jnp.sum(onehot, axis=0, keepdims=True)
    count8 = jnp.floor((counts + 7.0) * 0.125) * 8.0
    exp_i = lax.broadcasted_iota(jnp.int32, (LANES, LANES), 0)
    exp_j = lax.broadcasted_iota(jnp.int32, (LANES, LANES), 1)
    seg_start = _dot(jnp.broadcast_to(count8, (8, LANES)), jnp.where(exp_i < exp_j, 1.0, 0.0),
                     precision=HIGHEST)[0:1]
    slot_of = seg_start + rank
    route = jnp.zeros(logits.shape, F32)
    for kk in range(TOP_K):
        slot = jnp.sum(jnp.where(lane == chosen[kk], slot_of, 0.0), axis=-1, keepdims=True)
        route = jnp.where(lane == kk, weights[kk] / denom, route)
        route = jnp.where(lane == TOP_K + kk, slot, route)
    route_ref[0] = route
    row8 = lax.broadcasted_iota(jnp.int32, (8, LANES), 0)
    meta_ref[0, 0] = jnp.where(row8 == 0, count8, jnp.where(row8 == 1, seg_start, 0.0))


def _outproj(xa, mods, o_na, ry, bv, gate, gy, p_gd, pr, n_ctx):
    b, l, d = xa.shape
    nt = l // TM
    na_w = o_na.shape[-1]
    n_heads = ry[0].shape[1]
    rw_w = n_heads * HEAD_DIM
    kern = functools.partial(_outproj_kernel, na_w=na_w, rw_w=rw_w)
    tok = lambda w: pl.BlockSpec((1, TM, w), lambda i, t: (i, t, 0))
    head_major = pl.BlockSpec((1, n_heads, TM, HEAD_DIM), lambda i, t: (i, 0, t, 0))
    params = [pr['rw_ln_w'].reshape(1, rw_w), pr['rw_ln_b'].reshape(1, rw_w), pr['gd_norm_g'].reshape(1, HEAD_DIM),
              pr['w_out_bf'], pr['norm_ffn_g'], pr['router_w_p'], pr['router_b_p']]
    return pl.pallas_call(
        kern,
        grid=(b, nt),
        in_specs=[tok(d), pl.BlockSpec((1, 6, d), _mod_index(n_ctx // TM, b)), tok(na_w),
                  head_major, head_major, tok(rw_w), tok(rw_w), head_major, head_major,
                  pl.BlockSpec((1, TM, rw_w), lambda i, t: (i, t, 3))]
                 + [_full_spec(a) for a in params],
        out_specs=[tok(d), tok(d), tok(LANES), pl.BlockSpec((1, 1, 8, LANES), lambda i, t: (i, t, 0, 0))],
        out_shape=[jax.ShapeDtypeStruct((b, l, d), F32), jax.ShapeDtypeStruct((b, l, d), F32),
                   jax.ShapeDtypeStruct((b, l, LANES), F32), jax.ShapeDtypeStruct((b, nt, 8, LANES), F32)],
        compiler_params=_cparams(("arbitrary", "arbitrary")),
        name="outproj_router",
    )(xa, mods, o_na, ry[0], ry[1], bv, gate, gy[0], gy[1], p_gd, *params)


MOE_BM = 256
SEG_ALIGN = 8
SEG_BITS = (32, 16, 8, 4, 2, 1)


def _n_slots(n_exp):
    return _round_up(TOP_K * TM + n_exp * (SEG_ALIGN - 1), LANES)


def _segment_dma(segs_ref, buf_ref, hbm_ref, sem, n_exp, to_hbm, wait):
    def body(e, carry):
        start = segs_ref[0, 0, e]
        units = segs_ref[0, 0, n_exp + e]
        offset = segs_ref[0, 0, 2 * n_exp + e]
        for bit in SEG_BITS:
            done = (units & ~(2 * bit - 1)) * SEG_ALIGN
            rows = bit * SEG_ALIGN

            @pl.when((units & bit) != 0)
            def _():
                in_buf = buf_ref.at[pl.ds(pl.multiple_of(start + done, SEG_ALIGN), rows), :]
                in_hbm = hbm_ref.at[pl.ds(pl.multiple_of(offset + done, SEG_ALIGN), rows), :]
                copy = (pltpu.make_async_copy(in_buf, in_hbm, sem) if to_hbm
                        else pltpu.make_async_copy(in_hbm, in_buf, sem))
                if wait:
                    copy.wait()
                else:
                    copy.start()
        return carry
    lax.fori_loop(0, n_exp, body, 0)


def _slot_lanes(route, n_slots):
    lane = lax.broadcasted_iota(jnp.int32, (route.shape[0], n_slots), 1)
    slots = [route[:, TOP_K + k:TOP_K + k + 1].astype(jnp.int32) for k in range(TOP_K)]
    return lane, slots


def _dispatch_kernel(segs_ref, h_ref, route_ref, xb_in_ref, xb_ref, xs_ref, sem, *, n_exp):
    del xb_in_ref
    lane, slots = _slot_lanes(route_ref[...], xs_ref.shape[0])
    picked = lane == slots[0]
    for k in range(1, TOP_K):
        picked = picked | (lane == slots[k])
    onehot = jnp.where(picked, 1.0, 0.0).astype(BF16)
    xs_ref[...] = _dot_tn(onehot, h_ref[...].astype(BF16))
    _segment_dma(segs_ref, xs_ref, xb_ref, sem, n_exp, to_hbm=True, wait=False)
    _segment_dma(segs_ref, xs_ref, xb_ref, sem, n_exp, to_hbm=True, wait=True)


def _dispatch(h2, route, segs, n_pad, n_exp):
    t, d = h2.shape
    nt = t // TM
    n_slots = _n_slots(n_exp)
    kern = functools.partial(_dispatch_kernel, n_exp=n_exp)
    return pl.pallas_call(
        kern,
        grid=(nt,),
        in_specs=[pl.BlockSpec((1, 1, LANES), lambda i: (i, 0, 0), memory_space=pltpu.SMEM),
                  pl.BlockSpec((TM, d), lambda i: (i, 0)),
                  pl.BlockSpec((TM, LANES), lambda i: (i, 0)),
                  pl.BlockSpec(memory_space=pl.ANY)],
        out_specs=pl.BlockSpec(memory_space=pl.ANY),
        out_shape=jax.ShapeDtypeStruct((n_pad, d), F32),
        input_output_aliases={3: 0},
        scratch_shapes=[pltpu.VMEM((n_slots, d), F32), pltpu.SemaphoreType.DMA(())],
        compiler_params=_cparams(("arbitrary",)),
        name="moe_dispatch",
    )(segs, h2, route, jnp.zeros((n_pad, d), F32))


def _expert_kernel(blk_e_ref, x_ref, wgu_ref, bgu_ref, wd_ref, bd_ref, y_ref):
    del blk_e_ref
    x = x_ref[...].astype(BF16)
    d_exp = wd_ref.shape[1]
    gu = _dot(x, wgu_ref[0]) + bgu_ref[0]
    gate = jnp.minimum(gu[:, :d_exp], SWIGLU_LIMIT)
    lin = jnp.clip(gu[:, d_exp:], -SWIGLU_LIMIT, SWIGLU_LIMIT)
    act = gate * jax.nn.sigmoid(SWIGLU_ALPHA * gate) * (lin + 1.0)
    y_ref[...] = _dot(act.astype(BF16), wd_ref[0]) + bd_ref[0]


def _expert_blocks(xb, blk_e, w_gu, b_gu, w_down, b_down):
    n_pad, d = xb.shape
    n_blocks = n_pad // MOE_BM
    n_exp, _, d_gu = w_gu.shape
    grid_spec = pltpu.PrefetchScalarGridSpec(
        num_scalar_prefetch=1,
        grid=(n_blocks,),
        in_specs=[pl.BlockSpec((MOE_BM, d), lambda i, be: (i, 0)),
                  pl.BlockSpec((1, d, d_gu), lambda i, be: (be[i], 0, 0)),
                  pl.BlockSpec((1, 1, d_gu), lambda i, be: (be[i], 0, 0)),
                  pl.BlockSpec((1, d_gu // 2, d), lambda i, be: (be[i], 0, 0)),
                  pl.BlockSpec((1, 1, d), lambda i, be: (be[i], 0, 0))],
        out_specs=pl.BlockSpec((MOE_BM, d), lambda i, be: (i, 0)))
    return pl.pallas_call(
        _expert_kernel,
        grid_spec=grid_spec,
        out_shape=jax.ShapeDtypeStruct((n_pad, d), F32),
        compiler_params=_cparams(("arbitrary",)),
        name="moe_experts",
    )(blk_e, xb, w_gu, b_gu.reshape(n_exp, 1, d_gu), w_down, b_down.reshape(n_exp, 1, d))


def _combine_kernel(segs_ref, y_hbm, x_ref, mod_ref, route_ref, o_ref, ybuf, sem, *, n_exp):
    @pl.when(pl.program_id(0) == 0)
    def _():
        ybuf[...] = jnp.zeros_like(ybuf)

    _segment_dma(segs_ref, ybuf, y_hbm, sem, n_exp, to_hbm=False, wait=False)
    _segment_dma(segs_ref, ybuf, y_hbm, sem, n_exp, to_hbm=False, wait=True)
    y = ybuf[...].astype(BF16)
    route = route_ref[...]
    lane, slots = _slot_lanes(route, ybuf.shape[0])
    weights = jnp.zeros(lane.shape, F32)
    for k in range(TOP_K):
        weights = jnp.where(lane == slots[k], route[:, k:k + 1], weights)
    o_ref[...] = x_ref[...] + mod_ref[0, 5:6, :] * _dot(weights.astype(BF16), y)


def _combine(yb, segs, x1, mods, route, n_batch, l, n_ctx, n_exp):
    t, d = x1.shape
    nt = t // TM
    per_sample = l // TM
    n_ctx_tiles = n_ctx // TM
    kern = functools.partial(_combine_kernel, n_exp=n_exp)

    def mod_index(i):
        return (jnp.where(i % per_sample < n_ctx_tiles, n_batch, i // per_sample), 0, 0)

    return pl.pallas_call(
        kern,
        grid=(nt,),
        in_specs=[pl.BlockSpec((1, 1, LANES), lambda i: (i, 0, 0), memory_space=pltpu.SMEM),
                  pl.BlockSpec(memory_space=pl.ANY),
                  pl.BlockSpec((TM, d), lambda i: (i, 0)),
                  pl.BlockSpec((1, 6, d), mod_index),
                  pl.BlockSpec((TM, LANES), lambda i: (i, 0))],
        out_specs=pl.BlockSpec((TM, d), lambda i: (i, 0)),
        out_shape=jax.ShapeDtypeStruct((t, d), F32),
        scratch_shapes=[pltpu.VMEM((_n_slots(n_exp), d), F32), pltpu.SemaphoreType.DMA(())],
        compiler_params=_cparams(("arbitrary",)),
        name="moe_combine",
    )(segs, yb, x1, mods, route)


def _route(meta, n_exp, n_tokens):
    nt = meta.shape[0]
    count8 = meta[:, 0, :n_exp].astype(jnp.int32)
    seg_start = meta[:, 1, :n_exp].astype(jnp.int32)
    per_expert = jnp.sum(count8, axis=0)
    padded = (per_expert + MOE_BM - 1) // MOE_BM * MOE_BM
    pad_end = jnp.cumsum(padded)
    offset = (pad_end - padded)[None, :] + jnp.cumsum(count8, axis=0) - count8
    n_pad = _round_up(n_tokens * TOP_K + nt * n_exp * (SEG_ALIGN - 1), MOE_BM) + n_exp * MOE_BM
    block_row = jnp.arange(n_pad // MOE_BM, dtype=jnp.int32) * MOE_BM
    blk_e = jnp.minimum(jnp.sum(block_row[:, None] >= pad_end[None, :], axis=1), n_exp - 1).astype(jnp.int32)
    segs = jnp.concatenate([seg_start, count8 // SEG_ALIGN, offset,
                            jnp.zeros((nt, LANES - 3 * n_exp), jnp.int32)], axis=1)
    return segs.reshape(nt, 1, LANES), blk_e, n_pad


def _channel_sublayer(x1, h2, route, meta, mods, pr, n_ctx):
    b, l, d = x1.shape
    t = b * l
    n_exp = pr['moe_w_gu_bf'].shape[0]
    assert 3 * n_exp <= LANES
    route = route.reshape(t, LANES)
    segs, blk_e, n_pad = _route(meta.reshape(t // TM, 8, LANES), n_exp, t)
    xb = _dispatch(h2.reshape(t, d), route, segs, n_pad, n_exp)
    yb = _expert_blocks(xb, blk_e, pr['moe_w_gu_bf'], pr['moe_b_gu'], pr['moe_w_down_bf'], pr['moe_b_down'])
    x2 = _combine(yb, segs, x1.reshape(t, d), mods, route, b, l, n_ctx, n_exp)
    return x2.reshape(b, l, d)


def _pad_cols(w, n):
    return jnp.pad(w, ((0, 0), (0, n - w.shape[1])))


def _round_up(n, m):
    return -(-n // m) * m


def _prep_layer_params(pr):
    d = pr['w_in'].shape[0]
    na_w = d // 2
    rw_w = d // 4
    gd_w = d - na_w - rw_w
    na_cols = 3 * na_w
    rw_cols = 3 * rw_w + RW_DECAY_LORA + RW_ICLR_LORA + RW_GATE_LORA
    rw_pad = _round_up(rw_cols, LANES)
    gd_cols = pr['w_in'].shape[1] - na_cols - rw_cols
    gd_pad = _round_up(gd_cols, LANES)
    w_in = pr['w_in']
    out = dict(pr)
    out['w_in'] = jnp.concatenate([w_in[:, :na_cols],
                                   _pad_cols(w_in[:, na_cols:na_cols + rw_cols], rw_pad),
                                   _pad_cols(w_in[:, na_cols + rw_cols:], gd_pad)], axis=1).astype(BF16)
    out['rw_w'] = rw_pad
    out['gd_w'] = gd_pad
    n_heads = na_w // HEAD_DIM
    out['qkg'] = jnp.stack([jnp.tile(pr['na_q_gain'], n_heads) * HEAD_DIM ** -0.5,
                            jnp.tile(pr['na_k_gain'], n_heads)])
    head_of = np.arange(na_w) // HEAD_DIM
    out['avg'] = jnp.asarray((head_of[:, None] == head_of[None, :]) / HEAD_DIM, BF16)
    out['norm_mix_g'] = pr['norm_mix_g'].reshape(1, d)
    out['norm_ffn_g'] = pr['norm_ffn_g'].reshape(1, d)
    out['rw_mu_p'] = jnp.pad(pr['rw_mu'], (0, rw_pad - rw_cols)).reshape(1, rw_pad)
    o1, o2 = RW_DECAY_LORA, RW_DECAY_LORA + RW_ICLR_LORA
    out['rw_wup_p'] = jnp.pad(pr['rw_w_up'], ((0, 0), (0, LANES - o1), (0, 0)))
    out['rw_aup_p'] = jnp.pad(pr['rw_a_up'], ((0, 0), (o1, LANES - o2), (0, 0)))
    out['rw_gup_p'] = jnp.pad(pr['rw_g_up'], ((o2, LANES - o2 - RW_GATE_LORA), (0, 0)))
    head_of = np.arange(rw_w) // HEAD_DIM
    out['ones_bd'] = jnp.asarray(head_of[:, None] == head_of[None, :], F32)
    out['gd_width'] = gd_w
    n_ab = pr['gd_A_log'].size
    out['gd_alog_p'] = jnp.pad(pr['gd_A_log'].reshape(-1), (0, LANES - n_ab)).reshape(1, LANES)
    out['gd_dtb_p'] = jnp.pad(pr['gd_dt_bias'].reshape(-1), (0, LANES - n_ab)).reshape(1, LANES)
    out['w_out_bf'] = pr['w_out'].astype(BF16)
    n_exp = pr['moe_router_w'].shape[1]
    out['router_w_p'] = _pad_cols(pr['moe_router_w'], LANES)
    out['router_b_p'] = jnp.concatenate([pr['moe_router_b'],
                                         jnp.full((LANES - n_exp,), MASKED, F32)]).reshape(1, LANES)
    out['moe_w_gu_bf'] = pr['moe_w_gu'].astype(BF16)
    out['moe_w_down_bf'] = pr['moe_w_down'].astype(BF16)
    return out


def _mixing_sublayer(xa, mods, pr, n_ctx):
    b, l, d = xa.shape
    na_w = d // 2
    p_na, p_rw, p_gd = _inproj(xa, mods, pr['norm_mix_g'], pr['w_in'], pr['qkg'], pr['avg'], n_ctx,
                               na_w, pr['rw_w'], pr['gd_w'])
    o_na = _na_attention(p_na, pr['na_bias'], n_ctx, na_w)
    rw2, ry1, rm, rn, bv, gate = _rwkv_prepare(p_rw, pr, n_ctx)
    ry = _scan(rw2, ry1, rm, rn, n_ctx)
    gy = _scan(*_gdn_prepare(p_gd, pr, n_ctx), n_ctx)
    return _outproj(xa, mods, o_na, ry, bv, gate, gy, p_gd, pr, n_ctx)


_LAYER_PARAMS = ('norm_mix_g', 'norm_ffn_g', 'w_in', 'w_out', 'na_q_gain', 'na_k_gain', 'na_rpb',
                 'rw_mu', 'rw_w0', 'rw_w_up', 'rw_a0', 'rw_a_up', 'rw_g_up', 'rw_k_k', 'rw_k_a', 'rw_r_k',
                 'rw_ln_w', 'rw_ln_b', 'gd_conv_w', 'gd_A_log', 'gd_dt_bias', 'gd_norm_g',
                 'moe_router_w', 'moe_router_b', 'moe_w_gu', 'moe_b_gu', 'moe_w_down', 'moe_b_down')


def kernel(x, c, ctx, c_ctx, ada_w, ada_b, norm_mix_g, norm_ffn_g, w_in, w_out, na_q_gain, na_k_gain, na_rpb, rw_mu, rw_w0, rw_w_up, rw_a0, rw_a_up, rw_g_up, rw_k_k, rw_k_a, rw_r_k, rw_ln_w, rw_ln_b, gd_conv_w, gd_A_log, gd_dt_bias, gd_norm_g, moe_router_w, moe_router_b, moe_w_gu, moe_b_gu, moe_w_down, moe_b_down):
    stacked = dict(zip(_LAYER_PARAMS, (norm_mix_g, norm_ffn_g, w_in, w_out, na_q_gain, na_k_gain, na_rpb,
                                       rw_mu, rw_w0, rw_w_up, rw_a0, rw_a_up, rw_g_up, rw_k_k, rw_k_a, rw_r_k,
                                       rw_ln_w, rw_ln_b, gd_conv_w, gd_A_log, gd_dt_bias, gd_norm_g,
                                       moe_router_w, moe_router_b, moe_w_gu, moe_b_gu, moe_w_down, moe_b_down)))
    b, s, d = x.shape
    n_ctx = ctx.shape[1]
    depth = ada_w.shape[0]
    assert b + 1 <= 8 and n_ctx % TM == 0 and s % TM == 0 and s % GRID_W == 0
    cvec = jnp.zeros((8, d), F32).at[:b].set(c).at[b].set(c_ctx)
    mods_all = _adaln(cvec, ada_w, ada_b).reshape(depth, 8, 6, d)[:, :b + 1]
    xa = jnp.concatenate([ctx, x], axis=1)
    for layer in range(depth):
        pr = _prep_layer_params({name: value[layer] for name, value in stacked.items()})
        pr['na_bias'] = _na_bias_tables(pr['na_rpb'], s // GRID_W)
        mods = mods_all[layer]
        x1, h2, route, meta = _mixing_sublayer(xa, mods, pr, n_ctx)
        xa = _channel_sublayer(x1, h2, route, meta, mods, pr, n_ctx)
    return xa[:, n_ctx:]
```

```python
import functools
import math

import numpy as np
import jax
import jax.numpy as jnp
from jax import lax
from jax.experimental import pallas as pl
from jax.experimental.pallas import tpu as pltpu

F32 = jnp.float32
BF16 = jnp.bfloat16
HIGHEST = lax.Precision.HIGHEST

GRID_W = 64
NORM_EPS = 1e-6
HEAD_DIM = 64
NA_WIN_ROWS = 8
NA_WIN_COLS = 16
RW_DECAY_LORA = 32
RW_ICLR_LORA = 32
RW_GATE_LORA = 64
RW_GN_EPS = 64e-5
GD_CONV = 5
TOP_K = 4
SWIGLU_LIMIT = 7.0
SWIGLU_ALPHA = 1.702

LANES = 128
TM = 256
NA_QROWS = 4
NA_KROWS = NA_QROWS + NA_WIN_ROWS
CHUNK = 64
CHUNKS_PER_STEP = 2
MASKED = -1e30
VMEM_LIMIT = 48 * 1024 * 1024


def _cparams(sem):
    return pltpu.CompilerParams(dimension_semantics=sem, vmem_limit_bytes=VMEM_LIMIT)


def _dot(a, b, **kw):
    return jnp.dot(a, b, preferred_element_type=F32, **kw)


def _dot_split(a, b_bf16):
    hi = a.astype(BF16)
    lo = (a - hi.astype(F32)).astype(BF16)
    return _dot(hi, b_bf16) + _dot(lo, b_bf16)


def _dot_nt(a, b, **kw):
    return lax.dot_general(a, b, (((1,), (1,)), ((), ())), preferred_element_type=F32, **kw)


def _dot_tn(a, b, **kw):
    return lax.dot_general(a, b, (((0,), (0,)), ((), ())), preferred_element_type=F32, **kw)


def _adaln_kernel(c_ref, w_ref, b_ref, o_ref):
    c = c_ref[...]
    s = c * jax.nn.sigmoid(c)
    o_ref[0] = _dot(s.astype(BF16), w_ref[0].astype(BF16)) + b_ref[0]


def _adaln(cvec, ada_w, ada_b):
    depth, d, n = ada_w.shape
    tn = n // 4
    return pl.pallas_call(
        _adaln_kernel,
        grid=(depth, n // tn),
        in_specs=[pl.BlockSpec((8, d), lambda l, j: (0, 0)),
                  pl.BlockSpec((1, d, tn), lambda l, j: (l, 0, j)),
                  pl.BlockSpec((1, 1, tn), lambda l, j: (l, 0, j))],
        out_specs=pl.BlockSpec((1, 8, tn), lambda l, j: (l, 0, j)),
        out_shape=jax.ShapeDtypeStruct((depth, 8, n), F32),
        compiler_params=_cparams(("arbitrary", "arbitrary")),
        name="adaln",
    )(cvec, ada_w, ada_b.reshape(depth, 1, n))


def _mod_index(n_ctx_tiles, n_batch):
    return lambda b, t: (jnp.where(t < n_ctx_tiles, n_batch, b), 0, 0)


def _rms(x):
    return x * lax.rsqrt(jnp.mean(x * x, axis=-1, keepdims=True) + NORM_EPS)


def _inproj_kernel(x_ref, mod_ref, g_ref, w_ref, qkg_ref, avg_ref, na_ref, rw_ref, gd_ref, *, na_w, rw_w):
    mod = mod_ref[0]
    h = _rms(x_ref[0]) * g_ref[...]
    hb = (h * (1.0 + mod[1:2]) + mod[0:1]).astype(BF16)
    pa = _dot(hb, w_ref[:, 0:3 * na_w])
    avg = avg_ref[...]
    qkg = qkg_ref[...]
    q = pa[:, 0:na_w]
    k = pa[:, na_w:2 * na_w]
    qn = q * lax.rsqrt(_dot(q * q, avg) + NORM_EPS) * qkg[0:1]
    kn = k * lax.rsqrt(_dot(k * k, avg) + NORM_EPS) * qkg[1:2]
    na_ref[0, :, 0:na_w] = qn.astype(BF16)
    na_ref[0, :, na_w:2 * na_w] = kn.astype(BF16)
    na_ref[0, :, 2 * na_w:3 * na_w] = pa[:, 2 * na_w:3 * na_w].astype(BF16)
    rw_ref[0] = _dot(hb, w_ref[:, 3 * na_w:3 * na_w + rw_w])
    gd_ref[0] = _dot(hb, w_ref[:, 3 * na_w + rw_w:])


def _inproj(xa, mods, g, w_all, qkg, avg, n_ctx, na_w, rw_w, gd_w):
    b, l, d = xa.shape
    nt = l // TM
    kern = functools.partial(_inproj_kernel, na_w=na_w, rw_w=rw_w)
    return pl.pallas_call(
        kern,
        grid=(b, nt),
        in_specs=[pl.BlockSpec((1, TM, d), lambda i, t: (i, t, 0)),
                  pl.BlockSpec((1, 6, d), _mod_index(n_ctx // TM, b)),
                  pl.BlockSpec((1, d), lambda i, t: (0, 0)),
                  pl.BlockSpec(w_all.shape, lambda i, t: (0, 0)),
                  pl.BlockSpec(qkg.shape, lambda i, t: (0, 0)),
                  pl.BlockSpec(avg.shape, lambda i, t: (0, 0))],
        out_specs=[pl.BlockSpec((1, TM, 3 * na_w), lambda i, t: (i, t, 0)),
                   pl.BlockSpec((1, TM, rw_w), lambda i, t: (i, t, 0)),
                   pl.BlockSpec((1, TM, gd_w), lambda i, t: (i, t, 0))],
        out_shape=[jax.ShapeDtypeStruct((b, l, 3 * na_w), BF16),
                   jax.ShapeDtypeStruct((b, l, rw_w), F32),
                   jax.ShapeDtypeStruct((b, l, gd_w), F32)],
        compiler_params=_cparams(("arbitrary", "arbitrary")),
        name="inproj",
    )(xa, mods, g, w_all, qkg, avg)


def _na_bias_tables(rpb, rows):
    n_heads = rpb.shape[0]
    nq, nk = NA_QROWS * GRID_W, NA_KROWS * GRID_W
    qc = np.arange(GRID_W)[:, None]
    kc = np.arange(GRID_W)[None, :]
    c0 = np.clip(qc - NA_WIN_COLS // 2, 0, GRID_W - NA_WIN_COLS)
    col_ok = (kc >= c0) & (kc < c0 + NA_WIN_COLS)
    dc = kc - qc + NA_WIN_COLS - 1
    pick_col = (dc[None] == np.arange(2 * NA_WIN_COLS - 1)[:, None, None]).astype(np.float32)
    by_col = jnp.einsum('hrd,dqk->hrqk', rpb, pick_col, precision=HIGHEST)
    tables = []
    for r0 in (0, NA_QROWS, rows - NA_QROWS):
        ks = int(np.clip(r0 - NA_WIN_ROWS // 2, 0, rows - NA_KROWS))
        qr = (r0 + np.arange(NA_QROWS))[:, None]
        kr = (ks + np.arange(NA_KROWS))[None, :]
        s0 = np.clip(qr - NA_WIN_ROWS // 2, 0, rows - NA_WIN_ROWS)
        row_ok = (kr >= s0) & (kr < s0 + NA_WIN_ROWS)
        dr = kr - qr + NA_WIN_ROWS - 1
        pick_row = ((dr[None] == np.arange(2 * NA_WIN_ROWS - 1)[:, None, None]) & row_ok[None]).astype(np.float32)
        bias = jnp.einsum('rab,hrqk->haqbk', pick_row, by_col, precision=HIGHEST)
        ok = row_ok[:, None, :, None] & col_ok[None, :, None, :]
        tables.append(jnp.where(ok[None], bias, MASKED).reshape(n_heads, nq, nk))
    tables.append(jnp.full((n_heads, nq, nk), MASKED, F32))
    return jnp.stack(tables).reshape(4, n_heads // 2, 2, nq, nk)


def _na_kernel(q_ref, k_ref, v_ref, bias_ref, o_ref, *, n_ctx, rows):
    j = pl.program_id(2)
    nk = NA_KROWS * GRID_W
    ks_row = jnp.clip((j - 1) * NA_QROWS - NA_WIN_ROWS // 2, 0, rows - NA_KROWS)
    kstart = pl.multiple_of(n_ctx + ks_row * GRID_W, GRID_W)
    q = q_ref[0]
    kl = k_ref[0, pl.ds(kstart, nk), :]
    vl = v_ref[0, pl.ds(kstart, nk), :]
    kc = k_ref[0, 0:n_ctx, :]
    vc = v_ref[0, 0:n_ctx, :]
    lane = lax.broadcasted_iota(jnp.int32, q.shape, 1)
    outs = []
    for h in range(2):
        in_head = (lane >= h * HEAD_DIM) & (lane < (h + 1) * HEAD_DIM)
        qh = jnp.where(in_head, q, jnp.zeros_like(q))
        s_loc = _dot_nt(qh, kl) + bias_ref[0, 0, h]
        s_ctx = _dot_nt(qh, kc)
        m = jnp.maximum(jnp.max(s_loc, axis=-1, keepdims=True), jnp.max(s_ctx, axis=-1, keepdims=True))
        p_loc = jnp.exp(s_loc - m)
        p_ctx = jnp.exp(s_ctx - m)
        den = jnp.sum(p_loc, axis=-1, keepdims=True) + jnp.sum(p_ctx, axis=-1, keepdims=True)
        o = _dot(p_loc.astype(BF16), vl) + _dot(p_ctx.astype(BF16), vc)
        outs.append(o / den)
    o_ref[0] = jnp.where(lane < HEAD_DIM, outs[0], outs[1]).astype(o_ref.dtype)


def _na_attention(p_na, bias, n_ctx, na_w):
    b, l, _ = p_na.shape
    rows = (l - n_ctx) // GRID_W
    nq = NA_QROWS * GRID_W
    assert n_ctx == nq and rows >= NA_KROWS and rows % NA_QROWS == 0
    n_pairs = na_w // LANES
    nblk = l // nq
    kern = functools.partial(_na_kernel, n_ctx=n_ctx, rows=rows)

    def bias_index(i, hp, j):
        pat = jnp.where(j == 0, 3, jnp.where(j == 1, 0, jnp.where(j == nblk - 1, 2, 1)))
        return (pat, hp, 0, 0, 0)

    return pl.pallas_call(
        kern,
        grid=(b, n_pairs, nblk),
        in_specs=[pl.BlockSpec((1, nq, LANES), lambda i, hp, j: (i, j, hp)),
                  pl.BlockSpec((1, l, LANES), lambda i, hp, j: (i, 0, n_pairs + hp)),
                  pl.BlockSpec((1, l, LANES), lambda i, hp, j: (i, 0, 2 * n_pairs + hp)),
                  pl.BlockSpec((1, 1, 2) + bias.shape[3:], bias_index)],
        out_specs=pl.BlockSpec((1, nq, LANES), lambda i, hp, j: (i, j, hp)),
        out_shape=jax.ShapeDtypeStruct((b, l, na_w), BF16),
        compiler_params=_cparams(("arbitrary", "arbitrary", "arbitrary")),
        name="na_attention",
    )(p_na, p_na, p_na, bias)


def _tri_masks(reverse):
    t = lax.broadcasted_iota(jnp.int32, (CHUNK, CHUNK), 0)
    s = lax.broadcasted_iota(jnp.int32, (CHUNK, CHUNK), 1)
    earlier_or_same = (s >= t) if reverse else (s <= t)
    earlier = (s > t) if reverse else (s < t)
    return earlier_or_same, earlier, t == s


def _each(fn, *lists):
    return [fn(*args) for args in zip(*lists)]


def _bf16(values):
    return [u.astype(BF16) for u in values]


def _unit_tri_inverse(a, eye):
    assert CHUNK == 64
    x1 = [-u for u in a]
    x1b = _bf16(x1)
    x2 = _each(_dot, x1b, x1b)
    x2b = _bf16(x2)
    x4 = _each(_dot, x2b, x2b)
    x3 = _each(_dot, x1b, x2b)
    x4b = _bf16(x4)
    x8 = _each(_dot, x4b, x4b)
    p1 = _each(lambda u1, u2, u3: eye + u1 + u2 + u3, x1, x2, x3)
    x8b = _bf16(x8)
    x12 = _each(_dot, x4b, x8b)
    x16 = _each(_dot, x8b, x8b)
    p2 = _each(lambda u4, u8, u12: eye + u4 + u8 + u12, x4, x8, x12)
    p12 = _each(_dot, _bf16(p1), _bf16(p2))
    x16b = _bf16(x16)
    x32 = _each(_dot, x16b, x16b)
    x48 = _each(_dot, x16b, _bf16(x32))
    p3 = _each(lambda u16, u32, u48: eye + u16 + u32 + u48, x16, x32, x48)
    return _each(_dot, _bf16(p12), _bf16(p3))


def _head_lanes(width):
    lane = lax.broadcasted_iota(jnp.int32, (1, width), 1)
    return [(lane >= h * HEAD_DIM) & (lane < (h + 1) * HEAD_DIM) for h in range(width // HEAD_DIM)]


def _stack_heads(x, heads):
    return jnp.concatenate([jnp.where(in_head, x, 0.0) for in_head in heads], axis=0)


def _stacked_masks(n, reverse):
    row = lax.broadcasted_iota(jnp.int32, (n, n), 0)
    col = lax.broadcasted_iota(jnp.int32, (n, n), 1)
    t, s = row % CHUNK, col % CHUNK
    same_head = (row // CHUNK) == (col // CHUNK)
    earlier = ((s > t) if reverse else (s < t)) & same_head
    earlier_or_same = earlier | (row == col)
    return earlier, earlier_or_same, row == col


def _chunk_matrices(a_ka, a_kk, a_ra, a_rk, ks, rs, vs, kds, ads, eye):
    t_inv = _bf16(_unit_tri_inverse(a_ka, eye.astype(F32)))
    vsb, adsb, a_rab = _bf16(vs), _bf16(ads), _bf16(a_ra)
    av = _each(_dot, _bf16(a_kk), vsb)
    w1 = _bf16(_each(_dot, t_inv, _bf16(ks)))
    u0 = _bf16(_each(_dot, t_inv, _bf16(av)))
    rkv = _each(_dot, _bf16(a_rk), vsb)
    w2 = _each(lambda r, a, w: r - _dot(a, w), rs, a_rab, w1)
    y1 = _each(lambda y, a, u: y - _dot(a, u), rkv, a_rab, u0)
    mm = _each(_dot_tn, adsb, w1)
    nn = _each(lambda kd, v, ad, u: _dot_tn(kd, v) - _dot_tn(ad, u), _bf16(kds), vsb, adsb, u0)
    return list(zip(w2, y1, mm, nn))


def _store_chunk(refs, d, c0, n_heads, w2, y1, mm, nn, p_end):
    w2_ref, y1_ref, m_ref, n_ref = refs
    rows = pl.ds(c0, CHUNK)
    eye = (lax.broadcasted_iota(jnp.int32, (HEAD_DIM, HEAD_DIM), 0)
           == lax.broadcasted_iota(jnp.int32, (HEAD_DIM, HEAD_DIM), 1))
    for h in range(n_heads):
        lo, hi = h * HEAD_DIM, (h + 1) * HEAD_DIM
        r_lo, r_hi = h * CHUNK, (h + 1) * CHUNK
        w2_ref[d, 0, h, rows, :] = w2[r_lo:r_hi, lo:hi]
        y1_ref[d, 0, h, rows, :] = y1[r_lo:r_hi, lo:hi]
        m_ref[d, 0, h, rows, :] = jnp.where(eye, p_end[:, lo:hi], 0.0) - mm[lo:hi, lo:hi]
        n_ref[d, 0, h, rows, :] = nn[lo:hi, lo:hi]


def _scan_out_specs(b, n_heads, l):
    shape = jax.ShapeDtypeStruct((2, b, n_heads, l, HEAD_DIM), F32)
    spec = pl.BlockSpec((2, 1, n_heads, TM, HEAD_DIM), lambda i, t: (0, i, 0, t, 0))
    return [shape] * 4, [spec] * 4


def _seq_edges(t, n_ctx_tiles, n_tiles):
    first = (t == 0) | (t == n_ctx_tiles)
    last = (t == n_ctx_tiles - 1) | (t == n_tiles - 1)
    return first, last


def _fill_halo(ext_ref, x_ref, prev_ref, next_ref, first, last):
    ext_ref[0:8, :] = jnp.where(first, 0.0, prev_ref[0])
    ext_ref[8:8 + TM, :] = x_ref[0]
    ext_ref[8 + TM:16 + TM, :] = jnp.where(last, 0.0, next_ref[0])


def _halo_specs(width, n_tiles):
    per = TM // 8
    return [pl.BlockSpec((1, TM, width), lambda i, t: (i, t, 0)),
            pl.BlockSpec((1, 8, width), lambda i, t: (i, jnp.maximum(t * per - 1, 0), 0)),
            pl.BlockSpec((1, 8, width), lambda i, t: (i, jnp.minimum((t + 1) * per, n_tiles * per - 1), 0))]


def _full_spec(a):
    return pl.BlockSpec(a.shape, lambda i, t: (0,) * a.ndim)


def _rwkv_prep_kernel(x_ref, prev_ref, next_ref, mu_ref, w0_ref, wup_ref, a0_ref, aup_ref, gup_ref,
                      kk_ref, ka_ref, rk_ref, ones_ref,
                      w2_ref, y1_ref, m_ref, n_ref, bv_ref, gate_ref,
                      ext_ref, tok_ref, *, n_ctx_tiles, n_tiles, width):
    t = pl.program_id(1)
    first, last = _seq_edges(t, n_ctx_tiles, n_tiles)
    _fill_halo(ext_ref, x_ref, prev_ref, next_ref, first, last)
    p = ext_ref[8:8 + TM, :]
    prev = ext_ref[7:7 + TM, :]
    nxt = ext_ref[9:9 + TM, :]
    ps = p + mu_ref[...] * (0.5 * (prev + nxt) - p)
    r = ps[:, 0:width]
    k = ps[:, width:2 * width]
    v = ps[:, 2 * width:3 * width]
    lo = ps[:, 3 * width:3 * width + LANES]
    ones_bd = ones_ref[...]
    gate_ref[0] = _dot(jax.nn.sigmoid(lo), gup_ref[...])
    kq = k * kk_ref[...]
    kk = kq * lax.rsqrt(_dot_split(kq * kq, ones_bd) + NORM_EPS)
    tanh_lo = jnp.tanh(lo)
    tok_ref[0] = r
    tok_ref[1] = kk
    tok_ref[2] = v
    k_sum = jnp.zeros_like(k)
    for d in range(2):
        z = w0_ref[d:d + 1, :] + _dot(tanh_lo, wup_ref[d])
        w_log = -jax.nn.softplus(-z) - 0.5
        a = jax.nn.sigmoid(a0_ref[d:d + 1, :] + _dot(lo, aup_ref[d]))
        k_dir = k * (1.0 + (a - 1.0) * ka_ref[...])
        k_sum = k_sum + k_dir
        tok_ref[3 + 3 * d] = -jnp.exp(w_log)
        tok_ref[4 + 3 * d] = kk * a
        tok_ref[5 + 3 * d] = k_dir
    bonus = _dot_split(r * k_sum * rk_ref[...], ones_bd)
    bv_ref[0] = bonus * v

    heads = _head_lanes(width)
    out_refs = (w2_ref, y1_ref, m_ref, n_ref)

    def chunk_body(c, carry):
        args = [[] for _ in range(9)]
        where = []
        for cc in range(CHUNKS_PER_STEP):
            c0 = pl.multiple_of((c * CHUNKS_PER_STEP + cc) * CHUNK, CHUNK)
            rows = pl.ds(c0, CHUNK)
            r_c, kk_c, v_c = tok_ref[0, rows, :], tok_ref[1, rows, :], tok_ref[2, rows, :]
            vs = _stack_heads(v_c, heads)
            for d in range(2):
                logw, alpha, k_c = (tok_ref[3 + 3 * d, rows, :], tok_ref[4 + 3 * d, rows, :],
                                    tok_ref[5 + 3 * d, rows, :])
                incl, _, _ = _tri_masks(reverse=(d == 1))
                ci = _dot(incl.astype(F32), logw, precision=HIGHEST)
                e_ci, e_nci, e_ce = jnp.exp(ci), jnp.exp(-ci), jnp.exp(ci - logw)
                p_end = e_ci[0:1, :] if d == 1 else e_ci[CHUNK - 1:CHUNK, :]
                ks, rs, khs, ahs = (_stack_heads(u, heads)
                                    for u in (kk_c * e_ce, r_c * e_ci, k_c * e_nci, alpha * e_nci))
                earlier, earlier_or_same, eye = _stacked_masks(ks.shape[0], reverse=(d == 1))
                ksb, rsb, khsb, ahsb = _bf16((ks, rs, khs, ahs))
                a_ka = jnp.where(earlier, _dot_nt(ksb, ahsb), 0.0)
                a_kk = jnp.where(earlier, _dot_nt(ksb, khsb), 0.0)
                a_ra = jnp.where(earlier_or_same, _dot_nt(rsb, ahsb), 0.0)
                a_rk = jnp.where(earlier_or_same, _dot_nt(rsb, khsb), 0.0)
                for slot, value in zip(args, (a_ka, a_kk, a_ra, a_rk, ks, rs, vs, khs * p_end, ahs * p_end)):
                    slot.append(value)
                where.append((d, c0, p_end))
        for (d, c0, p_end), res in zip(where, _chunk_matrices(*args, eye)):
            _store_chunk(out_refs, d, c0, len(heads), *res, p_end)
        return carry

    lax.fori_loop(0, TM // CHUNK // CHUNKS_PER_STEP, chunk_body, 0)


def _rwkv_prepare(p_rw, pr, n_ctx):
    b, l, pw = p_rw.shape
    width = pr['rw_k_k'].shape[-1]
    n_heads = width // HEAD_DIM
    nt = l // TM
    kern = functools.partial(_rwkv_prep_kernel, n_ctx_tiles=n_ctx // TM, n_tiles=nt, width=width)
    params = [pr['rw_mu_p'], pr['rw_w0'], pr['rw_wup_p'], pr['rw_a0'], pr['rw_aup_p'], pr['rw_gup_p'],
              pr['rw_k_k'].reshape(1, width), pr['rw_k_a'].reshape(1, width), pr['rw_r_k'].reshape(1, width),
              pr['ones_bd']]
    scan_shapes, scan_specs = _scan_out_specs(b, n_heads, l)
    tok_spec = pl.BlockSpec((1, TM, width), lambda i, t: (i, t, 0))
    return pl.pallas_call(
        kern,
        grid=(b, nt),
        in_specs=_halo_specs(pw, nt) + [_full_spec(a) for a in params],
        out_specs=scan_specs + [tok_spec, tok_spec],
        out_shape=scan_shapes + [jax.ShapeDtypeStruct((b, l, width), F32)] * 2,
        scratch_shapes=[pltpu.VMEM((TM + 16, pw), F32), pltpu.VMEM((9, TM, width), F32)],
        compiler_params=_cparams(("arbitrary", "arbitrary")),
        name="rwkv_prepare",
    )(p_rw, p_rw, p_rw, *params)


def _scan_kernel(w2f_ref, y1f_ref, mf_ref, nf_ref, w2b_ref, y1b_ref, mb_ref, nb_ref, yf_ref, yb_ref, h_ref):
    @pl.when(pl.program_id(0) == 0)
    def _():
        h_ref[...] = jnp.zeros_like(h_ref)

    n_batch, n_heads = h_ref.shape[1], h_ref.shape[2]
    n_sub = yf_ref.shape[2] // CHUNK
    dirs = ((w2f_ref, y1f_ref, mf_ref, nf_ref, yf_ref), (w2b_ref, y1b_ref, mb_ref, nb_ref, yb_ref))
    chains = [(d, i, h) for d in range(2) for i in range(n_batch) for h in range(n_heads)]
    states = [h_ref[d, i, h] for d, i, h in chains]
    for j in range(n_sub):
        for c, (d, i, h) in enumerate(chains):
            w2_ref, y1_ref, m_ref, n_ref, y_ref = dirs[d]
            jj = j if d == 0 else n_sub - 1 - j
            rows = slice(jj * CHUNK, (jj + 1) * CHUNK)
            y_ref[i, h, rows, :] = _dot(w2_ref[0, i, h, rows, :], states[c]) + y1_ref[0, i, h, rows, :]
            states[c] = _dot(m_ref[0, i, h, rows, :], states[c], precision=HIGHEST) + n_ref[0, i, h, rows, :]
    for c, (d, i, h) in enumerate(chains):
        h_ref[d, i, h] = states[c]


def _scan(w2, y1, m, n, n_ctx):
    _, b, n_heads, l, _ = w2.shape
    nc = l // TM
    ncc = n_ctx // TM

    def bwd_chunk(i):
        return jnp.where(i < ncc, ncc - 1 - i, nc - 1 + ncc - i)

    blk = (1, b, n_heads, TM, HEAD_DIM)
    fwd = pl.BlockSpec(blk, lambda i: (0, 0, 0, i, 0))
    bwd = pl.BlockSpec(blk, lambda i: (1, 0, 0, bwd_chunk(i), 0))
    out_shape = jax.ShapeDtypeStruct((b, n_heads, l, HEAD_DIM), F32)
    return pl.pallas_call(
        _scan_kernel,
        grid=(nc,),
        in_specs=[fwd] * 4 + [bwd] * 4,
        out_specs=[pl.BlockSpec(blk[1:], lambda i: (0, 0, i, 0)),
                   pl.BlockSpec(blk[1:], lambda i: (0, 0, bwd_chunk(i), 0))],
        out_shape=[out_shape, out_shape],
        scratch_shapes=[pltpu.VMEM((2, b, n_heads, HEAD_DIM, HEAD_DIM), F32)],
        compiler_params=_cparams(("arbitrary",)),
        name="chunk_scan",
    )(w2, y1, m, n, w2, y1, m, n)


def _gdn_prep_kernel(x_ref, prev_ref, next_ref, cw_ref, alog_ref, dtb_ref, ones_ref,
                     w2_ref, y1_ref, m_ref, n_ref,
                     ext_ref, tok_ref, *, n_ctx_tiles, n_tiles, width):
    t = pl.program_id(1)
    first, last = _seq_edges(t, n_ctx_tiles, n_tiles)
    _fill_halo(ext_ref, x_ref, prev_ref, next_ref, first, last)
    half = GD_CONV // 2
    conv = jnp.zeros((TM, 3 * width), F32)
    for j in range(GD_CONV):
        conv = conv + cw_ref[j:j + 1, :] * ext_ref[8 + j - half:8 + j - half + TM, 0:3 * width]
    qkv = conv * jax.nn.sigmoid(conv)
    ones_bd = ones_ref[...]

    def l2n(u):
        return u * lax.rsqrt(_dot_split(u * u, ones_bd) + NORM_EPS)

    q = l2n(qkv[:, 0:width]) * HEAD_DIM ** -0.5
    k = l2n(qkv[:, width:2 * width])
    v = qkv[:, 2 * width:3 * width]
    ab = ext_ref[8:8 + TM, 4 * width:4 * width + LANES]
    g_all = -jnp.exp(alog_ref[...]) * jax.nn.softplus(ab + dtb_ref[...])
    beta_all = jax.nn.sigmoid(ab)
    heads = _head_lanes(width)
    n_heads = len(heads)
    tok_ref[0] = q
    tok_ref[1] = k
    tok_ref[2] = v
    for d in range(2):
        g = jnp.zeros((TM, width), F32)
        beta = jnp.zeros((TM, width), F32)
        for h, in_head in enumerate(heads):
            ja = d * n_heads + h
            jb = 2 * n_heads + ja
            g = jnp.where(in_head, g_all[:, ja:ja + 1], g)
            beta = jnp.where(in_head, beta_all[:, jb:jb + 1], beta)
        tok_ref[3 + 2 * d] = g
        tok_ref[4 + 2 * d] = k * beta
    out_refs = (w2_ref, y1_ref, m_ref, n_ref)

    def chunk_inputs(c0, d):
        rows = pl.ds(c0, CHUNK)
        q_c, k_c, v_c = tok_ref[0, rows, :], tok_ref[1, rows, :], tok_ref[2, rows, :]
        g, kb = tok_ref[3 + 2 * d, rows, :], tok_ref[4 + 2 * d, rows, :]
        incl, _, _ = _tri_masks(reverse=(d == 1))
        ci = _dot(incl.astype(F32), g, precision=HIGHEST)
        ce = ci - g
        ci_end = ci[0:1, :] if d == 1 else ci[CHUNK - 1:CHUNK, :]
        ks, rs, kbs, vs, kds, ads = (_stack_heads(u, heads) for u in (
            k_c * jnp.exp(ce), q_c * jnp.exp(ci), kb, v_c,
            kb * jnp.exp(ci_end - ci), kb * jnp.exp(ci_end - ce)))
        earlier, earlier_or_same, eye = _stacked_masks(ks.shape[0], reverse=(d == 1))
        ci_col = jnp.concatenate([ci[:, h * HEAD_DIM:h * HEAD_DIM + 1] for h in range(n_heads)], axis=0)
        ce_col = jnp.concatenate([ce[:, h * HEAD_DIM:h * HEAD_DIM + 1] for h in range(n_heads)], axis=0)
        ci_row = jnp.sum(jnp.where(eye, ci_col, 0.0), axis=0, keepdims=True)
        ce_row = jnp.sum(jnp.where(eye, ce_col, 0.0), axis=0, keepdims=True)
        kbsb = kbs.astype(BF16)
        kkb = _dot_nt(_stack_heads(k_c, heads).astype(BF16), kbsb)
        qkb = _dot_nt(_stack_heads(q_c, heads).astype(BF16), kbsb)
        a_ka = kkb * jnp.exp(jnp.where(earlier, ce_col - ce_row, MASKED))
        a_kk = kkb * jnp.exp(jnp.where(earlier, ce_col - ci_row, MASKED))
        a_ra = qkb * jnp.exp(jnp.where(earlier_or_same, ci_col - ce_row, MASKED))
        a_rk = qkb * jnp.exp(jnp.where(earlier_or_same, ci_col - ci_row, MASKED))
        return (a_ka, a_kk, a_ra, a_rk, ks, rs, vs, kds, ads), jnp.exp(ci_end), eye

    def chunk_body(c, carry):
        args = [[] for _ in range(9)]
        where = []
        for cc in range(CHUNKS_PER_STEP):
            c0 = pl.multiple_of((c * CHUNKS_PER_STEP + cc) * CHUNK, CHUNK)
            for d in range(2):
                values, p_end, eye = chunk_inputs(c0, d)
                for slot, value in zip(args, values):
                    slot.append(value)
                where.append((d, c0, p_end))
        for (d, c0, p_end), res in zip(where, _chunk_matrices(*args, eye)):
            _store_chunk(out_refs, d, c0, n_heads, *res, p_end)
        return carry

    lax.fori_loop(0, TM // CHUNK // CHUNKS_PER_STEP, chunk_body, 0)


def _gdn_prepare(p_gd, pr, n_ctx):
    b, l, pw = p_gd.shape
    width = pr['gd_width']
    n_heads = width // HEAD_DIM
    nt = l // TM
    kern = functools.partial(_gdn_prep_kernel, n_ctx_tiles=n_ctx // TM, n_tiles=nt, width=width)
    params = [pr['gd_conv_w'], pr['gd_alog_p'], pr['gd_dtb_p'], pr['ones_bd']]
    scan_shapes, scan_specs = _scan_out_specs(b, n_heads, l)
    return pl.pallas_call(
        kern,
        grid=(b, nt),
        in_specs=_halo_specs(pw, nt) + [_full_spec(a) for a in params],
        out_specs=scan_specs,
        out_shape=scan_shapes,
        scratch_shapes=[pltpu.VMEM((TM + 16, pw), F32), pltpu.VMEM((7, TM, width), F32)],
        compiler_params=_cparams(("arbitrary", "arbitrary")),
        name="gdn_prepare",
    )(p_gd, p_gd, p_gd, *params)


def _outproj_kernel(x_ref, mod_ref, na_ref, ryf_ref, ryb_ref, bv_ref, gate_ref, gyf_ref, gyb_ref, z_ref,
                    lnw_ref, lnb_ref, gng_ref, wo_ref, g2_ref, rwh_ref, rwl_ref, rb_ref,
                    x1_ref, h2_ref, route_ref, meta_ref, *, na_w, rw_w):
    n_heads = ryf_ref.shape[1]
    rw_parts, gd_parts = [], []
    for h in range(n_heads):
        y = ryf_ref[0, h] + ryb_ref[0, h]
        yc = y - jnp.mean(y, axis=-1, keepdims=True)
        rw_parts.append(yc * lax.rsqrt(jnp.mean(yc * yc, axis=-1, keepdims=True) + RW_GN_EPS))
        gd_parts.append(_rms(gyf_ref[0, h] + gyb_ref[0, h]) * gng_ref[...])
    yn = jnp.concatenate(rw_parts, axis=-1)
    o_rw = (yn * lnw_ref[...] + lnb_ref[...] + bv_ref[0]) * gate_ref[0]
    z = z_ref[0]
    o_gd = jnp.concatenate(gd_parts, axis=-1) * (z * jax.nn.sigmoid(z))
    o = (_dot(na_ref[0], wo_ref[0:na_w, :])
         + _dot(o_rw.astype(BF16), wo_ref[na_w:na_w + rw_w, :])
         + _dot(o_gd.astype(BF16), wo_ref[na_w + rw_w:, :]))
    mod = mod_ref[0]
    x1 = x_ref[0] + mod[2:3] * o
    x1_ref[0] = x1
    h2 = _rms(x1) * g2_ref[...] * (1.0 + mod[4:5]) + mod[3:4]
    h2_ref[0] = h2
    h2_hi = h2.astype(BF16)
    h2_lo = (h2 - h2_hi.astype(F32)).astype(BF16)
    logits = (_dot(h2_hi, rwh_ref[...]) + (_dot(h2_hi, rwl_ref[...]) + _dot(h2_lo, rwh_ref[...]))) + rb_ref[...]
    lane = lax.broadcasted_iota(jnp.int32, logits.shape, 1)
    chosen, weights = [], []
    onehot = jnp.zeros(logits.shape, F32)
    for kk in range(TOP_K):
        m = jnp.max(logits, axis=-1, keepdims=True)
        idx = jnp.min(jnp.where(logits == m, lane, LANES), axis=-1, keepdims=True)
        chosen.append(idx)
        top_max = m if kk == 0 else top_max
        weights.append(jnp.exp(m - top_max))
        onehot = jnp.where(lane == idx, 1.0, onehot)
        logits = jnp.where(lane == idx, -jnp.inf, logits)
    denom = weights[0] + weights[1] + weights[2] + weights[3]
    tok_i = lax.broadcasted_iota(jnp.int32, (TM, TM), 0)
    tok_j = lax.broadcasted_iota(jnp.int32, (TM, TM), 1)
    before = jnp.where(tok_j < tok_i, 1.0, 0.0).astype(BF16)
    rank = _dot(before, onehot.astype(BF16))
    counts = jnp.sum(onehot, axis=0, keepdims=True)
    count8 = jnp.floor((counts + 7.0) * 0.125) * 8.0
    exp_i = lax.broadcasted_iota(jnp.int32, (LANES, LANES), 0)
    exp_j = lax.broadcasted_iota(jnp.int32, (LANES, LANES), 1)
    seg_start = _dot(jnp.broadcast_to(count8, (8, LANES)), jnp.where(exp_i < exp_j, 1.0, 0.0),
                     precision=HIGHEST)[0:1]
    slot_of = seg_start + rank
    route = jnp.zeros(logits.shape, F32)
    for kk in range(TOP_K):
        slot = jnp.sum(jnp.where(lane == chosen[kk], slot_of, 0.0), axis=-1, keepdims=True)
        route = jnp.where(lane == kk, weights[kk] / denom, route)
        route = jnp.where(lane == TOP_K + kk, slot, route)
    route_ref[0] = route
    row8 = lax.broadcasted_iota(jnp.int32, (8, LANES), 0)
    meta_ref[0, 0] = jnp.where(row8 == 0, count8, jnp.where(row8 == 1, seg_start, 0.0))


def _outproj(xa, mods, o_na, ry, bv, gate, gy, p_gd, pr, n_ctx):
    b, l, d = xa.shape
    nt = l // TM
    na_w = o_na.shape[-1]
    n_heads = ry[0].shape[1]
    rw_w = n_heads * HEAD_DIM
    kern = functools.partial(_outproj_kernel, na_w=na_w, rw_w=rw_w)
    tok = lambda w: pl.BlockSpec((1, TM, w), lambda i, t: (i, t, 0))
    head_major = pl.BlockSpec((1, n_heads, TM, HEAD_DIM), lambda i, t: (i, 0, t, 0))
    params = [pr['rw_ln_w'].reshape(1, rw_w), pr['rw_ln_b'].reshape(1, rw_w), pr['gd_norm_g'].reshape(1, HEAD_DIM),
              pr['w_out_bf'], pr['norm_ffn_g'], pr['router_w_hi'], pr['router_w_lo'], pr['router_b_p']]
    return pl.pallas_call(
        kern,
        grid=(b, nt),
        in_specs=[tok(d), pl.BlockSpec((1, 6, d), _mod_index(n_ctx // TM, b)), tok(na_w),
                  head_major, head_major, tok(rw_w), tok(rw_w), head_major, head_major,
                  pl.BlockSpec((1, TM, rw_w), lambda i, t: (i, t, 3))]
                 + [_full_spec(a) for a in params],
        out_specs=[tok(d), tok(d), tok(LANES), pl.BlockSpec((1, 1, 8, LANES), lambda i, t: (i, t, 0, 0))],
        out_shape=[jax.ShapeDtypeStruct((b, l, d), F32), jax.ShapeDtypeStruct((b, l, d), F32),
                   jax.ShapeDtypeStruct((b, l, LANES), F32), jax.ShapeDtypeStruct((b, nt, 8, LANES), F32)],
        compiler_params=_cparams(("arbitrary", "arbitrary")),
        name="outproj_router",
    )(xa, mods, o_na, ry[0], ry[1], bv, gate, gy[0], gy[1], p_gd, *params)


MOE_BM = 256
MOE_EXPERT_VMEM = 56 * 1024 * 1024
SEG_ALIGN = 8
SEG_BITS = (32, 16, 8, 4, 2, 1)


def _n_slots(n_exp):
    return _round_up(TOP_K * TM + n_exp * (SEG_ALIGN - 1), LANES)


def _segment_dma(segs_ref, buf_ref, hbm_ref, sem, n_exp, to_hbm, wait):
    def body(e, carry):
        start = segs_ref[0, 0, e]
        units = segs_ref[0, 0, n_exp + e]
        offset = segs_ref[0, 0, 2 * n_exp + e]
        for bit in SEG_BITS:
            done = (units & ~(2 * bit - 1)) * SEG_ALIGN
            rows = bit * SEG_ALIGN

            @pl.when((units & bit) != 0)
            def _():
                in_buf = buf_ref.at[pl.ds(pl.multiple_of(start + done, SEG_ALIGN), rows), :]
                in_hbm = hbm_ref.at[pl.ds(pl.multiple_of(offset + done, SEG_ALIGN), rows), :]
                copy = (pltpu.make_async_copy(in_buf, in_hbm, sem) if to_hbm
                        else pltpu.make_async_copy(in_hbm, in_buf, sem))
                if wait:
                    copy.wait()
                else:
                    copy.start()
        return carry
    lax.fori_loop(0, n_exp, body, 0)


def _slot_lanes(route, n_slots):
    lane = lax.broadcasted_iota(jnp.int32, (route.shape[0], n_slots), 1)
    slots = [route[:, TOP_K + k:TOP_K + k + 1].astype(jnp.int32) for k in range(TOP_K)]
    return lane, slots


def _dispatch_kernel(segs_ref, segs_prev_ref, gaps_ref, h_ref, route_ref, xb_ref, xs_ref, zero_ref, sem, gap_sem,
                     *, n_exp):
    i = pl.program_id(0)
    slot = i % 2

    @pl.when(i == 0)
    def _():
        zero_ref[...] = jnp.zeros_like(zero_ref)
        _segment_dma(gaps_ref, zero_ref, xb_ref, gap_sem, n_exp, to_hbm=True, wait=False)
        _segment_dma(gaps_ref, zero_ref, xb_ref, gap_sem, n_exp, to_hbm=True, wait=True)
        n_blocks = xb_ref.shape[0] // MOE_BM

        def block_copy(blk):
            return pltpu.make_async_copy(zero_ref, xb_ref.at[pl.ds(pl.multiple_of(blk * MOE_BM, MOE_BM), MOE_BM), :],
                                         gap_sem)

        def start(blk, carry):
            block_copy(blk).start()
            return carry

        def wait(blk, carry):
            block_copy(blk).wait()
            return carry

        lax.fori_loop(gaps_ref[0, 0, 3 * n_exp], n_blocks, start, 0)
        lax.fori_loop(gaps_ref[0, 0, 3 * n_exp], n_blocks, wait, 0)

    lane, slots = _slot_lanes(route_ref[...], xs_ref.shape[1])
    picked = lane == slots[0]
    for k in range(1, TOP_K):
        picked = picked | (lane == slots[k])
    onehot = jnp.where(picked, 1.0, 0.0).astype(BF16)
    xs_ref[slot] = _dot_tn(onehot, h_ref[...].astype(BF16))
    _segment_dma(segs_ref, xs_ref.at[slot], xb_ref, sem.at[slot], n_exp, to_hbm=True, wait=False)

    @pl.when(i > 0)
    def _():
        _segment_dma(segs_prev_ref, xs_ref.at[1 - slot], xb_ref, sem.at[1 - slot], n_exp, to_hbm=True, wait=True)

    @pl.when(i == pl.num_programs(0) - 1)
    def _():
        _segment_dma(segs_ref, xs_ref.at[slot], xb_ref, sem.at[slot], n_exp, to_hbm=True, wait=True)


def _seg_spec(index):
    return pl.BlockSpec((1, 1, LANES), index, memory_space=pltpu.SMEM)


def _dispatch(h2, route, segs, gaps, n_pad, n_exp):
    t, d = h2.shape
    nt = t // TM
    kern = functools.partial(_dispatch_kernel, n_exp=n_exp)
    return pl.pallas_call(
        kern,
        grid=(nt,),
        in_specs=[_seg_spec(lambda i: (i, 0, 0)),
                  _seg_spec(lambda i: (jnp.maximum(i - 1, 0), 0, 0)),
                  _seg_spec(lambda i: (0, 0, 0)),
                  pl.BlockSpec((TM, d), lambda i: (i, 0)),
                  pl.BlockSpec((TM, LANES), lambda i: (i, 0))],
        out_specs=pl.BlockSpec(memory_space=pl.ANY),
        out_shape=jax.ShapeDtypeStruct((n_pad, d), F32),
        scratch_shapes=[pltpu.VMEM((2, _n_slots(n_exp), d), F32), pltpu.VMEM((MOE_BM, d), F32),
                        pltpu.SemaphoreType.DMA((2,)), pltpu.SemaphoreType.DMA(())],
        compiler_params=_cparams(("arbitrary",)),
        name="moe_dispatch",
    )(segs, segs, gaps, h2, route)


def _expert_kernel(blk_e_ref, n_used_ref, x_ref, wgu_ref, bgu_ref, wd_ref, bd_ref, y_ref, wgu_bf, wd_bf):
    i = pl.program_id(0)
    used = i < n_used_ref[0]
    new_expert = (i == 0) | (blk_e_ref[i] != blk_e_ref[jnp.maximum(i - 1, 0)])

    @pl.when(used & new_expert)
    def _():
        wgu_bf[...] = wgu_ref[0].astype(BF16)
        wd_bf[...] = wd_ref[0].astype(BF16)

    @pl.when(used)
    def _():
        x = x_ref[...].astype(BF16)
        d_exp = wd_bf.shape[0]
        gu = _dot(x, wgu_bf[...]) + bgu_ref[0]
        gate = jnp.minimum(gu[:, :d_exp], SWIGLU_LIMIT)
        lin = jnp.clip(gu[:, d_exp:], -SWIGLU_LIMIT, SWIGLU_LIMIT)
        act = gate * jax.nn.sigmoid(SWIGLU_ALPHA * gate) * (lin + 1.0)
        y_ref[...] = _dot(act.astype(BF16), wd_bf[...]) + bd_ref[0]

    @pl.when(jnp.logical_not(used))
    def _():
        y_ref[...] = jnp.zeros_like(y_ref)


def _expert_blocks(xb, blk_e, n_used, w_gu, b_gu, w_down, b_down):
    n_pad, d = xb.shape
    n_blocks = n_pad // MOE_BM
    n_exp, _, d_gu = w_gu.shape
    grid_spec = pltpu.PrefetchScalarGridSpec(
        num_scalar_prefetch=2,
        grid=(n_blocks,),
        in_specs=[pl.BlockSpec((MOE_BM, d), lambda i, be, nu: (jnp.minimum(i, nu[0] - 1), 0)),
                  pl.BlockSpec((1, d, d_gu), lambda i, be, nu: (be[i], 0, 0)),
                  pl.BlockSpec((1, 1, d_gu), lambda i, be, nu: (be[i], 0, 0)),
                  pl.BlockSpec((1, d_gu // 2, d), lambda i, be, nu: (be[i], 0, 0)),
                  pl.BlockSpec((1, 1, d), lambda i, be, nu: (be[i], 0, 0))],
        out_specs=pl.BlockSpec((MOE_BM, d), lambda i, be, nu: (i, 0)),
        scratch_shapes=[pltpu.VMEM((d, d_gu), BF16), pltpu.VMEM((d_gu // 2, d), BF16)])
    return pl.pallas_call(
        _expert_kernel,
        grid_spec=grid_spec,
        out_shape=jax.ShapeDtypeStruct((n_pad, d), F32),
        compiler_params=pltpu.CompilerParams(dimension_semantics=("arbitrary",),
                                             vmem_limit_bytes=MOE_EXPERT_VMEM),
        name="moe_experts",
    )(blk_e, n_used, xb, w_gu, b_gu.reshape(n_exp, 1, d_gu), w_down, b_down.reshape(n_exp, 1, d))


def _combine_kernel(segs_ref, segs_next_ref, y_hbm, x_ref, mod_ref, route_ref, o_ref, ybuf, sem, *, n_exp):
    i = pl.program_id(0)
    slot = i % 2

    @pl.when(i == 0)
    def _():
        ybuf[...] = jnp.zeros_like(ybuf)
        _segment_dma(segs_ref, ybuf.at[0], y_hbm, sem.at[0], n_exp, to_hbm=False, wait=False)

    @pl.when(i + 1 < pl.num_programs(0))
    def _():
        _segment_dma(segs_next_ref, ybuf.at[1 - slot], y_hbm, sem.at[1 - slot], n_exp, to_hbm=False, wait=False)

    _segment_dma(segs_ref, ybuf.at[slot], y_hbm, sem.at[slot], n_exp, to_hbm=False, wait=True)
    y = ybuf[slot].astype(BF16)
    route = route_ref[...]
    lane, slots = _slot_lanes(route, ybuf.shape[1])
    weights = jnp.zeros(lane.shape, F32)
    for k in range(TOP_K):
        weights = jnp.where(lane == slots[k], route[:, k:k + 1], weights)
    o_ref[...] = x_ref[...] + mod_ref[0, 5:6, :] * _dot(weights.astype(BF16), y)


def _combine(yb, segs, x1, mods, route, n_batch, l, n_ctx, n_exp):
    t, d = x1.shape
    nt = t // TM
    per_sample = l // TM
    n_ctx_tiles = n_ctx // TM
    kern = functools.partial(_combine_kernel, n_exp=n_exp)

    def mod_index(i):
        return (jnp.where(i % per_sample < n_ctx_tiles, n_batch, i // per_sample), 0, 0)

    return pl.pallas_call(
        kern,
        grid=(nt,),
        in_specs=[_seg_spec(lambda i: (i, 0, 0)),
                  _seg_spec(lambda i: (jnp.minimum(i + 1, nt - 1), 0, 0)),
                  pl.BlockSpec(memory_space=pl.ANY),
                  pl.BlockSpec((TM, d), lambda i: (i, 0)),
                  pl.BlockSpec((1, 6, d), mod_index),
                  pl.BlockSpec((TM, LANES), lambda i: (i, 0))],
        out_specs=pl.BlockSpec((TM, d), lambda i: (i, 0)),
        out_shape=jax.ShapeDtypeStruct((t, d), F32),
        scratch_shapes=[pltpu.VMEM((2, _n_slots(n_exp), d), F32), pltpu.SemaphoreType.DMA((2,))],
        compiler_params=_cparams(("arbitrary",)),
        name="moe_combine",
    )(segs, segs, yb, x1, mods, route)


def _seg_table(start, units, offset, extra=None):
    n = start.shape[0]
    cols = [start, units, offset] + ([] if extra is None else [extra])
    pad = jnp.zeros((n, LANES - sum(c.shape[1] for c in cols)), jnp.int32)
    return jnp.concatenate(cols + [pad], axis=1).reshape(n, 1, LANES)


def _route(meta, n_exp, n_tokens):
    nt = meta.shape[0]
    count8 = meta[:, 0, :n_exp].astype(jnp.int32)
    seg_start = meta[:, 1, :n_exp].astype(jnp.int32)
    per_expert = jnp.sum(count8, axis=0)
    padded = (per_expert + MOE_BM - 1) // MOE_BM * MOE_BM
    pad_end = jnp.cumsum(padded)
    pad_start = pad_end - padded
    offset = pad_start[None, :] + jnp.cumsum(count8, axis=0) - count8
    n_pad = _round_up(n_tokens * TOP_K + nt * n_exp * (SEG_ALIGN - 1), MOE_BM) + n_exp * MOE_BM
    block_row = jnp.arange(n_pad // MOE_BM, dtype=jnp.int32) * MOE_BM
    blk_e = jnp.minimum(jnp.sum(block_row[:, None] >= pad_end[None, :], axis=1), n_exp - 1).astype(jnp.int32)
    n_used = (pad_end[-1:] // MOE_BM).astype(jnp.int32)
    segs = _seg_table(seg_start, count8 // SEG_ALIGN, offset)
    gaps = _seg_table(jnp.zeros((1, n_exp), jnp.int32), ((padded - per_expert) // SEG_ALIGN)[None, :],
                      (pad_start + per_expert)[None, :], n_used[None, :])
    return segs, gaps, blk_e, n_used, n_pad


def _channel_sublayer(x1, h2, route, meta, mods, pr, n_ctx):
    b, l, d = x1.shape
    t = b * l
    n_exp = pr['moe_w_gu'].shape[0]
    assert 3 * n_exp < LANES and MOE_BM // SEG_ALIGN <= 2 * SEG_BITS[0]
    route = route.reshape(t, LANES)
    segs, gaps, blk_e, n_used, n_pad = _route(meta.reshape(t // TM, 8, LANES), n_exp, t)
    xb = _dispatch(h2.reshape(t, d), route, segs, gaps, n_pad, n_exp)
    yb = _expert_blocks(xb, blk_e, n_used, pr['moe_w_gu'], pr['moe_b_gu'], pr['moe_w_down'], pr['moe_b_down'])
    x2 = _combine(yb, segs, x1.reshape(t, d), mods, route, b, l, n_ctx, n_exp)
    return x2.reshape(b, l, d)


def _pad_cols(w, n):
    return jnp.pad(w, ((0, 0), (0, n - w.shape[1])))


def _round_up(n, m):
    return -(-n // m) * m


def _prep_layer_params(pr):
    d = pr['w_in'].shape[0]
    na_w = d // 2
    rw_w = d // 4
    gd_w = d - na_w - rw_w
    na_cols = 3 * na_w
    rw_cols = 3 * rw_w + RW_DECAY_LORA + RW_ICLR_LORA + RW_GATE_LORA
    rw_pad = _round_up(rw_cols, LANES)
    gd_cols = pr['w_in'].shape[1] - na_cols - rw_cols
    gd_pad = _round_up(gd_cols, LANES)
    w_in = pr['w_in']
    out = dict(pr)
    out['w_in'] = jnp.concatenate([w_in[:, :na_cols],
                                   _pad_cols(w_in[:, na_cols:na_cols + rw_cols], rw_pad),
                                   _pad_cols(w_in[:, na_cols + rw_cols:], gd_pad)], axis=1).astype(BF16)
    out['rw_w'] = rw_pad
    out['gd_w'] = gd_pad
    n_heads = na_w // HEAD_DIM
    out['qkg'] = jnp.stack([jnp.tile(pr['na_q_gain'], n_heads) * HEAD_DIM ** -0.5,
                            jnp.tile(pr['na_k_gain'], n_heads)])
    head_of = np.arange(na_w) // HEAD_DIM
    out['avg'] = jnp.asarray((head_of[:, None] == head_of[None, :]) / HEAD_DIM, BF16)
    out['norm_mix_g'] = pr['norm_mix_g'].reshape(1, d)
    out['norm_ffn_g'] = pr['norm_ffn_g'].reshape(1, d)
    out['rw_mu_p'] = jnp.pad(pr['rw_mu'], (0, rw_pad - rw_cols)).reshape(1, rw_pad)
    o1, o2 = RW_DECAY_LORA, RW_DECAY_LORA + RW_ICLR_LORA
    out['rw_wup_p'] = jnp.pad(pr['rw_w_up'], ((0, 0), (0, LANES - o1), (0, 0)))
    out['rw_aup_p'] = jnp.pad(pr['rw_a_up'], ((0, 0), (o1, LANES - o2), (0, 0)))
    out['rw_gup_p'] = jnp.pad(pr['rw_g_up'], ((o2, LANES - o2 - RW_GATE_LORA), (0, 0)))
    head_of = np.arange(rw_w) // HEAD_DIM
    out['ones_bd'] = jnp.asarray(head_of[:, None] == head_of[None, :], BF16)
    out['gd_width'] = gd_w
    n_ab = pr['gd_A_log'].size
    out['gd_alog_p'] = jnp.pad(pr['gd_A_log'].reshape(-1), (0, LANES - n_ab)).reshape(1, LANES)
    out['gd_dtb_p'] = jnp.pad(pr['gd_dt_bias'].reshape(-1), (0, LANES - n_ab)).reshape(1, LANES)
    out['w_out_bf'] = pr['w_out'].astype(BF16)
    n_exp = pr['moe_router_w'].shape[1]
    router_w = _pad_cols(pr['moe_router_w'], LANES)
    out['router_w_hi'] = router_w.astype(BF16)
    out['router_w_lo'] = (router_w - out['router_w_hi'].astype(F32)).astype(BF16)
    out['router_b_p'] = jnp.concatenate([pr['moe_router_b'],
                                         jnp.full((LANES - n_exp,), MASKED, F32)]).reshape(1, LANES)
    return out


def _mixing_sublayer(xa, mods, pr, n_ctx):
    b, l, d = xa.shape
    na_w = d // 2
    p_na, p_rw, p_gd = _inproj(xa, mods, pr['norm_mix_g'], pr['w_in'], pr['qkg'], pr['avg'], n_ctx,
                               na_w, pr['rw_w'], pr['gd_w'])
    o_na = _na_attention(p_na, pr['na_bias'], n_ctx, na_w)
    rw2, ry1, rm, rn, bv, gate = _rwkv_prepare(p_rw, pr, n_ctx)
    ry = _scan(rw2, ry1, rm, rn, n_ctx)
    gy = _scan(*_gdn_prepare(p_gd, pr, n_ctx), n_ctx)
    return _outproj(xa, mods, o_na, ry, bv, gate, gy, p_gd, pr, n_ctx)


_LAYER_PARAMS = ('norm_mix_g', 'norm_ffn_g', 'w_in', 'w_out', 'na_q_gain', 'na_k_gain', 'na_rpb',
                 'rw_mu', 'rw_w0', 'rw_w_up', 'rw_a0', 'rw_a_up', 'rw_g_up', 'rw_k_k', 'rw_k_a', 'rw_r_k',
                 'rw_ln_w', 'rw_ln_b', 'gd_conv_w', 'gd_A_log', 'gd_dt_bias', 'gd_norm_g',
                 'moe_router_w', 'moe_router_b', 'moe_w_gu', 'moe_b_gu', 'moe_w_down', 'moe_b_down')


def kernel(x, c, ctx, c_ctx, ada_w, ada_b, norm_mix_g, norm_ffn_g, w_in, w_out, na_q_gain, na_k_gain, na_rpb, rw_mu, rw_w0, rw_w_up, rw_a0, rw_a_up, rw_g_up, rw_k_k, rw_k_a, rw_r_k, rw_ln_w, rw_ln_b, gd_conv_w, gd_A_log, gd_dt_bias, gd_norm_g, moe_router_w, moe_router_b, moe_w_gu, moe_b_gu, moe_w_down, moe_b_down):
    stacked = dict(zip(_LAYER_PARAMS, (norm_mix_g, norm_ffn_g, w_in, w_out, na_q_gain, na_k_gain, na_rpb,
                                       rw_mu, rw_w0, rw_w_up, rw_a0, rw_a_up, rw_g_up, rw_k_k, rw_k_a, rw_r_k,
                                       rw_ln_w, rw_ln_b, gd_conv_w, gd_A_log, gd_dt_bias, gd_norm_g,
                                       moe_router_w, moe_router_b, moe_w_gu, moe_b_gu, moe_w_down, moe_b_down)))
    b, s, d = x.shape
    n_ctx = ctx.shape[1]
    depth = ada_w.shape[0]
    assert b + 1 <= 8 and n_ctx % TM == 0 and s % TM == 0 and s % GRID_W == 0
    cvec = jnp.zeros((8, d), F32).at[:b].set(c).at[b].set(c_ctx)
    mods_all = _adaln(cvec, ada_w, ada_b).reshape(depth, 8, 6, d)[:, :b + 1]
    xa = jnp.concatenate([ctx, x], axis=1)
    for layer in range(depth):
        pr = _prep_layer_params({name: value[layer] for name, value in stacked.items()})
        pr['na_bias'] = _na_bias_tables(pr['na_rpb'], s // GRID_W)
        mods = mods_all[layer]
        x1, h2, route, meta = _mixing_sublayer(xa, mods, pr, n_ctx)
        xa = _channel_sublayer(x1, h2, route, meta, mods, pr, n_ctx)
    return xa[:, n_ctx:]
```

```python
import functools
import math

import numpy as np
import jax
import jax.numpy as jnp
from jax import lax
from jax.experimental import pallas as pl
from jax.experimental.pallas import tpu as pltpu

F32 = jnp.float32
BF16 = jnp.bfloat16
HIGHEST = lax.Precision.HIGHEST

GRID_W = 64
NORM_EPS = 1e-6
HEAD_DIM = 64
NA_WIN_ROWS = 8
NA_WIN_COLS = 16
RW_DECAY_LORA = 32
RW_ICLR_LORA = 32
RW_GATE_LORA = 64
RW_GN_EPS = 64e-5
GD_CONV = 5
TOP_K = 4
SWIGLU_LIMIT = 7.0
SWIGLU_ALPHA = 1.702

LANES = 128
TM = 256
NA_QROWS = 4
NA_KROWS = NA_QROWS + NA_WIN_ROWS
CHUNK = 64
CHUNKS_PER_STEP = 2
MASKED = -1e30
VMEM_LIMIT = 48 * 1024 * 1024


def _cparams(sem):
    return pltpu.CompilerParams(dimension_semantics=sem, vmem_limit_bytes=VMEM_LIMIT)


def _dot(a, b, **kw):
    return jnp.dot(a, b, preferred_element_type=F32, **kw)


def _dot_split(a, b_bf16):
    hi = a.astype(BF16)
    lo = (a - hi.astype(F32)).astype(BF16)
    return _dot(hi, b_bf16) + _dot(lo, b_bf16)


def _dot_nt(a, b, **kw):
    return lax.dot_general(a, b, (((1,), (1,)), ((), ())), preferred_element_type=F32, **kw)


def _dot_tn(a, b, **kw):
    return lax.dot_general(a, b, (((0,), (0,)), ((), ())), preferred_element_type=F32, **kw)


def _adaln_kernel(c_ref, w_ref, b_ref, o_ref):
    c = c_ref[...]
    s = c * jax.nn.sigmoid(c)
    o_ref[0] = _dot(s.astype(BF16), w_ref[0].astype(BF16)) + b_ref[0]


def _adaln(cvec, ada_w, ada_b):
    depth, d, n = ada_w.shape
    tn = n // 4
    return pl.pallas_call(
        _adaln_kernel,
        grid=(depth, n // tn),
        in_specs=[pl.BlockSpec((8, d), lambda l, j: (0, 0)),
                  pl.BlockSpec((1, d, tn), lambda l, j: (l, 0, j)),
                  pl.BlockSpec((1, 1, tn), lambda l, j: (l, 0, j))],
        out_specs=pl.BlockSpec((1, 8, tn), lambda l, j: (l, 0, j)),
        out_shape=jax.ShapeDtypeStruct((depth, 8, n), F32),
        compiler_params=_cparams(("arbitrary", "arbitrary")),
        name="adaln",
    )(cvec, ada_w, ada_b.reshape(depth, 1, n))


def _mod_index(n_ctx_tiles, n_batch):
    return lambda b, t: (jnp.where(t < n_ctx_tiles, n_batch, b), 0, 0)


def _rms(x):
    return x * lax.rsqrt(jnp.mean(x * x, axis=-1, keepdims=True) + NORM_EPS)


def _inproj_kernel(x_ref, mod_ref, g_ref, w_ref, qkg_ref, avg_ref, na_ref, rw_ref, gd_ref, *, na_w, rw_w):
    mod = mod_ref[0]
    h = _rms(x_ref[0]) * g_ref[...]
    hb = (h * (1.0 + mod[1:2]) + mod[0:1]).astype(BF16)
    pa = _dot(hb, w_ref[:, 0:3 * na_w])
    avg = avg_ref[...]
    qkg = qkg_ref[...]
    q = pa[:, 0:na_w]
    k = pa[:, na_w:2 * na_w]
    qn = q * lax.rsqrt(_dot(q * q, avg) + NORM_EPS) * qkg[0:1]
    kn = k * lax.rsqrt(_dot(k * k, avg) + NORM_EPS) * qkg[1:2]
    na_ref[0, :, 0:na_w] = qn.astype(BF16)
    na_ref[0, :, na_w:2 * na_w] = kn.astype(BF16)
    na_ref[0, :, 2 * na_w:3 * na_w] = pa[:, 2 * na_w:3 * na_w].astype(BF16)
    rw_ref[0] = _dot(hb, w_ref[:, 3 * na_w:3 * na_w + rw_w])
    gd_ref[0] = _dot(hb, w_ref[:, 3 * na_w + rw_w:])


def _inproj(xa, mods, g, w_all, qkg, avg, n_ctx, na_w, rw_w, gd_w):
    b, l, d = xa.shape
    nt = l // TM
    kern = functools.partial(_inproj_kernel, na_w=na_w, rw_w=rw_w)
    return pl.pallas_call(
        kern,
        grid=(b, nt),
        in_specs=[pl.BlockSpec((1, TM, d), lambda i, t: (i, t, 0)),
                  pl.BlockSpec((1, 6, d), _mod_index(n_ctx // TM, b)),
                  pl.BlockSpec((1, d), lambda i, t: (0, 0)),
                  pl.BlockSpec(w_all.shape, lambda i, t: (0, 0)),
                  pl.BlockSpec(qkg.shape, lambda i, t: (0, 0)),
                  pl.BlockSpec(avg.shape, lambda i, t: (0, 0))],
        out_specs=[pl.BlockSpec((1, TM, 3 * na_w), lambda i, t: (i, t, 0)),
                   pl.BlockSpec((1, TM, rw_w), lambda i, t: (i, t, 0)),
                   pl.BlockSpec((1, TM, gd_w), lambda i, t: (i, t, 0))],
        out_shape=[jax.ShapeDtypeStruct((b, l, 3 * na_w), BF16),
                   jax.ShapeDtypeStruct((b, l, rw_w), F32),
                   jax.ShapeDtypeStruct((b, l, gd_w), F32)],
        compiler_params=_cparams(("arbitrary", "arbitrary")),
        name="inproj",
    )(xa, mods, g, w_all, qkg, avg)


def _na_bias_tables(rpb, rows):
    n_heads = rpb.shape[0]
    nq, nk = NA_QROWS * GRID_W, NA_KROWS * GRID_W
    qc = np.arange(GRID_W)[:, None]
    kc = np.arange(GRID_W)[None, :]
    c0 = np.clip(qc - NA_WIN_COLS // 2, 0, GRID_W - NA_WIN_COLS)
    col_ok = (kc >= c0) & (kc < c0 + NA_WIN_COLS)
    dc = kc - qc + NA_WIN_COLS - 1
    pick_col = (dc[None] == np.arange(2 * NA_WIN_COLS - 1)[:, None, None]).astype(np.float32)
    by_col = jnp.einsum('hrd,dqk->hrqk', rpb, pick_col, precision=HIGHEST)
    tables = []
    for r0 in (0, NA_QROWS, rows - NA_QROWS):
        ks = int(np.clip(r0 - NA_WIN_ROWS // 2, 0, rows - NA_KROWS))
        qr = (r0 + np.arange(NA_QROWS))[:, None]
        kr = (ks + np.arange(NA_KROWS))[None, :]
        s0 = np.clip(qr - NA_WIN_ROWS // 2, 0, rows - NA_WIN_ROWS)
        row_ok = (kr >= s0) & (kr < s0 + NA_WIN_ROWS)
        dr = kr - qr + NA_WIN_ROWS - 1
        pick_row = ((dr[None] == np.arange(2 * NA_WIN_ROWS - 1)[:, None, None]) & row_ok[None]).astype(np.float32)
        bias = jnp.einsum('rab,hrqk->haqbk', pick_row, by_col, precision=HIGHEST)
        ok = row_ok[:, None, :, None] & col_ok[None, :, None, :]
        tables.append(jnp.where(ok[None], bias, MASKED).reshape(n_heads, nq, nk))
    tables.append(jnp.full((n_heads, nq, nk), MASKED, F32))
    return jnp.stack(tables).reshape(4, n_heads // 2, 2, nq, nk)


def _na_kernel(q_ref, k_ref, v_ref, bias_ref, o_ref, *, n_ctx, rows):
    j = pl.program_id(2)
    nk = NA_KROWS * GRID_W
    ks_row = jnp.clip((j - 1) * NA_QROWS - NA_WIN_ROWS // 2, 0, rows - NA_KROWS)
    kstart = pl.multiple_of(n_ctx + ks_row * GRID_W, GRID_W)
    q = q_ref[0]
    kl = k_ref[0, pl.ds(kstart, nk), :]
    vl = v_ref[0, pl.ds(kstart, nk), :]
    kc = k_ref[0, 0:n_ctx, :]
    vc = v_ref[0, 0:n_ctx, :]
    lane = lax.broadcasted_iota(jnp.int32, q.shape, 1)
    outs = []
    for h in range(2):
        in_head = (lane >= h * HEAD_DIM) & (lane < (h + 1) * HEAD_DIM)
        qh = jnp.where(in_head, q, jnp.zeros_like(q))
        s_loc = _dot_nt(qh, kl) + bias_ref[0, 0, h]
        s_ctx = _dot_nt(qh, kc)
        m = jnp.maximum(jnp.max(s_loc, axis=-1, keepdims=True), jnp.max(s_ctx, axis=-1, keepdims=True))
        p_loc = jnp.exp(s_loc - m)
        p_ctx = jnp.exp(s_ctx - m)
        den = jnp.sum(p_loc, axis=-1, keepdims=True) + jnp.sum(p_ctx, axis=-1, keepdims=True)
        o = _dot(p_loc.astype(BF16), vl) + _dot(p_ctx.astype(BF16), vc)
        outs.append(o / den)
    o_ref[0] = jnp.where(lane < HEAD_DIM, outs[0], outs[1]).astype(o_ref.dtype)


def _na_attention(p_na, bias, n_ctx, na_w):
    b, l, _ = p_na.shape
    rows = (l - n_ctx) // GRID_W
    nq = NA_QROWS * GRID_W
    assert n_ctx == nq and rows >= NA_KROWS and rows % NA_QROWS == 0
    n_pairs = na_w // LANES
    nblk = l // nq
    kern = functools.partial(_na_kernel, n_ctx=n_ctx, rows=rows)

    def bias_index(i, hp, j):
        pat = jnp.where(j == 0, 3, jnp.where(j == 1, 0, jnp.where(j == nblk - 1, 2, 1)))
        return (pat, hp, 0, 0, 0)

    return pl.pallas_call(
        kern,
        grid=(b, n_pairs, nblk),
        in_specs=[pl.BlockSpec((1, nq, LANES), lambda i, hp, j: (i, j, hp)),
                  pl.BlockSpec((1, l, LANES), lambda i, hp, j: (i, 0, n_pairs + hp)),
                  pl.BlockSpec((1, l, LANES), lambda i, hp, j: (i, 0, 2 * n_pairs + hp)),
                  pl.BlockSpec((1, 1, 2) + bias.shape[3:], bias_index)],
        out_specs=pl.BlockSpec((1, nq, LANES), lambda i, hp, j: (i, j, hp)),
        out_shape=jax.ShapeDtypeStruct((b, l, na_w), BF16),
        compiler_params=_cparams(("arbitrary", "arbitrary", "arbitrary")),
        name="na_attention",
    )(p_na, p_na, p_na, bias)


def _tri_masks(reverse):
    t = lax.broadcasted_iota(jnp.int32, (CHUNK, CHUNK), 0)
    s = lax.broadcasted_iota(jnp.int32, (CHUNK, CHUNK), 1)
    earlier_or_same = (s >= t) if reverse else (s <= t)
    earlier = (s > t) if reverse else (s < t)
    return earlier_or_same, earlier, t == s


def _each(fn, *lists):
    return [fn(*args) for args in zip(*lists)]


def _bf16(values):
    return [u.astype(BF16) for u in values]


def _unit_tri_inverse(a, eye):
    assert CHUNK == 64
    x1 = [-u for u in a]
    x1b = _bf16(x1)
    x2 = _each(_dot, x1b, x1b)
    x2b = _bf16(x2)
    x4 = _each(_dot, x2b, x2b)
    x3 = _each(_dot, x1b, x2b)
    x4b = _bf16(x4)
    x8 = _each(_dot, x4b, x4b)
    p1 = _each(lambda u1, u2, u3: eye + u1 + u2 + u3, x1, x2, x3)
    x8b = _bf16(x8)
    x12 = _each(_dot, x4b, x8b)
    x16 = _each(_dot, x8b, x8b)
    p2 = _each(lambda u4, u8, u12: eye + u4 + u8 + u12, x4, x8, x12)
    p12 = _each(_dot, _bf16(p1), _bf16(p2))
    x16b = _bf16(x16)
    x32 = _each(_dot, x16b, x16b)
    x48 = _each(_dot, x16b, _bf16(x32))
    p3 = _each(lambda u16, u32, u48: eye + u16 + u32 + u48, x16, x32, x48)
    return _each(_dot, _bf16(p12), _bf16(p3))


def _head_lanes(width):
    lane = lax.broadcasted_iota(jnp.int32, (1, width), 1)
    return [(lane >= h * HEAD_DIM) & (lane < (h + 1) * HEAD_DIM) for h in range(width // HEAD_DIM)]


def _stack_heads(x, heads):
    return jnp.concatenate([jnp.where(in_head, x, 0.0) for in_head in heads], axis=0)


def _stacked_masks(n, reverse):
    row = lax.broadcasted_iota(jnp.int32, (n, n), 0)
    col = lax.broadcasted_iota(jnp.int32, (n, n), 1)
    t, s = row % CHUNK, col % CHUNK
    same_head = (row // CHUNK) == (col // CHUNK)
    earlier = ((s > t) if reverse else (s < t)) & same_head
    earlier_or_same = earlier | (row == col)
    return earlier, earlier_or_same, row == col


def _chunk_matrices(a_ka, a_kk, a_ra, a_rk, ks, rs, vs, kds, ads, eye):
    t_inv = _bf16(_unit_tri_inverse(a_ka, eye.astype(F32)))
    vsb, adsb, a_rab = _bf16(vs), _bf16(ads), _bf16(a_ra)
    av = _each(_dot, _bf16(a_kk), vsb)
    w1 = _bf16(_each(_dot, t_inv, _bf16(ks)))
    u0 = _bf16(_each(_dot, t_inv, _bf16(av)))
    rkv = _each(_dot, _bf16(a_rk), vsb)
    w2 = _each(lambda r, a, w: r - _dot(a, w), rs, a_rab, w1)
    y1 = _each(lambda y, a, u: y - _dot(a, u), rkv, a_rab, u0)
    mm = _each(_dot_tn, adsb, w1)
    nn = _each(lambda kd, v, ad, u: _dot_tn(kd, v) - _dot_tn(ad, u), _bf16(kds), vsb, adsb, u0)
    return list(zip(w2, y1, mm, nn))


def _store_chunk(refs, d, c0, n_heads, w2, y1, mm, nn, p_end):
    w2_ref, y1_ref, m_ref, n_ref = refs
    rows = pl.ds(c0, CHUNK)
    eye = (lax.broadcasted_iota(jnp.int32, (HEAD_DIM, HEAD_DIM), 0)
           == lax.broadcasted_iota(jnp.int32, (HEAD_DIM, HEAD_DIM), 1))
    for h in range(n_heads):
        lo, hi = h * HEAD_DIM, (h + 1) * HEAD_DIM
        r_lo, r_hi = h * CHUNK, (h + 1) * CHUNK
        w2_ref[d, 0, h, rows, :] = w2[r_lo:r_hi, lo:hi]
        y1_ref[d, 0, h, rows, :] = y1[r_lo:r_hi, lo:hi]
        m_ref[d, 0, h, rows, :] = jnp.where(eye, p_end[:, lo:hi], 0.0) - mm[lo:hi, lo:hi]
        n_ref[d, 0, h, rows, :] = nn[lo:hi, lo:hi]


def _scan_out_specs(b, n_heads, l):
    shape = jax.ShapeDtypeStruct((2, b, n_heads, l, HEAD_DIM), F32)
    spec = pl.BlockSpec((2, 1, n_heads, TM, HEAD_DIM), lambda i, t: (0, i, 0, t, 0))
    return [shape] * 4, [spec] * 4


def _seq_edges(t, n_ctx_tiles, n_tiles):
    first = (t == 0) | (t == n_ctx_tiles)
    last = (t == n_ctx_tiles - 1) | (t == n_tiles - 1)
    return first, last


def _fill_halo(ext_ref, x_ref, prev_ref, next_ref, first, last):
    ext_ref[0:8, :] = jnp.where(first, 0.0, prev_ref[0])
    ext_ref[8:8 + TM, :] = x_ref[0]
    ext_ref[8 + TM:16 + TM, :] = jnp.where(last, 0.0, next_ref[0])


def _halo_specs(width, n_tiles):
    per = TM // 8
    return [pl.BlockSpec((1, TM, width), lambda i, t: (i, t, 0)),
            pl.BlockSpec((1, 8, width), lambda i, t: (i, jnp.maximum(t * per - 1, 0), 0)),
            pl.BlockSpec((1, 8, width), lambda i, t: (i, jnp.minimum((t + 1) * per, n_tiles * per - 1), 0))]


def _full_spec(a):
    return pl.BlockSpec(a.shape, lambda i, t: (0,) * a.ndim)


def _rwkv_prep_kernel(x_ref, prev_ref, next_ref, mu_ref, w0_ref, wup_ref, a0_ref, aup_ref, gup_ref,
                      kk_ref, ka_ref, rk_ref, ones_ref,
                      w2_ref, y1_ref, m_ref, n_ref, bv_ref, gate_ref,
                      ext_ref, tok_ref, *, n_ctx_tiles, n_tiles, width):
    t = pl.program_id(1)
    first, last = _seq_edges(t, n_ctx_tiles, n_tiles)
    _fill_halo(ext_ref, x_ref, prev_ref, next_ref, first, last)
    p = ext_ref[8:8 + TM, :]
    prev = ext_ref[7:7 + TM, :]
    nxt = ext_ref[9:9 + TM, :]
    ps = p + mu_ref[...] * (0.5 * (prev + nxt) - p)
    r = ps[:, 0:width]
    k = ps[:, width:2 * width]
    v = ps[:, 2 * width:3 * width]
    lo = ps[:, 3 * width:3 * width + LANES]
    ones_bd = ones_ref[...]
    gate_ref[0] = _dot(jax.nn.sigmoid(lo), gup_ref[...])
    kq = k * kk_ref[...]
    kk = kq * lax.rsqrt(_dot_split(kq * kq, ones_bd) + NORM_EPS)
    tanh_lo = jnp.tanh(lo)
    tok_ref[0] = r
    tok_ref[1] = kk
    tok_ref[2] = v
    k_sum = jnp.zeros_like(k)
    for d in range(2):
        z = w0_ref[d:d + 1, :] + _dot(tanh_lo, wup_ref[d])
        w_log = -jax.nn.softplus(-z) - 0.5
        a = jax.nn.sigmoid(a0_ref[d:d + 1, :] + _dot(lo, aup_ref[d]))
        k_dir = k * (1.0 + (a - 1.0) * ka_ref[...])
        k_sum = k_sum + k_dir
        tok_ref[3 + 3 * d] = -jnp.exp(w_log)
        tok_ref[4 + 3 * d] = kk * a
        tok_ref[5 + 3 * d] = k_dir
    bonus = _dot_split(r * k_sum * rk_ref[...], ones_bd)
    bv_ref[0] = bonus * v

    heads = _head_lanes(width)
    out_refs = (w2_ref, y1_ref, m_ref, n_ref)

    def chunk_body(c, carry):
        args = [[] for _ in range(9)]
        where = []
        for cc in range(CHUNKS_PER_STEP):
            c0 = pl.multiple_of((c * CHUNKS_PER_STEP + cc) * CHUNK, CHUNK)
            rows = pl.ds(c0, CHUNK)
            r_c, kk_c, v_c = tok_ref[0, rows, :], tok_ref[1, rows, :], tok_ref[2, rows, :]
            vs = _stack_heads(v_c, heads)
            for d in range(2):
                logw, alpha, k_c = (tok_ref[3 + 3 * d, rows, :], tok_ref[4 + 3 * d, rows, :],
                                    tok_ref[5 + 3 * d, rows, :])
                incl, _, _ = _tri_masks(reverse=(d == 1))
                ci = _dot(incl.astype(F32), logw, precision=HIGHEST)
                e_ci, e_nci, e_ce = jnp.exp(ci), jnp.exp(-ci), jnp.exp(ci - logw)
                p_end = e_ci[0:1, :] if d == 1 else e_ci[CHUNK - 1:CHUNK, :]
                ks, rs, khs, ahs = (_stack_heads(u, heads)
                                    for u in (kk_c * e_ce, r_c * e_ci, k_c * e_nci, alpha * e_nci))
                earlier, earlier_or_same, eye = _stacked_masks(ks.shape[0], reverse=(d == 1))
                ksb, rsb, khsb, ahsb = _bf16((ks, rs, khs, ahs))
                a_ka = jnp.where(earlier, _dot_nt(ksb, ahsb), 0.0)
                a_kk = jnp.where(earlier, _dot_nt(ksb, khsb), 0.0)
                a_ra = jnp.where(earlier_or_same, _dot_nt(rsb, ahsb), 0.0)
                a_rk = jnp.where(earlier_or_same, _dot_nt(rsb, khsb), 0.0)
                for slot, value in zip(args, (a_ka, a_kk, a_ra, a_rk, ks, rs, vs, khs * p_end, ahs * p_end)):
                    slot.append(value)
                where.append((d, c0, p_end))
        for (d, c0, p_end), res in zip(where, _chunk_matrices(*args, eye)):
            _store_chunk(out_refs, d, c0, len(heads), *res, p_end)
        return carry

    lax.fori_loop(0, TM // CHUNK // CHUNKS_PER_STEP, chunk_body, 0)


def _rwkv_prepare(p_rw, pr, n_ctx):
    b, l, pw = p_rw.shape
    width = pr['rw_k_k'].shape[-1]
    n_heads = width // HEAD_DIM
    nt = l // TM
    kern = functools.partial(_rwkv_prep_kernel, n_ctx_tiles=n_ctx // TM, n_tiles=nt, width=width)
    params = [pr['rw_mu_p'], pr['rw_w0'], pr['rw_wup_p'], pr['rw_a0'], pr['rw_aup_p'], pr['rw_gup_p'],
              pr['rw_k_k'].reshape(1, width), pr['rw_k_a'].reshape(1, width), pr['rw_r_k'].reshape(1, width),
              pr['ones_bd']]
    scan_shapes, scan_specs = _scan_out_specs(b, n_heads, l)
    tok_spec = pl.BlockSpec((1, TM, width), lambda i, t: (i, t, 0))
    return pl.pallas_call(
        kern,
        grid=(b, nt),
        in_specs=_halo_specs(pw, nt) + [_full_spec(a) for a in params],
        out_specs=scan_specs + [tok_spec, tok_spec],
        out_shape=scan_shapes + [jax.ShapeDtypeStruct((b, l, width), F32)] * 2,
        scratch_shapes=[pltpu.VMEM((TM + 16, pw), F32), pltpu.VMEM((9, TM, width), F32)],
        compiler_params=_cparams(("arbitrary", "arbitrary")),
        name="rwkv_prepare",
    )(p_rw, p_rw, p_rw, *params)


def _scan_kernel(w2f_ref, y1f_ref, mf_ref, nf_ref, w2b_ref, y1b_ref, mb_ref, nb_ref, yf_ref, yb_ref, h_ref):
    @pl.when(pl.program_id(0) == 0)
    def _():
        h_ref[...] = jnp.zeros_like(h_ref)

    n_batch, n_heads = h_ref.shape[1], h_ref.shape[2]
    n_sub = yf_ref.shape[2] // CHUNK
    dirs = ((w2f_ref, y1f_ref, mf_ref, nf_ref, yf_ref), (w2b_ref, y1b_ref, mb_ref, nb_ref, yb_ref))
    chains = [(d, i, h) for d in range(2) for i in range(n_batch) for h in range(n_heads)]
    states = [h_ref[d, i, h] for d, i, h in chains]
    for j in range(n_sub):
        for c, (d, i, h) in enumerate(chains):
            w2_ref, y1_ref, m_ref, n_ref, y_ref = dirs[d]
            jj = j if d == 0 else n_sub - 1 - j
            rows = slice(jj * CHUNK, (jj + 1) * CHUNK)
            y_ref[i, h, rows, :] = _dot(w2_ref[0, i, h, rows, :], states[c]) + y1_ref[0, i, h, rows, :]
            states[c] = _dot(m_ref[0, i, h, rows, :], states[c], precision=HIGHEST) + n_ref[0, i, h, rows, :]
    for c, (d, i, h) in enumerate(chains):
        h_ref[d, i, h] = states[c]


def _scan(w2, y1, m, n, n_ctx):
    _, b, n_heads, l, _ = w2.shape
    nc = l // TM
    ncc = n_ctx // TM

    def bwd_chunk(i):
        return jnp.where(i < ncc, ncc - 1 - i, nc - 1 + ncc - i)

    blk = (1, b, n_heads, TM, HEAD_DIM)
    fwd = pl.BlockSpec(blk, lambda i: (0, 0, 0, i, 0))
    bwd = pl.BlockSpec(blk, lambda i: (1, 0, 0, bwd_chunk(i), 0))
    out_shape = jax.ShapeDtypeStruct((b, n_heads, l, HEAD_DIM), F32)
    return pl.pallas_call(
        _scan_kernel,
        grid=(nc,),
        in_specs=[fwd] * 4 + [bwd] * 4,
        out_specs=[pl.BlockSpec(blk[1:], lambda i: (0, 0, i, 0)),
                   pl.BlockSpec(blk[1:], lambda i: (0, 0, bwd_chunk(i), 0))],
        out_shape=[out_shape, out_shape],
        scratch_shapes=[pltpu.VMEM((2, b, n_heads, HEAD_DIM, HEAD_DIM), F32)],
        compiler_params=_cparams(("arbitrary",)),
        name="chunk_scan",
    )(w2, y1, m, n, w2, y1, m, n)


def _gdn_prep_kernel(x_ref, prev_ref, next_ref, cw_ref, alog_ref, dtb_ref, ones_ref,
                     w2_ref, y1_ref, m_ref, n_ref,
                     ext_ref, tok_ref, *, n_ctx_tiles, n_tiles, width):
    t = pl.program_id(1)
    first, last = _seq_edges(t, n_ctx_tiles, n_tiles)
    _fill_halo(ext_ref, x_ref, prev_ref, next_ref, first, last)
    half = GD_CONV // 2
    conv = jnp.zeros((TM, 3 * width), F32)
    for j in range(GD_CONV):
        conv = conv + cw_ref[j:j + 1, :] * ext_ref[8 + j - half:8 + j - half + TM, 0:3 * width]
    qkv = conv * jax.nn.sigmoid(conv)
    ones_bd = ones_ref[...]

    def l2n(u):
        return u * lax.rsqrt(_dot_split(u * u, ones_bd) + NORM_EPS)

    q = l2n(qkv[:, 0:width]) * HEAD_DIM ** -0.5
    k = l2n(qkv[:, width:2 * width])
    v = qkv[:, 2 * width:3 * width]
    ab = ext_ref[8:8 + TM, 4 * width:4 * width + LANES]
    g_all = -jnp.exp(alog_ref[...]) * jax.nn.softplus(ab + dtb_ref[...])
    beta_all = jax.nn.sigmoid(ab)
    heads = _head_lanes(width)
    n_heads = len(heads)
    tok_ref[0] = q
    tok_ref[1] = k
    tok_ref[2] = v
    for d in range(2):
        g = jnp.zeros((TM, width), F32)
        beta = jnp.zeros((TM, width), F32)
        for h, in_head in enumerate(heads):
            ja = d * n_heads + h
            jb = 2 * n_heads + ja
            g = jnp.where(in_head, g_all[:, ja:ja + 1], g)
            beta = jnp.where(in_head, beta_all[:, jb:jb + 1], beta)
        tok_ref[3 + 2 * d] = g
        tok_ref[4 + 2 * d] = k * beta
    out_refs = (w2_ref, y1_ref, m_ref, n_ref)

    def chunk_inputs(c0, d):
        rows = pl.ds(c0, CHUNK)
        q_c, k_c, v_c = tok_ref[0, rows, :], tok_ref[1, rows, :], tok_ref[2, rows, :]
        g, kb = tok_ref[3 + 2 * d, rows, :], tok_ref[4 + 2 * d, rows, :]
        incl, _, _ = _tri_masks(reverse=(d == 1))
        ci = _dot(incl.astype(F32), g, precision=HIGHEST)
        ce = ci - g
        ci_end = ci[0:1, :] if d == 1 else ci[CHUNK - 1:CHUNK, :]
        ks, rs, kbs, vs, kds, ads = (_stack_heads(u, heads) for u in (
            k_c * jnp.exp(ce), q_c * jnp.exp(ci), kb, v_c,
            kb * jnp.exp(ci_end - ci), kb * jnp.exp(ci_end - ce)))
        earlier, earlier_or_same, eye = _stacked_masks(ks.shape[0], reverse=(d == 1))
        ci_col = jnp.concatenate([ci[:, h * HEAD_DIM:h * HEAD_DIM + 1] for h in range(n_heads)], axis=0)
        ce_col = jnp.concatenate([ce[:, h * HEAD_DIM:h * HEAD_DIM + 1] for h in range(n_heads)], axis=0)
        ci_row = jnp.sum(jnp.where(eye, ci_col, 0.0), axis=0, keepdims=True)
        ce_row = jnp.sum(jnp.where(eye, ce_col, 0.0), axis=0, keepdims=True)
        kbsb = kbs.astype(BF16)
        kkb = _dot_nt(_stack_heads(k_c, heads).astype(BF16), kbsb)
        qkb = _dot_nt(_stack_heads(q_c, heads).astype(BF16), kbsb)
        a_ka = kkb * jnp.exp(jnp.where(earlier, ce_col - ce_row, MASKED))
        a_kk = kkb * jnp.exp(jnp.where(earlier, ce_col - ci_row, MASKED))
        a_ra = qkb * jnp.exp(jnp.where(earlier_or_same, ci_col - ce_row, MASKED))
        a_rk = qkb * jnp.exp(jnp.where(earlier_or_same, ci_col - ci_row, MASKED))
        return (a_ka, a_kk, a_ra, a_rk, ks, rs, vs, kds, ads), jnp.exp(ci_end), eye

    def chunk_body(c, carry):
        args = [[] for _ in range(9)]
        where = []
        for cc in range(CHUNKS_PER_STEP):
            c0 = pl.multiple_of((c * CHUNKS_PER_STEP + cc) * CHUNK, CHUNK)
            for d in range(2):
                values, p_end, eye = chunk_inputs(c0, d)
                for slot, value in zip(args, values):
                    slot.append(value)
                where.append((d, c0, p_end))
        for (d, c0, p_end), res in zip(where, _chunk_matrices(*args, eye)):
            _store_chunk(out_refs, d, c0, n_heads, *res, p_end)
        return carry

    lax.fori_loop(0, TM // CHUNK // CHUNKS_PER_STEP, chunk_body, 0)


def _gdn_prepare(p_gd, pr, n_ctx):
    b, l, pw = p_gd.shape
    width = pr['gd_width']
    n_heads = width // HEAD_DIM
    nt = l // TM
    kern = functools.partial(_gdn_prep_kernel, n_ctx_tiles=n_ctx // TM, n_tiles=nt, width=width)
    params = [pr['gd_conv_w'], pr['gd_alog_p'], pr['gd_dtb_p'], pr['ones_bd']]
    scan_shapes, scan_specs = _scan_out_specs(b, n_heads, l)
    return pl.pallas_call(
        kern,
        grid=(b, nt),
        in_specs=_halo_specs(pw, nt) + [_full_spec(a) for a in params],
        out_specs=scan_specs,
        out_shape=scan_shapes,
        scratch_shapes=[pltpu.VMEM((TM + 16, pw), F32), pltpu.VMEM((7, TM, width), F32)],
        compiler_params=_cparams(("arbitrary", "arbitrary")),
        name="gdn_prepare",
    )(p_gd, p_gd, p_gd, *params)


def _outproj_kernel(x_ref, mod_ref, na_ref, ryf_ref, ryb_ref, bv_ref, gate_ref, gyf_ref, gyb_ref, z_ref,
                    lnw_ref, lnb_ref, gng_ref, wo_ref, g2_ref, rwh_ref, rwl_ref, rb_ref,
                    x1_ref, h2_ref, route_ref, meta_ref, *, na_w, rw_w):
    n_heads = ryf_ref.shape[1]
    rw_parts, gd_parts = [], []
    for h in range(n_heads):
        y = ryf_ref[0, h] + ryb_ref[0, h]
        yc = y - jnp.mean(y, axis=-1, keepdims=True)
        rw_parts.append(yc * lax.rsqrt(jnp.mean(yc * yc, axis=-1, keepdims=True) + RW_GN_EPS))
        gd_parts.append(_rms(gyf_ref[0, h] + gyb_ref[0, h]) * gng_ref[...])
    yn = jnp.concatenate(rw_parts, axis=-1)
    o_rw = (yn * lnw_ref[...] + lnb_ref[...] + bv_ref[0]) * gate_ref[0]
    z = z_ref[0]
    o_gd = jnp.concatenate(gd_parts, axis=-1) * (z * jax.nn.sigmoid(z))
    o = (_dot(na_ref[0], wo_ref[0:na_w, :])
         + _dot(o_rw.astype(BF16), wo_ref[na_w:na_w + rw_w, :])
         + _dot(o_gd.astype(BF16), wo_ref[na_w + rw_w:, :]))
    mod = mod_ref[0]
    x1 = x_ref[0] + mod[2:3] * o
    x1_ref[0] = x1
    h2 = _rms(x1) * g2_ref[...] * (1.0 + mod[4:5]) + mod[3:4]
    h2_ref[0] = h2
    h2_hi = h2.astype(BF16)
    h2_lo = (h2 - h2_hi.astype(F32)).astype(BF16)
    logits = (_dot(h2_hi, rwh_ref[...]) + (_dot(h2_hi, rwl_ref[...]) + _dot(h2_lo, rwh_ref[...]))) + rb_ref[...]
    lane = lax.broadcasted_iota(jnp.int32, logits.shape, 1)
    chosen, weights = [], []
    onehot = jnp.zeros(logits.shape, F32)
    for kk in range(TOP_K):
        m = jnp.max(logits, axis=-1, keepdims=True)
        idx = jnp.min(jnp.where(logits == m, lane, LANES), axis=-1, keepdims=True)
        chosen.append(idx)
        top_max = m if kk == 0 else top_max
        weights.append(jnp.exp(m - top_max))
        onehot = jnp.where(lane == idx, 1.0, onehot)
        logits = jnp.where(lane == idx, -jnp.inf, logits)
    denom = weights[0] + weights[1] + weights[2] + weights[3]
    tok_i = lax.broadcasted_iota(jnp.int32, (TM, TM), 0)
    tok_j = lax.broadcasted_iota(jnp.int32, (TM, TM), 1)
    before = jnp.where(tok_j < tok_i, 1.0, 0.0).astype(BF16)
    rank = _dot(before, onehot.astype(BF16))
    counts = jnp.sum(onehot, axis=0, keepdims=True)
    count8 = jnp.floor((counts + 7.0) * 0.125) * 8.0
    exp_i = lax.broadcasted_iota(jnp.int32, (LANES, LANES), 0)
    exp_j = lax.broadcasted_iota(jnp.int32, (LANES, LANES), 1)
    seg_start = _dot(jnp.broadcast_to(count8, (8, LANES)), jnp.where(exp_i < exp_j, 1.0, 0.0),
                     precision=HIGHEST)[0:1]
    slot_of = seg_start + rank
    route = jnp.zeros(logits.shape, F32)
    for kk in range(TOP_K):
        slot = jnp.sum(jnp.where(lane == chosen[kk], slot_of, 0.0), axis=-1, keepdims=True)
        route = jnp.where(lane == kk, weights[kk] / denom, route)
        route = jnp.where(lane == TOP_K + kk, slot, route)
    route_ref[0] = route
    row8 = lax.broadcasted_iota(jnp.int32, (8, LANES), 0)
    meta_ref[0, 0] = jnp.where(row8 == 0, count8, jnp.where(row8 == 1, seg_start, 0.0))


def _outproj(xa, mods, o_na, ry, bv, gate, gy, p_gd, pr, n_ctx):
    b, l, d = xa.shape
    nt = l // TM
    na_w = o_na.shape[-1]
    n_heads = ry[0].shape[1]
    rw_w = n_heads * HEAD_DIM
    kern = functools.partial(_outproj_kernel, na_w=na_w, rw_w=rw_w)
    tok = lambda w: pl.BlockSpec((1, TM, w), lambda i, t: (i, t, 0))
    head_major = pl.BlockSpec((1, n_heads, TM, HEAD_DIM), lambda i, t: (i, 0, t, 0))
    params = [pr['rw_ln_w'].reshape(1, rw_w), pr['rw_ln_b'].reshape(1, rw_w), pr['gd_norm_g'].reshape(1, HEAD_DIM),
              pr['w_out_bf'], pr['norm_ffn_g'], pr['router_w_hi'], pr['router_w_lo'], pr['router_b_p']]
    return pl.pallas_call(
        kern,
        grid=(b, nt),
        in_specs=[tok(d), pl.BlockSpec((1, 6, d), _mod_index(n_ctx // TM, b)), tok(na_w),
                  head_major, head_major, tok(rw_w), tok(rw_w), head_major, head_major,
                  pl.BlockSpec((1, TM, rw_w), lambda i, t: (i, t, 3))]
                 + [_full_spec(a) for a in params],
        out_specs=[tok(d), tok(d), tok(LANES), pl.BlockSpec((1, 1, 8, LANES), lambda i, t: (i, t, 0, 0))],
        out_shape=[jax.ShapeDtypeStruct((b, l, d), F32), jax.ShapeDtypeStruct((b, l, d), F32),
                   jax.ShapeDtypeStruct((b, l, LANES), F32), jax.ShapeDtypeStruct((b, nt, 8, LANES), F32)],
        compiler_params=_cparams(("arbitrary", "arbitrary")),
        name="outproj_router",
    )(xa, mods, o_na, ry[0], ry[1], bv, gate, gy[0], gy[1], p_gd, *params)


MOE_BM = 512
MOE_EXPERT_VMEM = 56 * 1024 * 1024
SEG_ALIGN = 8
SEG_BITS = (32, 16, 8, 4, 2, 1)


def _n_slots(n_exp):
    return _round_up(TOP_K * TM + n_exp * (SEG_ALIGN - 1), LANES)


def _segment_dma(segs_ref, buf_ref, hbm_ref, sem, n_exp, to_hbm, wait):
    def body(e, carry):
        start = segs_ref[0, 0, e]
        units = segs_ref[0, 0, n_exp + e]
        offset = segs_ref[0, 0, 2 * n_exp + e]
        for bit in SEG_BITS:
            done = (units & ~(2 * bit - 1)) * SEG_ALIGN
            rows = bit * SEG_ALIGN

            @pl.when((units & bit) != 0)
            def _():
                in_buf = buf_ref.at[pl.ds(pl.multiple_of(start + done, SEG_ALIGN), rows), :]
                in_hbm = hbm_ref.at[pl.ds(pl.multiple_of(offset + done, SEG_ALIGN), rows), :]
                copy = (pltpu.make_async_copy(in_buf, in_hbm, sem) if to_hbm
                        else pltpu.make_async_copy(in_hbm, in_buf, sem))
                if wait:
                    copy.wait()
                else:
                    copy.start()
        return carry
    lax.fori_loop(0, n_exp, body, 0)


def _slot_lanes(route, n_slots):
    lane = lax.broadcasted_iota(jnp.int32, (route.shape[0], n_slots), 1)
    slots = [route[:, TOP_K + k:TOP_K + k + 1].astype(jnp.int32) for k in range(TOP_K)]
    return lane, slots


def _dispatch_kernel(segs_ref, segs_prev_ref, gaps_ref, h_ref, route_ref, xb_ref, xs_ref, zero_ref, sem, gap_sem,
                     *, n_exp):
    i = pl.program_id(0)
    slot = i % 2

    @pl.when(i == 0)
    def _():
        zero_ref[...] = jnp.zeros_like(zero_ref)
        _segment_dma(gaps_ref, zero_ref, xb_ref, gap_sem, n_exp, to_hbm=True, wait=False)
        _segment_dma(gaps_ref, zero_ref, xb_ref, gap_sem, n_exp, to_hbm=True, wait=True)
        n_blocks = xb_ref.shape[0] // MOE_BM

        def block_copy(blk):
            return pltpu.make_async_copy(zero_ref, xb_ref.at[pl.ds(pl.multiple_of(blk * MOE_BM, MOE_BM), MOE_BM), :],
                                         gap_sem)

        def start(blk, carry):
            block_copy(blk).start()
            return carry

        def wait(blk, carry):
            block_copy(blk).wait()
            return carry

        lax.fori_loop(gaps_ref[0, 0, 3 * n_exp], n_blocks, start, 0)
        lax.fori_loop(gaps_ref[0, 0, 3 * n_exp], n_blocks, wait, 0)

    lane, slots = _slot_lanes(route_ref[...], xs_ref.shape[1])
    picked = lane == slots[0]
    for k in range(1, TOP_K):
        picked = picked | (lane == slots[k])
    onehot = jnp.where(picked, 1.0, 0.0).astype(BF16)
    xs_ref[slot] = _dot_tn(onehot, h_ref[...].astype(BF16))
    _segment_dma(segs_ref, xs_ref.at[slot], xb_ref, sem.at[slot], n_exp, to_hbm=True, wait=False)

    @pl.when(i > 0)
    def _():
        _segment_dma(segs_prev_ref, xs_ref.at[1 - slot], xb_ref, sem.at[1 - slot], n_exp, to_hbm=True, wait=True)

    @pl.when(i == pl.num_programs(0) - 1)
    def _():
        _segment_dma(segs_ref, xs_ref.at[slot], xb_ref, sem.at[slot], n_exp, to_hbm=True, wait=True)


def _seg_spec(index):
    return pl.BlockSpec((1, 1, LANES), index, memory_space=pltpu.SMEM)


def _dispatch(h2, route, segs, gaps, n_pad, n_exp):
    t, d = h2.shape
    nt = t // TM
    kern = functools.partial(_dispatch_kernel, n_exp=n_exp)
    return pl.pallas_call(
        kern,
        grid=(nt,),
        in_specs=[_seg_spec(lambda i: (i, 0, 0)),
                  _seg_spec(lambda i: (jnp.maximum(i - 1, 0), 0, 0)),
                  _seg_spec(lambda i: (0, 0, 0)),
                  pl.BlockSpec((TM, d), lambda i: (i, 0)),
                  pl.BlockSpec((TM, LANES), lambda i: (i, 0))],
        out_specs=pl.BlockSpec(memory_space=pl.ANY),
        out_shape=jax.ShapeDtypeStruct((n_pad, d), F32),
        scratch_shapes=[pltpu.VMEM((2, _n_slots(n_exp), d), F32), pltpu.VMEM((MOE_BM, d), F32),
                        pltpu.SemaphoreType.DMA((2,)), pltpu.SemaphoreType.DMA(())],
        compiler_params=_cparams(("arbitrary",)),
        name="moe_dispatch",
    )(segs, segs, gaps, h2, route)


def _expert_kernel(blk_e_ref, n_used_ref, x_ref, wgu_ref, bgu_ref, wd_ref, bd_ref, y_ref, wgu_bf, wd_bf):
    i = pl.program_id(0)
    used = i < n_used_ref[0]
    new_expert = (i == 0) | (blk_e_ref[i] != blk_e_ref[jnp.maximum(i - 1, 0)])

    @pl.when(used & new_expert)
    def _():
        wgu_bf[...] = wgu_ref[0, 0].astype(BF16)
        wd_bf[...] = wd_ref[0, 0].astype(BF16)

    @pl.when(used)
    def _():
        x = x_ref[...].astype(BF16)
        d_exp = wd_bf.shape[0]
        gu = _dot(x, wgu_bf[...]) + bgu_ref[0, 0]
        gate = jnp.minimum(gu[:, :d_exp], SWIGLU_LIMIT)
        lin = jnp.clip(gu[:, d_exp:], -SWIGLU_LIMIT, SWIGLU_LIMIT)
        act = gate * jax.nn.sigmoid(SWIGLU_ALPHA * gate) * (lin + 1.0)
        y_ref[...] = _dot(act.astype(BF16), wd_bf[...]) + bd_ref[0, 0]

    @pl.when(jnp.logical_not(used))
    def _():
        y_ref[...] = jnp.zeros_like(y_ref)


def _expert_blocks(xb, blk_e, n_used, layer, w_gu, b_gu, w_down, b_down):
    n_pad, d = xb.shape
    n_blocks = n_pad // MOE_BM
    depth, n_exp, _, d_gu = w_gu.shape
    grid_spec = pltpu.PrefetchScalarGridSpec(
        num_scalar_prefetch=2,
        grid=(n_blocks,),
        in_specs=[pl.BlockSpec((MOE_BM, d), lambda i, be, nu: (jnp.minimum(i, nu[0] - 1), 0)),
                  pl.BlockSpec((1, 1, d, d_gu), lambda i, be, nu: (layer, be[i], 0, 0)),
                  pl.BlockSpec((1, 1, 1, d_gu), lambda i, be, nu: (layer, be[i], 0, 0)),
                  pl.BlockSpec((1, 1, d_gu // 2, d), lambda i, be, nu: (layer, be[i], 0, 0)),
                  pl.BlockSpec((1, 1, 1, d), lambda i, be, nu: (layer, be[i], 0, 0))],
        out_specs=pl.BlockSpec((MOE_BM, d), lambda i, be, nu: (i, 0)),
        scratch_shapes=[pltpu.VMEM((d, d_gu), BF16), pltpu.VMEM((d_gu // 2, d), BF16)])
    return pl.pallas_call(
        _expert_kernel,
        grid_spec=grid_spec,
        out_shape=jax.ShapeDtypeStruct((n_pad, d), F32),
        compiler_params=pltpu.CompilerParams(dimension_semantics=("arbitrary",),
                                             vmem_limit_bytes=MOE_EXPERT_VMEM),
        name="moe_experts",
    )(blk_e, n_used, xb, w_gu, b_gu.reshape(depth, n_exp, 1, d_gu), w_down, b_down.reshape(depth, n_exp, 1, d))


def _combine_kernel(segs_ref, segs_next_ref, y_hbm, x_ref, mod_ref, route_ref, o_ref, ybuf, sem, *, n_exp):
    i = pl.program_id(0)
    slot = i % 2

    @pl.when(i == 0)
    def _():
        ybuf[...] = jnp.zeros_like(ybuf)
        _segment_dma(segs_ref, ybuf.at[0], y_hbm, sem.at[0], n_exp, to_hbm=False, wait=False)

    @pl.when(i + 1 < pl.num_programs(0))
    def _():
        _segment_dma(segs_next_ref, ybuf.at[1 - slot], y_hbm, sem.at[1 - slot], n_exp, to_hbm=False, wait=False)

    _segment_dma(segs_ref, ybuf.at[slot], y_hbm, sem.at[slot], n_exp, to_hbm=False, wait=True)
    y = ybuf[slot].astype(BF16)
    route = route_ref[...]
    lane, slots = _slot_lanes(route, ybuf.shape[1])
    weights = jnp.zeros(lane.shape, F32)
    for k in range(TOP_K):
        weights = jnp.where(lane == slots[k], route[:, k:k + 1], weights)
    o_ref[...] = x_ref[...] + mod_ref[0, 5:6, :] * _dot(weights.astype(BF16), y)


def _combine(yb, segs, x1, mods, route, n_batch, l, n_ctx, n_exp, latent_only):
    t, d = x1.shape
    nt = t // TM
    per_sample = l // TM
    n_ctx_tiles = n_ctx // TM
    kern = functools.partial(_combine_kernel, n_exp=n_exp)

    def mod_index(i):
        return (jnp.where(i % per_sample < n_ctx_tiles, n_batch, i // per_sample), 0, 0)

    if latent_only:
        lat_tiles = per_sample - n_ctx_tiles
        out_rows = n_batch * lat_tiles * TM
        out_index = lambda i: ((i // per_sample) * lat_tiles + jnp.maximum(i % per_sample - n_ctx_tiles, 0), 0)
    else:
        out_rows = t
        out_index = lambda i: (i, 0)

    return pl.pallas_call(
        kern,
        grid=(nt,),
        in_specs=[_seg_spec(lambda i: (i, 0, 0)),
                  _seg_spec(lambda i: (jnp.minimum(i + 1, nt - 1), 0, 0)),
                  pl.BlockSpec(memory_space=pl.ANY),
                  pl.BlockSpec((TM, d), lambda i: (i, 0)),
                  pl.BlockSpec((1, 6, d), mod_index),
                  pl.BlockSpec((TM, LANES), lambda i: (i, 0))],
        out_specs=pl.BlockSpec((TM, d), out_index),
        out_shape=jax.ShapeDtypeStruct((out_rows, d), F32),
        scratch_shapes=[pltpu.VMEM((2, _n_slots(n_exp), d), F32), pltpu.SemaphoreType.DMA((2,))],
        compiler_params=_cparams(("arbitrary",)),
        name="moe_combine",
    )(segs, segs, yb, x1, mods, route)


def _seg_table(start, units, offset, extra=None):
    n = start.shape[0]
    cols = [start, units, offset] + ([] if extra is None else [extra])
    pad = jnp.zeros((n, LANES - sum(c.shape[1] for c in cols)), jnp.int32)
    return jnp.concatenate(cols + [pad], axis=1).reshape(n, 1, LANES)


def _route(meta, n_exp, n_tokens):
    nt = meta.shape[0]
    count8 = meta[:, 0, :n_exp].astype(jnp.int32)
    seg_start = meta[:, 1, :n_exp].astype(jnp.int32)
    per_expert = jnp.sum(count8, axis=0)
    padded = (per_expert + MOE_BM - 1) // MOE_BM * MOE_BM
    pad_end = jnp.cumsum(padded)
    pad_start = pad_end - padded
    offset = pad_start[None, :] + jnp.cumsum(count8, axis=0) - count8
    n_pad = _round_up(n_tokens * TOP_K + nt * n_exp * (SEG_ALIGN - 1), MOE_BM) + n_exp * MOE_BM
    block_row = jnp.arange(n_pad // MOE_BM, dtype=jnp.int32) * MOE_BM
    blk_e = jnp.minimum(jnp.sum(block_row[:, None] >= pad_end[None, :], axis=1), n_exp - 1).astype(jnp.int32)
    n_used = (pad_end[-1:] // MOE_BM).astype(jnp.int32)
    segs = _seg_table(seg_start, count8 // SEG_ALIGN, offset)
    gaps = _seg_table(jnp.zeros((1, n_exp), jnp.int32), ((padded - per_expert) // SEG_ALIGN)[None, :],
                      (pad_start + per_expert)[None, :], n_used[None, :])
    return segs, gaps, blk_e, n_used, n_pad


def _channel_sublayer(x1, h2, route, meta, mods, layer, experts, n_ctx, latent_only):
    b, l, d = x1.shape
    t = b * l
    n_exp = experts[0].shape[1]
    assert 3 * n_exp < LANES and MOE_BM // SEG_ALIGN <= 2 * SEG_BITS[0]
    route = route.reshape(t, LANES)
    segs, gaps, blk_e, n_used, n_pad = _route(meta.reshape(t // TM, 8, LANES), n_exp, t)
    xb = _dispatch(h2.reshape(t, d), route, segs, gaps, n_pad, n_exp)
    yb = _expert_blocks(xb, blk_e, n_used, layer, *experts)
    x2 = _combine(yb, segs, x1.reshape(t, d), mods, route, b, l, n_ctx, n_exp, latent_only)
    return x2.reshape(b, -1, d)


def _pad_cols(w, n):
    return jnp.pad(w, ((0, 0), (0, n - w.shape[1])))


def _round_up(n, m):
    return -(-n // m) * m


def _prep_layer_params(pr):
    d = pr['w_in'].shape[0]
    na_w = d // 2
    rw_w = d // 4
    gd_w = d - na_w - rw_w
    na_cols = 3 * na_w
    rw_cols = 3 * rw_w + RW_DECAY_LORA + RW_ICLR_LORA + RW_GATE_LORA
    rw_pad = _round_up(rw_cols, LANES)
    gd_cols = pr['w_in'].shape[1] - na_cols - rw_cols
    gd_pad = _round_up(gd_cols, LANES)
    w_in = pr['w_in']
    out = dict(pr)
    out['w_in'] = jnp.concatenate([w_in[:, :na_cols],
                                   _pad_cols(w_in[:, na_cols:na_cols + rw_cols], rw_pad),
                                   _pad_cols(w_in[:, na_cols + rw_cols:], gd_pad)], axis=1).astype(BF16)
    out['rw_w'] = rw_pad
    out['gd_w'] = gd_pad
    n_heads = na_w // HEAD_DIM
    out['qkg'] = jnp.stack([jnp.tile(pr['na_q_gain'], n_heads) * HEAD_DIM ** -0.5,
                            jnp.tile(pr['na_k_gain'], n_heads)])
    head_of = np.arange(na_w) // HEAD_DIM
    out['avg'] = jnp.asarray((head_of[:, None] == head_of[None, :]) / HEAD_DIM, BF16)
    out['norm_mix_g'] = pr['norm_mix_g'].reshape(1, d)
    out['norm_ffn_g'] = pr['norm_ffn_g'].reshape(1, d)
    out['rw_mu_p'] = jnp.pad(pr['rw_mu'], (0, rw_pad - rw_cols)).reshape(1, rw_pad)
    o1, o2 = RW_DECAY_LORA, RW_DECAY_LORA + RW_ICLR_LORA
    out['rw_wup_p'] = jnp.pad(pr['rw_w_up'], ((0, 0), (0, LANES - o1), (0, 0)))
    out['rw_aup_p'] = jnp.pad(pr['rw_a_up'], ((0, 0), (o1, LANES - o2), (0, 0)))
    out['rw_gup_p'] = jnp.pad(pr['rw_g_up'], ((o2, LANES - o2 - RW_GATE_LORA), (0, 0)))
    head_of = np.arange(rw_w) // HEAD_DIM
    out['ones_bd'] = jnp.asarray(head_of[:, None] == head_of[None, :], BF16)
    out['gd_width'] = gd_w
    n_ab = pr['gd_A_log'].size
    out['gd_alog_p'] = jnp.pad(pr['gd_A_log'].reshape(-1), (0, LANES - n_ab)).reshape(1, LANES)
    out['gd_dtb_p'] = jnp.pad(pr['gd_dt_bias'].reshape(-1), (0, LANES - n_ab)).reshape(1, LANES)
    out['w_out_bf'] = pr['w_out'].astype(BF16)
    n_exp = pr['moe_router_w'].shape[1]
    router_w = _pad_cols(pr['moe_router_w'], LANES)
    out['router_w_hi'] = router_w.astype(BF16)
    out['router_w_lo'] = (router_w - out['router_w_hi'].astype(F32)).astype(BF16)
    out['router_b_p'] = jnp.concatenate([pr['moe_router_b'],
                                         jnp.full((LANES - n_exp,), MASKED, F32)]).reshape(1, LANES)
    return out


def _mixing_sublayer(xa, mods, pr, n_ctx):
    b, l, d = xa.shape
    na_w = d // 2
    p_na, p_rw, p_gd = _inproj(xa, mods, pr['norm_mix_g'], pr['w_in'], pr['qkg'], pr['avg'], n_ctx,
                               na_w, pr['rw_w'], pr['gd_w'])
    o_na = _na_attention(p_na, pr['na_bias'], n_ctx, na_w)
    rw2, ry1, rm, rn, bv, gate = _rwkv_prepare(p_rw, pr, n_ctx)
    ry = _scan(rw2, ry1, rm, rn, n_ctx)
    gy = _scan(*_gdn_prepare(p_gd, pr, n_ctx), n_ctx)
    return _outproj(xa, mods, o_na, ry, bv, gate, gy, p_gd, pr, n_ctx)


_LAYER_PARAMS = ('norm_mix_g', 'norm_ffn_g', 'w_in', 'w_out', 'na_q_gain', 'na_k_gain', 'na_rpb',
                 'rw_mu', 'rw_w0', 'rw_w_up', 'rw_a0', 'rw_a_up', 'rw_g_up', 'rw_k_k', 'rw_k_a', 'rw_r_k',
                 'rw_ln_w', 'rw_ln_b', 'gd_conv_w', 'gd_A_log', 'gd_dt_bias', 'gd_norm_g',
                 'moe_router_w', 'moe_router_b', 'moe_w_gu', 'moe_b_gu', 'moe_w_down', 'moe_b_down')


def kernel(x, c, ctx, c_ctx, ada_w, ada_b, norm_mix_g, norm_ffn_g, w_in, w_out, na_q_gain, na_k_gain, na_rpb, rw_mu, rw_w0, rw_w_up, rw_a0, rw_a_up, rw_g_up, rw_k_k, rw_k_a, rw_r_k, rw_ln_w, rw_ln_b, gd_conv_w, gd_A_log, gd_dt_bias, gd_norm_g, moe_router_w, moe_router_b, moe_w_gu, moe_b_gu, moe_w_down, moe_b_down):
    stacked = dict(zip(_LAYER_PARAMS, (norm_mix_g, norm_ffn_g, w_in, w_out, na_q_gain, na_k_gain, na_rpb,
                                       rw_mu, rw_w0, rw_w_up, rw_a0, rw_a_up, rw_g_up, rw_k_k, rw_k_a, rw_r_k,
                                       rw_ln_w, rw_ln_b, gd_conv_w, gd_A_log, gd_dt_bias, gd_norm_g,
                                       moe_router_w, moe_router_b, moe_w_gu, moe_b_gu, moe_w_down, moe_b_down)))
    b, s, d = x.shape
    n_ctx = ctx.shape[1]
    depth = ada_w.shape[0]
    assert b + 1 <= 8 and n_ctx % TM == 0 and s % TM == 0 and s % GRID_W == 0
    cvec = jnp.zeros((8, d), F32).at[:b].set(c).at[b].set(c_ctx)
    mods_all = _adaln(cvec, ada_w, ada_b).reshape(depth, 8, 6, d)[:, :b + 1]
    xa = jnp.concatenate([ctx, x], axis=1)
    experts = (moe_w_gu, moe_b_gu, moe_w_down, moe_b_down)
    for layer in range(depth):
        pr = _prep_layer_params({name: value[layer] for name, value in stacked.items()
                                 if not name.startswith('moe_w_') and not name.startswith('moe_b_')})
        pr['na_bias'] = _na_bias_tables(pr['na_rpb'], s // GRID_W)
        mods = mods_all[layer]
        x1, h2, route, meta = _mixing_sublayer(xa, mods, pr, n_ctx)
        xa = _channel_sublayer(x1, h2, route, meta, mods, layer, experts, n_ctx,
                               latent_only=(layer == depth - 1))
    return xa
```

```python
import functools
import math

import numpy as np
import jax
import jax.numpy as jnp
from jax import lax
from jax.experimental import pallas as pl
from jax.experimental.pallas import tpu as pltpu

F32 = jnp.float32
BF16 = jnp.bfloat16
HIGHEST = lax.Precision.HIGHEST

GRID_W = 64
NORM_EPS = 1e-6
HEAD_DIM = 64
NA_WIN_ROWS = 8
NA_WIN_COLS = 16
RW_DECAY_LORA = 32
RW_ICLR_LORA = 32
RW_GATE_LORA = 64
RW_GN_EPS = 64e-5
GD_CONV = 5
TOP_K = 4
SWIGLU_LIMIT = 7.0
SWIGLU_ALPHA = 1.702

LANES = 128
TM = 256
NA_QROWS = 4
NA_KROWS = NA_QROWS + NA_WIN_ROWS
CHUNK = 64
CHUNKS_PER_STEP = 4
MASKED = -1e30
VMEM_LIMIT = 48 * 1024 * 1024


def _cparams(sem):
    return pltpu.CompilerParams(dimension_semantics=sem, vmem_limit_bytes=VMEM_LIMIT)


def _dot(a, b, **kw):
    return jnp.dot(a, b, preferred_element_type=F32, **kw)


def _dot_split(a, b_bf16):
    hi = a.astype(BF16)
    lo = (a - hi.astype(F32)).astype(BF16)
    return _dot(hi, b_bf16) + _dot(lo, b_bf16)


def _dot_nt(a, b, **kw):
    return lax.dot_general(a, b, (((1,), (1,)), ((), ())), preferred_element_type=F32, **kw)


def _dot_tn(a, b, **kw):
    return lax.dot_general(a, b, (((0,), (0,)), ((), ())), preferred_element_type=F32, **kw)


def _adaln_kernel(c_ref, w_ref, b_ref, o_ref):
    c = c_ref[...]
    s = c * jax.nn.sigmoid(c)
    o_ref[0] = _dot(s.astype(BF16), w_ref[0].astype(BF16)) + b_ref[0]


def _adaln(cvec, ada_w, ada_b):
    depth, d, n = ada_w.shape
    tn = n // 4
    return pl.pallas_call(
        _adaln_kernel,
        grid=(depth, n // tn),
        in_specs=[pl.BlockSpec((8, d), lambda l, j: (0, 0)),
                  pl.BlockSpec((1, d, tn), lambda l, j: (l, 0, j)),
                  pl.BlockSpec((1, 1, tn), lambda l, j: (l, 0, j))],
        out_specs=pl.BlockSpec((1, 8, tn), lambda l, j: (l, 0, j)),
        out_shape=jax.ShapeDtypeStruct((depth, 8, n), F32),
        compiler_params=_cparams(("arbitrary", "arbitrary")),
        name="adaln",
    )(cvec, ada_w, ada_b.reshape(depth, 1, n))


def _mod_index(n_ctx_tiles, n_batch):
    return lambda b, t: (jnp.where(t < n_ctx_tiles, n_batch, b), 0, 0)


def _rms(x):
    return x * lax.rsqrt(jnp.mean(x * x, axis=-1, keepdims=True) + NORM_EPS)


def _inproj_kernel(x_ref, mod_ref, g_ref, w_ref, qkg_ref, avg_ref, na_ref, rw_ref, gd_ref, *, na_w, rw_w):
    mod = mod_ref[0]
    h = _rms(x_ref[0]) * g_ref[...]
    hb = (h * (1.0 + mod[1:2]) + mod[0:1]).astype(BF16)
    pa = _dot(hb, w_ref[:, 0:3 * na_w])
    avg = avg_ref[...]
    qkg = qkg_ref[...]
    q = pa[:, 0:na_w]
    k = pa[:, na_w:2 * na_w]
    qn = q * lax.rsqrt(_dot(q * q, avg) + NORM_EPS) * qkg[0:1]
    kn = k * lax.rsqrt(_dot(k * k, avg) + NORM_EPS) * qkg[1:2]
    na_ref[0, :, 0:na_w] = qn.astype(BF16)
    na_ref[0, :, na_w:2 * na_w] = kn.astype(BF16)
    na_ref[0, :, 2 * na_w:3 * na_w] = pa[:, 2 * na_w:3 * na_w].astype(BF16)
    rw_ref[0] = _dot(hb, w_ref[:, 3 * na_w:3 * na_w + rw_w])
    gd_ref[0] = _dot(hb, w_ref[:, 3 * na_w + rw_w:])


def _inproj(xa, mods, g, w_all, qkg, avg, n_ctx, na_w, rw_w, gd_w):
    b, l, d = xa.shape
    nt = l // TM
    kern = functools.partial(_inproj_kernel, na_w=na_w, rw_w=rw_w)
    return pl.pallas_call(
        kern,
        grid=(b, nt),
        in_specs=[pl.BlockSpec((1, TM, d), lambda i, t: (i, t, 0)),
                  pl.BlockSpec((1, 6, d), _mod_index(n_ctx // TM, b)),
                  pl.BlockSpec((1, d), lambda i, t: (0, 0)),
                  pl.BlockSpec(w_all.shape, lambda i, t: (0, 0)),
                  pl.BlockSpec(qkg.shape, lambda i, t: (0, 0)),
                  pl.BlockSpec(avg.shape, lambda i, t: (0, 0))],
        out_specs=[pl.BlockSpec((1, TM, 3 * na_w), lambda i, t: (i, t, 0)),
                   pl.BlockSpec((1, TM, rw_w), lambda i, t: (i, t, 0)),
                   pl.BlockSpec((1, TM, gd_w), lambda i, t: (i, t, 0))],
        out_shape=[jax.ShapeDtypeStruct((b, l, 3 * na_w), BF16),
                   jax.ShapeDtypeStruct((b, l, rw_w), F32),
                   jax.ShapeDtypeStruct((b, l, gd_w), F32)],
        compiler_params=_cparams(("arbitrary", "arbitrary")),
        name="inproj",
    )(xa, mods, g, w_all, qkg, avg)


def _na_bias_tables(rpb, rows):
    n_heads = rpb.shape[0]
    nq, nk = NA_QROWS * GRID_W, NA_KROWS * GRID_W
    qc = np.arange(GRID_W)[:, None]
    kc = np.arange(GRID_W)[None, :]
    c0 = np.clip(qc - NA_WIN_COLS // 2, 0, GRID_W - NA_WIN_COLS)
    col_ok = (kc >= c0) & (kc < c0 + NA_WIN_COLS)
    dc = kc - qc + NA_WIN_COLS - 1
    pick_col = (dc[None] == np.arange(2 * NA_WIN_COLS - 1)[:, None, None]).astype(np.float32)
    by_col = jnp.einsum('hrd,dqk->hrqk', rpb, pick_col, precision=HIGHEST)
    tables = []
    for r0 in (0, NA_QROWS, rows - NA_QROWS):
        ks = int(np.clip(r0 - NA_WIN_ROWS // 2, 0, rows - NA_KROWS))
        qr = (r0 + np.arange(NA_QROWS))[:, None]
        kr = (ks + np.arange(NA_KROWS))[None, :]
        s0 = np.clip(qr - NA_WIN_ROWS // 2, 0, rows - NA_WIN_ROWS)
        row_ok = (kr >= s0) & (kr < s0 + NA_WIN_ROWS)
        dr = kr - qr + NA_WIN_ROWS - 1
        pick_row = ((dr[None] == np.arange(2 * NA_WIN_ROWS - 1)[:, None, None]) & row_ok[None]).astype(np.float32)
        bias = jnp.einsum('rab,hrqk->haqbk', pick_row, by_col, precision=HIGHEST)
        ok = row_ok[:, None, :, None] & col_ok[None, :, None, :]
        tables.append(jnp.where(ok[None], bias, MASKED).reshape(n_heads, nq, nk))
    tables.append(jnp.full((n_heads, nq, nk), MASKED, F32))
    return jnp.stack(tables).reshape(4, n_heads // 2, 2, nq, nk)


def _na_kernel(q_ref, k_ref, v_ref, bias_ref, o_ref, *, n_ctx, rows):
    j = pl.program_id(2)
    nk = NA_KROWS * GRID_W
    ks_row = jnp.clip((j - 1) * NA_QROWS - NA_WIN_ROWS // 2, 0, rows - NA_KROWS)
    kstart = pl.multiple_of(n_ctx + ks_row * GRID_W, GRID_W)
    q = q_ref[0]
    kl = k_ref[0, pl.ds(kstart, nk), :]
    vl = v_ref[0, pl.ds(kstart, nk), :]
    kc = k_ref[0, 0:n_ctx, :]
    vc = v_ref[0, 0:n_ctx, :]
    lane = lax.broadcasted_iota(jnp.int32, q.shape, 1)
    outs = []
    for h in range(2):
        in_head = (lane >= h * HEAD_DIM) & (lane < (h + 1) * HEAD_DIM)
        qh = jnp.where(in_head, q, jnp.zeros_like(q))
        s_loc = _dot_nt(qh, kl) + bias_ref[0, 0, h]
        s_ctx = _dot_nt(qh, kc)
        m = jnp.maximum(jnp.max(s_loc, axis=-1, keepdims=True), jnp.max(s_ctx, axis=-1, keepdims=True))
        p_loc = jnp.exp(s_loc - m)
        p_ctx = jnp.exp(s_ctx - m)
        den = jnp.sum(p_loc, axis=-1, keepdims=True) + jnp.sum(p_ctx, axis=-1, keepdims=True)
        o = _dot(p_loc.astype(BF16), vl) + _dot(p_ctx.astype(BF16), vc)
        outs.append(o / den)
    o_ref[0] = jnp.where(lane < HEAD_DIM, outs[0], outs[1]).astype(o_ref.dtype)


def _na_attention(p_na, bias, n_ctx, na_w):
    b, l, _ = p_na.shape
    rows = (l - n_ctx) // GRID_W
    nq = NA_QROWS * GRID_W
    assert n_ctx == nq and rows >= NA_KROWS and rows % NA_QROWS == 0
    n_pairs = na_w // LANES
    nblk = l // nq
    kern = functools.partial(_na_kernel, n_ctx=n_ctx, rows=rows)

    def bias_index(i, hp, j):
        pat = jnp.where(j == 0, 3, jnp.where(j == 1, 0, jnp.where(j == nblk - 1, 2, 1)))
        return (pat, hp, 0, 0, 0)

    return pl.pallas_call(
        kern,
        grid=(b, n_pairs, nblk),
        in_specs=[pl.BlockSpec((1, nq, LANES), lambda i, hp, j: (i, j, hp)),
                  pl.BlockSpec((1, l, LANES), lambda i, hp, j: (i, 0, n_pairs + hp)),
                  pl.BlockSpec((1, l, LANES), lambda i, hp, j: (i, 0, 2 * n_pairs + hp)),
                  pl.BlockSpec((1, 1, 2) + bias.shape[3:], bias_index)],
        out_specs=pl.BlockSpec((1, nq, LANES), lambda i, hp, j: (i, j, hp)),
        out_shape=jax.ShapeDtypeStruct((b, l, na_w), BF16),
        compiler_params=_cparams(("arbitrary", "arbitrary", "arbitrary")),
        name="na_attention",
    )(p_na, p_na, p_na, bias)


def _tri_masks(reverse):
    t = lax.broadcasted_iota(jnp.int32, (CHUNK, CHUNK), 0)
    s = lax.broadcasted_iota(jnp.int32, (CHUNK, CHUNK), 1)
    earlier_or_same = (s >= t) if reverse else (s <= t)
    earlier = (s > t) if reverse else (s < t)
    return earlier_or_same, earlier, t == s


def _each(fn, *lists):
    return [fn(*args) for args in zip(*lists)]


def _bf16(values):
    return [u.astype(BF16) for u in values]


def _head_lanes(width):
    lane = lax.broadcasted_iota(jnp.int32, (1, width), 1)
    return [(lane >= h * HEAD_DIM) & (lane < (h + 1) * HEAD_DIM) for h in range(width // HEAD_DIM)]


def _stack_heads(x, heads):
    return jnp.concatenate([jnp.where(in_head, x, 0.0) for in_head in heads], axis=0).astype(BF16)


def _tiled_masks(n_heads, reverse):
    t = lax.broadcasted_iota(jnp.int32, (CHUNK, n_heads * CHUNK), 0)
    s = lax.broadcasted_iota(jnp.int32, (CHUNK, n_heads * CHUNK), 1) % CHUNK
    earlier = (s > t) if reverse else (s < t)
    return earlier, earlier | (s == t), s == t


def _unit_tri_inverse(a, eye, heads):
    assert CHUNK == 64
    diag = lambda values: [_stack_heads(u, heads) for u in values]
    x1 = [-u for u in a]
    x1b, x1d = _bf16(x1), diag(x1)
    x2 = _each(_dot, x1b, x1d)
    x2b, x2d = _bf16(x2), diag(x2)
    x4 = _each(_dot, x2b, x2d)
    x3 = _each(_dot, x1b, x2d)
    x4b, x4d = _bf16(x4), diag(x4)
    x8 = _each(_dot, x4b, x4d)
    p1 = _each(lambda u1, u2, u3: eye + u1 + u2 + u3, x1, x2, x3)
    x8b, x8d = _bf16(x8), diag(x8)
    x12 = _each(_dot, x4b, x8d)
    x16 = _each(_dot, x8b, x8d)
    p2 = _each(lambda u4, u8, u12: eye + u4 + u8 + u12, x4, x8, x12)
    p12 = _each(_dot, _bf16(p1), diag(p2))
    x16b, x16d = _bf16(x16), diag(x16)
    x32 = _each(_dot, x16b, x16d)
    x48 = _each(_dot, x16b, diag(x32))
    p3 = _each(lambda u16, u32, u48: eye + u16 + u32 + u48, x16, x32, x48)
    return _each(_dot, _bf16(p12), diag(p3))


def _chunk_matrices(a_ka, a_kk, a_ra, a_rk, ks, rt, v, kd, ad, eye, heads):
    stack = lambda values: [_stack_heads(u, heads) for u in values]
    t_inv = _bf16(_unit_tri_inverse(a_ka, eye.astype(F32), heads))
    vs, a_rab, adb = stack(v), _bf16(a_ra), _bf16(ad)
    av = _each(_dot, _bf16(a_kk), vs)
    w1 = _each(_dot, t_inv, ks)
    u0 = _each(_dot, t_inv, stack(av))
    rkv = _each(_dot, _bf16(a_rk), vs)
    w2 = _each(lambda r, a, w: r - _dot(a, w), rt, a_rab, stack(w1))
    y1 = _each(lambda y, a, u: y - _dot(a, u), rkv, a_rab, stack(u0))
    mm = _each(_dot_tn, adb, _bf16(w1))
    nn = _each(lambda kd_, v_, ad_, u: _dot_tn(kd_, v_) - _dot_tn(ad_, u), _bf16(kd), _bf16(v), adb, _bf16(u0))
    return list(zip(w2, y1, mm, nn))


def _store_chunk(refs, d, c0, n_heads, w2, y1, mm, nn, p_end):
    w2_ref, y1_ref, m_ref, n_ref = refs
    rows = pl.ds(c0, CHUNK)
    eye = (lax.broadcasted_iota(jnp.int32, (HEAD_DIM, HEAD_DIM), 0)
           == lax.broadcasted_iota(jnp.int32, (HEAD_DIM, HEAD_DIM), 1))
    for h in range(n_heads):
        lo, hi = h * HEAD_DIM, (h + 1) * HEAD_DIM
        w2_ref[d, 0, h, rows, :] = w2[:, lo:hi]
        y1_ref[d, 0, h, rows, :] = y1[:, lo:hi]
        m_ref[d, 0, h, rows, :] = jnp.where(eye, p_end[:, lo:hi], 0.0) - mm[lo:hi, lo:hi]
        n_ref[d, 0, h, rows, :] = nn[lo:hi, lo:hi]


def _scan_out_specs(b, n_heads, l):
    shape = jax.ShapeDtypeStruct((2, b, n_heads, l, HEAD_DIM), F32)
    spec = pl.BlockSpec((2, 1, n_heads, TM, HEAD_DIM), lambda i, t: (0, i, 0, t, 0))
    return [shape] * 4, [spec] * 4


def _seq_edges(t, n_ctx_tiles, n_tiles):
    first = (t == 0) | (t == n_ctx_tiles)
    last = (t == n_ctx_tiles - 1) | (t == n_tiles - 1)
    return first, last


def _fill_halo(ext_ref, x_ref, prev_ref, next_ref, first, last):
    ext_ref[0:8, :] = jnp.where(first, 0.0, prev_ref[0])
    ext_ref[8:8 + TM, :] = x_ref[0]
    ext_ref[8 + TM:16 + TM, :] = jnp.where(last, 0.0, next_ref[0])


def _halo_specs(width, n_tiles):
    per = TM // 8
    return [pl.BlockSpec((1, TM, width), lambda i, t: (i, t, 0)),
            pl.BlockSpec((1, 8, width), lambda i, t: (i, jnp.maximum(t * per - 1, 0), 0)),
            pl.BlockSpec((1, 8, width), lambda i, t: (i, jnp.minimum((t + 1) * per, n_tiles * per - 1), 0))]


def _full_spec(a):
    return pl.BlockSpec(a.shape, lambda i, t: (0,) * a.ndim)


def _rwkv_prep_kernel(x_ref, prev_ref, next_ref, mu_ref, w0_ref, wup_ref, a0_ref, aup_ref, gup_ref,
                      kk_ref, ka_ref, rk_ref, ones_ref,
                      w2_ref, y1_ref, m_ref, n_ref, bv_ref, gate_ref,
                      ext_ref, tok_ref, *, n_ctx_tiles, n_tiles, width):
    t = pl.program_id(1)
    first, last = _seq_edges(t, n_ctx_tiles, n_tiles)
    _fill_halo(ext_ref, x_ref, prev_ref, next_ref, first, last)
    p = ext_ref[8:8 + TM, :]
    prev = ext_ref[7:7 + TM, :]
    nxt = ext_ref[9:9 + TM, :]
    ps = p + mu_ref[...] * (0.5 * (prev + nxt) - p)
    r = ps[:, 0:width]
    k = ps[:, width:2 * width]
    v = ps[:, 2 * width:3 * width]
    lo = ps[:, 3 * width:3 * width + LANES]
    ones_bd = ones_ref[...]
    gate_ref[0] = _dot(jax.nn.sigmoid(lo), gup_ref[...])
    kq = k * kk_ref[...]
    kk = kq * lax.rsqrt(_dot_split(kq * kq, ones_bd) + NORM_EPS)
    tanh_lo = jnp.tanh(lo)
    tok_ref[0] = r
    tok_ref[1] = kk
    tok_ref[2] = v
    k_sum = jnp.zeros_like(k)
    for d in range(2):
        z = w0_ref[d:d + 1, :] + _dot(tanh_lo, wup_ref[d])
        w_log = -jax.nn.softplus(-z) - 0.5
        a = jax.nn.sigmoid(a0_ref[d:d + 1, :] + _dot(lo, aup_ref[d]))
        k_dir = k * (1.0 + (a - 1.0) * ka_ref[...])
        k_sum = k_sum + k_dir
        tok_ref[3 + 3 * d] = -jnp.exp(w_log)
        tok_ref[4 + 3 * d] = kk * a
        tok_ref[5 + 3 * d] = k_dir
    bonus = _dot_split(r * k_sum * rk_ref[...], ones_bd)
    bv_ref[0] = bonus * v

    heads = _head_lanes(width)
    out_refs = (w2_ref, y1_ref, m_ref, n_ref)

    def chunk_body(c, carry):
        args = [[] for _ in range(9)]
        where = []
        for cc in range(CHUNKS_PER_STEP):
            c0 = pl.multiple_of((c * CHUNKS_PER_STEP + cc) * CHUNK, CHUNK)
            rows = pl.ds(c0, CHUNK)
            r_c, kk_c, v_c = tok_ref[0, rows, :], tok_ref[1, rows, :], tok_ref[2, rows, :]
            for d in range(2):
                logw, alpha, k_c = (tok_ref[3 + 3 * d, rows, :], tok_ref[4 + 3 * d, rows, :],
                                    tok_ref[5 + 3 * d, rows, :])
                incl, _, _ = _tri_masks(reverse=(d == 1))
                ci = _dot(incl.astype(F32), logw, precision=HIGHEST)
                e_ci, e_nci, e_ce = jnp.exp(ci), jnp.exp(-ci), jnp.exp(ci - logw)
                p_end = e_ci[0:1, :] if d == 1 else e_ci[CHUNK - 1:CHUNK, :]
                kt, rt, kh, ah = kk_c * e_ce, r_c * e_ci, k_c * e_nci, alpha * e_nci
                earlier, earlier_or_same, eye = _tiled_masks(len(heads), reverse=(d == 1))
                khs, ahs = _stack_heads(kh, heads), _stack_heads(ah, heads)
                ktb, rtb = kt.astype(BF16), rt.astype(BF16)
                a_ka = jnp.where(earlier, _dot_nt(ktb, ahs), 0.0)
                a_kk = jnp.where(earlier, _dot_nt(ktb, khs), 0.0)
                a_ra = jnp.where(earlier_or_same, _dot_nt(rtb, ahs), 0.0)
                a_rk = jnp.where(earlier_or_same, _dot_nt(rtb, khs), 0.0)
                values = (a_ka, a_kk, a_ra, a_rk, _stack_heads(kt, heads), rt, v_c, kh * p_end, ah * p_end)
                for slot, value in zip(args, values):
                    slot.append(value)
                where.append((d, c0, p_end))
        for (d, c0, p_end), res in zip(where, _chunk_matrices(*args, eye, heads)):
            _store_chunk(out_refs, d, c0, len(heads), *res, p_end)
        return carry

    lax.fori_loop(0, TM // CHUNK // CHUNKS_PER_STEP, chunk_body, 0)


def _rwkv_prepare(p_rw, pr, n_ctx):
    b, l, pw = p_rw.shape
    width = pr['rw_k_k'].shape[-1]
    n_heads = width // HEAD_DIM
    nt = l // TM
    kern = functools.partial(_rwkv_prep_kernel, n_ctx_tiles=n_ctx // TM, n_tiles=nt, width=width)
    params = [pr['rw_mu_p'], pr['rw_w0'], pr['rw_wup_p'], pr['rw_a0'], pr['rw_aup_p'], pr['rw_gup_p'],
              pr['rw_k_k'].reshape(1, width), pr['rw_k_a'].reshape(1, width), pr['rw_r_k'].reshape(1, width),
              pr['ones_bd']]
    scan_shapes, scan_specs = _scan_out_specs(b, n_heads, l)
    tok_spec = pl.BlockSpec((1, TM, width), lambda i, t: (i, t, 0))
    return pl.pallas_call(
        kern,
        grid=(b, nt),
        in_specs=_halo_specs(pw, nt) + [_full_spec(a) for a in params],
        out_specs=scan_specs + [tok_spec, tok_spec],
        out_shape=scan_shapes + [jax.ShapeDtypeStruct((b, l, width), F32)] * 2,
        scratch_shapes=[pltpu.VMEM((TM + 16, pw), F32), pltpu.VMEM((9, TM, width), F32)],
        compiler_params=_cparams(("arbitrary", "arbitrary")),
        name="rwkv_prepare",
    )(p_rw, p_rw, p_rw, *params)


def _scan_kernel(w2f_ref, y1f_ref, mf_ref, nf_ref, w2b_ref, y1b_ref, mb_ref, nb_ref, yf_ref, yb_ref, h_ref):
    @pl.when(pl.program_id(0) == 0)
    def _():
        h_ref[...] = jnp.zeros_like(h_ref)

    n_batch, n_heads = h_ref.shape[1], h_ref.shape[2]
    n_sub = yf_ref.shape[2] // CHUNK
    dirs = ((w2f_ref, y1f_ref, mf_ref, nf_ref, yf_ref), (w2b_ref, y1b_ref, mb_ref, nb_ref, yb_ref))
    chains = [(d, i, h) for d in range(2) for i in range(n_batch) for h in range(n_heads)]
    states = [h_ref[d, i, h] for d, i, h in chains]
    for j in range(n_sub):
        for c, (d, i, h) in enumerate(chains):
            w2_ref, y1_ref, m_ref, n_ref, y_ref = dirs[d]
            jj = j if d == 0 else n_sub - 1 - j
            rows = slice(jj * CHUNK, (jj + 1) * CHUNK)
            y_ref[i, h, rows, :] = _dot(w2_ref[0, i, h, rows, :], states[c]) + y1_ref[0, i, h, rows, :]
            states[c] = _dot(m_ref[0, i, h, rows, :], states[c], precision=HIGHEST) + n_ref[0, i, h, rows, :]
    for c, (d, i, h) in enumerate(chains):
        h_ref[d, i, h] = states[c]


def _scan(w2, y1, m, n, n_ctx):
    _, b, n_heads, l, _ = w2.shape
    nc = l // TM
    ncc = n_ctx // TM

    def bwd_chunk(i):
        return jnp.where(i < ncc, ncc - 1 - i, nc - 1 + ncc - i)

    blk = (1, b, n_heads, TM, HEAD_DIM)
    fwd = pl.BlockSpec(blk, lambda i: (0, 0, 0, i, 0))
    bwd = pl.BlockSpec(blk, lambda i: (1, 0, 0, bwd_chunk(i), 0))
    out_shape = jax.ShapeDtypeStruct((b, n_heads, l, HEAD_DIM), F32)
    return pl.pallas_call(
        _scan_kernel,
        grid=(nc,),
        in_specs=[fwd] * 4 + [bwd] * 4,
        out_specs=[pl.BlockSpec(blk[1:], lambda i: (0, 0, i, 0)),
                   pl.BlockSpec(blk[1:], lambda i: (0, 0, bwd_chunk(i), 0))],
        out_shape=[out_shape, out_shape],
        scratch_shapes=[pltpu.VMEM((2, b, n_heads, HEAD_DIM, HEAD_DIM), F32)],
        compiler_params=_cparams(("arbitrary",)),
        name="chunk_scan",
    )(w2, y1, m, n, w2, y1, m, n)


def _gdn_prep_kernel(x_ref, prev_ref, next_ref, cw_ref, alog_ref, dtb_ref, ones_ref,
                     w2_ref, y1_ref, m_ref, n_ref,
                     ext_ref, tok_ref, *, n_ctx_tiles, n_tiles, width):
    t = pl.program_id(1)
    first, last = _seq_edges(t, n_ctx_tiles, n_tiles)
    _fill_halo(ext_ref, x_ref, prev_ref, next_ref, first, last)
    half = GD_CONV // 2
    conv = jnp.zeros((TM, 3 * width), F32)
    for j in range(GD_CONV):
        conv = conv + cw_ref[j:j + 1, :] * ext_ref[8 + j - half:8 + j - half + TM, 0:3 * width]
    qkv = conv * jax.nn.sigmoid(conv)
    ones_bd = ones_ref[...]

    def l2n(u):
        return u * lax.rsqrt(_dot_split(u * u, ones_bd) + NORM_EPS)

    q = l2n(qkv[:, 0:width]) * HEAD_DIM ** -0.5
    k = l2n(qkv[:, width:2 * width])
    v = qkv[:, 2 * width:3 * width]
    ab = ext_ref[8:8 + TM, 4 * width:4 * width + LANES]
    g_all = -jnp.exp(alog_ref[...]) * jax.nn.softplus(ab + dtb_ref[...])
    beta_all = jax.nn.sigmoid(ab)
    heads = _head_lanes(width)
    n_heads = len(heads)
    tok_ref[0] = q
    tok_ref[1] = k
    tok_ref[2] = v
    for d in range(2):
        g = jnp.zeros((TM, width), F32)
        beta = jnp.zeros((TM, width), F32)
        for h, in_head in enumerate(heads):
            ja = d * n_heads + h
            jb = 2 * n_heads + ja
            g = jnp.where(in_head, g_all[:, ja:ja + 1], g)
            beta = jnp.where(in_head, beta_all[:, jb:jb + 1], beta)
        tok_ref[3 + 2 * d] = g
        tok_ref[4 + 2 * d] = k * beta
    out_refs = (w2_ref, y1_ref, m_ref, n_ref)

    def chunk_inputs(c0, d):
        rows = pl.ds(c0, CHUNK)
        q_c, k_c, v_c = tok_ref[0, rows, :], tok_ref[1, rows, :], tok_ref[2, rows, :]
        g, kb = tok_ref[3 + 2 * d, rows, :], tok_ref[4 + 2 * d, rows, :]
        incl, _, _ = _tri_masks(reverse=(d == 1))
        ci = _dot(incl.astype(F32), g, precision=HIGHEST)
        ce = ci - g
        ci_end = ci[0:1, :] if d == 1 else ci[CHUNK - 1:CHUNK, :]
        earlier, earlier_or_same, eye = _tiled_masks(n_heads, reverse=(d == 1))
        ci_row = jnp.sum(jnp.where(eye, ci, 0.0), axis=0, keepdims=True)
        ce_row = jnp.sum(jnp.where(eye, ce, 0.0), axis=0, keepdims=True)
        kbs = _stack_heads(kb, heads)
        kkb = _dot_nt(k_c.astype(BF16), kbs)
        qkb = _dot_nt(q_c.astype(BF16), kbs)
        a_ka = kkb * jnp.exp(jnp.where(earlier, ce - ce_row, MASKED))
        a_kk = kkb * jnp.exp(jnp.where(earlier, ce - ci_row, MASKED))
        a_ra = qkb * jnp.exp(jnp.where(earlier_or_same, ci - ce_row, MASKED))
        a_rk = qkb * jnp.exp(jnp.where(earlier_or_same, ci - ci_row, MASKED))
        values = (a_ka, a_kk, a_ra, a_rk, _stack_heads(k_c * jnp.exp(ce), heads), q_c * jnp.exp(ci), v_c,
                  kb * jnp.exp(ci_end - ci), kb * jnp.exp(ci_end - ce))
        return values, jnp.exp(ci_end), eye

    def chunk_body(c, carry):
        args = [[] for _ in range(9)]
        where = []
        for cc in range(CHUNKS_PER_STEP):
            c0 = pl.multiple_of((c * CHUNKS_PER_STEP + cc) * CHUNK, CHUNK)
            for d in range(2):
                values, p_end, eye = chunk_inputs(c0, d)
                for slot, value in zip(args, values):
                    slot.append(value)
                where.append((d, c0, p_end))
        for (d, c0, p_end), res in zip(where, _chunk_matrices(*args, eye, heads)):
            _store_chunk(out_refs, d, c0, n_heads, *res, p_end)
        return carry

    lax.fori_loop(0, TM // CHUNK // CHUNKS_PER_STEP, chunk_body, 0)


def _gdn_prepare(p_gd, pr, n_ctx):
    b, l, pw = p_gd.shape
    width = pr['gd_width']
    n_heads = width // HEAD_DIM
    nt = l // TM
    kern = functools.partial(_gdn_prep_kernel, n_ctx_tiles=n_ctx // TM, n_tiles=nt, width=width)
    params = [pr['gd_conv_w'], pr['gd_alog_p'], pr['gd_dtb_p'], pr['ones_bd']]
    scan_shapes, scan_specs = _scan_out_specs(b, n_heads, l)
    return pl.pallas_call(
        kern,
        grid=(b, nt),
        in_specs=_halo_specs(pw, nt) + [_full_spec(a) for a in params],
        out_specs=scan_specs,
        out_shape=scan_shapes,
        scratch_shapes=[pltpu.VMEM((TM + 16, pw), F32), pltpu.VMEM((7, TM, width), F32)],
        compiler_params=_cparams(("arbitrary", "arbitrary")),
        name="gdn_prepare",
    )(p_gd, p_gd, p_gd, *params)


def _outproj_kernel(x_ref, mod_ref, na_ref, ryf_ref, ryb_ref, bv_ref, gate_ref, gyf_ref, gyb_ref, z_ref,
                    lnw_ref, lnb_ref, gng_ref, wo_ref, g2_ref, rwh_ref, rwl_ref, rb_ref,
                    x1_ref, h2_ref, route_ref, meta_ref, *, na_w, rw_w):
    n_heads = ryf_ref.shape[1]
    rw_parts, gd_parts = [], []
    for h in range(n_heads):
        y = ryf_ref[0, h] + ryb_ref[0, h]
        yc = y - jnp.mean(y, axis=-1, keepdims=True)
        rw_parts.append(yc * lax.rsqrt(jnp.mean(yc * yc, axis=-1, keepdims=True) + RW_GN_EPS))
        gd_parts.append(_rms(gyf_ref[0, h] + gyb_ref[0, h]) * gng_ref[...])
    yn = jnp.concatenate(rw_parts, axis=-1)
    o_rw = (yn * lnw_ref[...] + lnb_ref[...] + bv_ref[0]) * gate_ref[0]
    z = z_ref[0]
    o_gd = jnp.concatenate(gd_parts, axis=-1) * (z * jax.nn.sigmoid(z))
    o = (_dot(na_ref[0], wo_ref[0:na_w, :])
         + _dot(o_rw.astype(BF16), wo_ref[na_w:na_w + rw_w, :])
         + _dot(o_gd.astype(BF16), wo_ref[na_w + rw_w:, :]))
    mod = mod_ref[0]
    x1 = x_ref[0] + mod[2:3] * o
    x1_ref[0] = x1
    h2 = _rms(x1) * g2_ref[...] * (1.0 + mod[4:5]) + mod[3:4]
    h2_ref[0] = h2
    h2_hi = h2.astype(BF16)
    h2_lo = (h2 - h2_hi.astype(F32)).astype(BF16)
    logits = (_dot(h2_hi, rwh_ref[...]) + (_dot(h2_hi, rwl_ref[...]) + _dot(h2_lo, rwh_ref[...]))) + rb_ref[...]
    lane = lax.broadcasted_iota(jnp.int32, logits.shape, 1)
    chosen, weights = [], []
    onehot = jnp.zeros(logits.shape, F32)
    for kk in range(TOP_K):
        m = jnp.max(logits, axis=-1, keepdims=True)
        idx = jnp.min(jnp.where(logits == m, lane, LANES), axis=-1, keepdims=True)
        chosen.append(idx)
        top_max = m if kk == 0 else top_max
        weights.append(jnp.exp(m - top_max))
        onehot = jnp.where(lane == idx, 1.0, onehot)
        logits = jnp.where(lane == idx, -jnp.inf, logits)
    denom = weights[0] + weights[1] + weights[2] + weights[3]
    tok_i = lax.broadcasted_iota(jnp.int32, (TM, TM), 0)
    tok_j = lax.broadcasted_iota(jnp.int32, (TM, TM), 1)
    before = jnp.where(tok_j < tok_i, 1.0, 0.0).astype(BF16)
    rank = _dot(before, onehot.astype(BF16))
    counts = jnp.sum(onehot, axis=0, keepdims=True)
    count8 = jnp.floor((counts + 7.0) * 0.125) * 8.0
    exp_i = lax.broadcasted_iota(jnp.int32, (LANES, LANES), 0)
    exp_j = lax.broadcasted_iota(jnp.int32, (LANES, LANES), 1)
    seg_start = _dot(jnp.broadcast_to(count8, (8, LANES)), jnp.where(exp_i < exp_j, 1.0, 0.0),
                     precision=HIGHEST)[0:1]
    slot_of = seg_start + rank
    route = jnp.zeros(logits.shape, F32)
    for kk in range(TOP_K):
        slot = jnp.sum(jnp.where(lane == chosen[kk], slot_of, 0.0), axis=-1, keepdims=True)
        route = jnp.where(lane == kk, weights[kk] / denom, route)
        route = jnp.where(lane == TOP_K + kk, slot, route)
    route_ref[0] = route
    row8 = lax.broadcasted_iota(jnp.int32, (8, LANES), 0)
    meta_ref[0, 0] = jnp.where(row8 == 0, count8, jnp.where(row8 == 1, seg_start, 0.0))


def _outproj(xa, mods, o_na, ry, bv, gate, gy, p_gd, pr, n_ctx):
    b, l, d = xa.shape
    nt = l // TM
    na_w = o_na.shape[-1]
    n_heads = ry[0].shape[1]
    rw_w = n_heads * HEAD_DIM
    kern = functools.partial(_outproj_kernel, na_w=na_w, rw_w=rw_w)
    tok = lambda w: pl.BlockSpec((1, TM, w), lambda i, t: (i, t, 0))
    head_major = pl.BlockSpec((1, n_heads, TM, HEAD_DIM), lambda i, t: (i, 0, t, 0))
    params = [pr['rw_ln_w'].reshape(1, rw_w), pr['rw_ln_b'].reshape(1, rw_w), pr['gd_norm_g'].reshape(1, HEAD_DIM),
              pr['w_out_bf'], pr['norm_ffn_g'], pr['router_w_hi'], pr['router_w_lo'], pr['router_b_p']]
    return pl.pallas_call(
        kern,
        grid=(b, nt),
        in_specs=[tok(d), pl.BlockSpec((1, 6, d), _mod_index(n_ctx // TM, b)), tok(na_w),
                  head_major, head_major, tok(rw_w), tok(rw_w), head_major, head_major,
                  pl.BlockSpec((1, TM, rw_w), lambda i, t: (i, t, 3))]
                 + [_full_spec(a) for a in params],
        out_specs=[tok(d), tok(d), tok(LANES), pl.BlockSpec((1, 1, 8, LANES), lambda i, t: (i, t, 0, 0))],
        out_shape=[jax.ShapeDtypeStruct((b, l, d), F32), jax.ShapeDtypeStruct((b, l, d), F32),
                   jax.ShapeDtypeStruct((b, l, LANES), F32), jax.ShapeDtypeStruct((b, nt, 8, LANES), F32)],
        compiler_params=_cparams(("arbitrary", "arbitrary")),
        name="outproj_router",
    )(xa, mods, o_na, ry[0], ry[1], bv, gate, gy[0], gy[1], p_gd, *params)


MOE_BM = 512
MOE_EXPERT_VMEM = 56 * 1024 * 1024
SEG_ALIGN = 8
SEG_BITS = (32, 16, 8, 4, 2, 1)


def _n_slots(n_exp):
    return _round_up(TOP_K * TM + n_exp * (SEG_ALIGN - 1), LANES)


def _segment_dma(segs_ref, buf_ref, hbm_ref, sem, n_exp, to_hbm, wait):
    def body(e, carry):
        start = segs_ref[0, 0, e]
        units = segs_ref[0, 0, n_exp + e]
        offset = segs_ref[0, 0, 2 * n_exp + e]
        for bit in SEG_BITS:
            done = (units & ~(2 * bit - 1)) * SEG_ALIGN
            rows = bit * SEG_ALIGN

            @pl.when((units & bit) != 0)
            def _():
                in_buf = buf_ref.at[pl.ds(pl.multiple_of(start + done, SEG_ALIGN), rows), :]
                in_hbm = hbm_ref.at[pl.ds(pl.multiple_of(offset + done, SEG_ALIGN), rows), :]
                copy = (pltpu.make_async_copy(in_buf, in_hbm, sem) if to_hbm
                        else pltpu.make_async_copy(in_hbm, in_buf, sem))
                if wait:
                    copy.wait()
                else:
                    copy.start()
        return carry
    lax.fori_loop(0, n_exp, body, 0)


def _slot_lanes(route, n_slots):
    lane = lax.broadcasted_iota(jnp.int32, (route.shape[0], n_slots), 1)
    slots = [route[:, TOP_K + k:TOP_K + k + 1].astype(jnp.int32) for k in range(TOP_K)]
    return lane, slots


def _dispatch_kernel(segs_ref, segs_prev_ref, gaps_ref, h_ref, route_ref, xb_ref, xs_ref, zero_ref, sem, gap_sem,
                     *, n_exp):
    i = pl.program_id(0)
    slot = i % 2

    @pl.when(i == 0)
    def _():
        zero_ref[...] = jnp.zeros_like(zero_ref)
        _segment_dma(gaps_ref, zero_ref, xb_ref, gap_sem, n_exp, to_hbm=True, wait=False)
        _segment_dma(gaps_ref, zero_ref, xb_ref, gap_sem, n_exp, to_hbm=True, wait=True)
        n_blocks = xb_ref.shape[0] // MOE_BM

        def block_copy(blk):
            return pltpu.make_async_copy(zero_ref, xb_ref.at[pl.ds(pl.multiple_of(blk * MOE_BM, MOE_BM), MOE_BM), :],
                                         gap_sem)

        def start(blk, carry):
            block_copy(blk).start()
            return carry

        def wait(blk, carry):
            block_copy(blk).wait()
            return carry

        lax.fori_loop(gaps_ref[0, 0, 3 * n_exp], n_blocks, start, 0)
        lax.fori_loop(gaps_ref[0, 0, 3 * n_exp], n_blocks, wait, 0)

    lane, slots = _slot_lanes(route_ref[...], xs_ref.shape[1])
    picked = lane == slots[0]
    for k in range(1, TOP_K):
        picked = picked | (lane == slots[k])
    onehot = jnp.where(picked, 1.0, 0.0).astype(BF16)
    xs_ref[slot] = _dot_tn(onehot, h_ref[...].astype(BF16))
    _segment_dma(segs_ref, xs_ref.at[slot], xb_ref, sem.at[slot], n_exp, to_hbm=True, wait=False)

    @pl.when(i > 0)
    def _():
        _segment_dma(segs_prev_ref, xs_ref.at[1 - slot], xb_ref, sem.at[1 - slot], n_exp, to_hbm=True, wait=True)

    @pl.when(i == pl.num_programs(0) - 1)
    def _():
        _segment_dma(segs_ref, xs_ref.at[slot], xb_ref, sem.at[slot], n_exp, to_hbm=True, wait=True)


def _seg_spec(index):
    return pl.BlockSpec((1, 1, LANES), index, memory_space=pltpu.SMEM)


def _dispatch(h2, route, segs, gaps, n_pad, n_exp):
    t, d = h2.shape
    nt = t // TM
    kern = functools.partial(_dispatch_kernel, n_exp=n_exp)
    return pl.pallas_call(
        kern,
        grid=(nt,),
        in_specs=[_seg_spec(lambda i: (i, 0, 0)),
                  _seg_spec(lambda i: (jnp.maximum(i - 1, 0), 0, 0)),
                  _seg_spec(lambda i: (0, 0, 0)),
                  pl.BlockSpec((TM, d), lambda i: (i, 0)),
                  pl.BlockSpec((TM, LANES), lambda i: (i, 0))],
        out_specs=pl.BlockSpec(memory_space=pl.ANY),
        out_shape=jax.ShapeDtypeStruct((n_pad, d), F32),
        scratch_shapes=[pltpu.VMEM((2, _n_slots(n_exp), d), F32), pltpu.VMEM((MOE_BM, d), F32),
                        pltpu.SemaphoreType.DMA((2,)), pltpu.SemaphoreType.DMA(())],
        compiler_params=_cparams(("arbitrary",)),
        name="moe_dispatch",
    )(segs, segs, gaps, h2, route)


def _expert_kernel(blk_e_ref, n_used_ref, x_ref, wgu_ref, bgu_ref, wd_ref, bd_ref, y_ref, wgu_bf, wd_bf):
    i = pl.program_id(0)
    used = i < n_used_ref[0]
    new_expert = (i == 0) | (blk_e_ref[i] != blk_e_ref[jnp.maximum(i - 1, 0)])

    @pl.when(used & new_expert)
    def _():
        wgu_bf[...] = wgu_ref[0, 0].astype(BF16)
        wd_bf[...] = wd_ref[0, 0].astype(BF16)

    @pl.when(used)
    def _():
        x = x_ref[...].astype(BF16)
        d_exp = wd_bf.shape[0]
        gu = _dot(x, wgu_bf[...]) + bgu_ref[0, 0]
        gate = jnp.minimum(gu[:, :d_exp], SWIGLU_LIMIT)
        lin = jnp.clip(gu[:, d_exp:], -SWIGLU_LIMIT, SWIGLU_LIMIT)
        act = gate * jax.nn.sigmoid(SWIGLU_ALPHA * gate) * (lin + 1.0)
        y_ref[...] = _dot(act.astype(BF16), wd_bf[...]) + bd_ref[0, 0]

    @pl.when(jnp.logical_not(used))
    def _():
        y_ref[...] = jnp.zeros_like(y_ref)


def _expert_blocks(xb, blk_e, n_used, layer, w_gu, b_gu, w_down, b_down):
    n_pad, d = xb.shape
    n_blocks = n_pad // MOE_BM
    depth, n_exp, _, d_gu = w_gu.shape
    grid_spec = pltpu.PrefetchScalarGridSpec(
        num_scalar_prefetch=2,
        grid=(n_blocks,),
        in_specs=[pl.BlockSpec((MOE_BM, d), lambda i, be, nu: (jnp.minimum(i, nu[0] - 1), 0)),
                  pl.BlockSpec((1, 1, d, d_gu), lambda i, be, nu: (layer, be[i], 0, 0)),
                  pl.BlockSpec((1, 1, 1, d_gu), lambda i, be, nu: (layer, be[i], 0, 0)),
                  pl.BlockSpec((1, 1, d_gu // 2, d), lambda i, be, nu: (layer, be[i], 0, 0)),
                  pl.BlockSpec((1, 1, 1, d), lambda i, be, nu: (layer, be[i], 0, 0))],
        out_specs=pl.BlockSpec((MOE_BM, d), lambda i, be, nu: (i, 0)),
        scratch_shapes=[pltpu.VMEM((d, d_gu), BF16), pltpu.VMEM((d_gu // 2, d), BF16)])
    return pl.pallas_call(
        _expert_kernel,
        grid_spec=grid_spec,
        out_shape=jax.ShapeDtypeStruct((n_pad, d), F32),
        compiler_params=pltpu.CompilerParams(dimension_semantics=("arbitrary",),
                                             vmem_limit_bytes=MOE_EXPERT_VMEM),
        name="moe_experts",
    )(blk_e, n_used, xb, w_gu, b_gu.reshape(depth, n_exp, 1, d_gu), w_down, b_down.reshape(depth, n_exp, 1, d))


def _combine_kernel(segs_ref, segs_next_ref, y_hbm, x_ref, mod_ref, route_ref, o_ref, ybuf, sem, *, n_exp):
    i = pl.program_id(0)
    slot = i % 2

    @pl.when(i == 0)
    def _():
        ybuf[...] = jnp.zeros_like(ybuf)
        _segment_dma(segs_ref, ybuf.at[0], y_hbm, sem.at[0], n_exp, to_hbm=False, wait=False)

    @pl.when(i + 1 < pl.num_programs(0))
    def _():
        _segment_dma(segs_next_ref, ybuf.at[1 - slot], y_hbm, sem.at[1 - slot], n_exp, to_hbm=False, wait=False)

    _segment_dma(segs_ref, ybuf.at[slot], y_hbm, sem.at[slot], n_exp, to_hbm=False, wait=True)
    y = ybuf[slot].astype(BF16)
    route = route_ref[...]
    lane, slots = _slot_lanes(route, ybuf.shape[1])
    weights = jnp.zeros(lane.shape, F32)
    for k in range(TOP_K):
        weights = jnp.where(lane == slots[k], route[:, k:k + 1], weights)
    o_ref[...] = x_ref[...] + mod_ref[0, 5:6, :] * _dot(weights.astype(BF16), y)


def _combine(yb, segs, x1, mods, route, n_batch, l, n_ctx, n_exp, latent_only):
    t, d = x1.shape
    nt = t // TM
    per_sample = l // TM
    n_ctx_tiles = n_ctx // TM
    kern = functools.partial(_combine_kernel, n_exp=n_exp)

    def mod_index(i):
        return (jnp.where(i % per_sample < n_ctx_tiles, n_batch, i // per_sample), 0, 0)

    if latent_only:
        lat_tiles = per_sample - n_ctx_tiles
        out_rows = n_batch * lat_tiles * TM
        out_index = lambda i: ((i // per_sample) * lat_tiles + jnp.maximum(i % per_sample - n_ctx_tiles, 0), 0)
    else:
        out_rows = t
        out_index = lambda i: (i, 0)

    return pl.pallas_call(
        kern,
        grid=(nt,),
        in_specs=[_seg_spec(lambda i: (i, 0, 0)),
                  _seg_spec(lambda i: (jnp.minimum(i + 1, nt - 1), 0, 0)),
                  pl.BlockSpec(memory_space=pl.ANY),
                  pl.BlockSpec((TM, d), lambda i: (i, 0)),
                  pl.BlockSpec((1, 6, d), mod_index),
                  pl.BlockSpec((TM, LANES), lambda i: (i, 0))],
        out_specs=pl.BlockSpec((TM, d), out_index),
        out_shape=jax.ShapeDtypeStruct((out_rows, d), F32),
        scratch_shapes=[pltpu.VMEM((2, _n_slots(n_exp), d), F32), pltpu.SemaphoreType.DMA((2,))],
        compiler_params=_cparams(("arbitrary",)),
        name="moe_combine",
    )(segs, segs, yb, x1, mods, route)


def _seg_table(start, units, offset, extra=None):
    n = start.shape[0]
    cols = [start, units, offset] + ([] if extra is None else [extra])
    pad = jnp.zeros((n, LANES - sum(c.shape[1] for c in cols)), jnp.int32)
    return jnp.concatenate(cols + [pad], axis=1).reshape(n, 1, LANES)


def _route(meta, n_exp, n_tokens):
    nt = meta.shape[0]
    count8 = meta[:, 0, :n_exp].astype(jnp.int32)
    seg_start = meta[:, 1, :n_exp].astype(jnp.int32)
    per_expert = jnp.sum(count8, axis=0)
    padded = (per_expert + MOE_BM - 1) // MOE_BM * MOE_BM
    pad_end = jnp.cumsum(padded)
    pad_start = pad_end - padded
    offset = pad_start[None, :] + jnp.cumsum(count8, axis=0) - count8
    n_pad = _round_up(n_tokens * TOP_K + nt * n_exp * (SEG_ALIGN - 1), MOE_BM) + n_exp * MOE_BM
    block_row = jnp.arange(n_pad // MOE_BM, dtype=jnp.int32) * MOE_BM
    blk_e = jnp.minimum(jnp.sum(block_row[:, None] >= pad_end[None, :], axis=1), n_exp - 1).astype(jnp.int32)
    n_used = (pad_end[-1:] // MOE_BM).astype(jnp.int32)
    segs = _seg_table(seg_start, count8 // SEG_ALIGN, offset)
    gaps = _seg_table(jnp.zeros((1, n_exp), jnp.int32), ((padded - per_expert) // SEG_ALIGN)[None, :],
                      (pad_start + per_expert)[None, :], n_used[None, :])
    return segs, gaps, blk_e, n_used, n_pad


def _channel_sublayer(x1, h2, route, meta, mods, layer, experts, n_ctx, latent_only):
    b, l, d = x1.shape
    t = b * l
    n_exp = experts[0].shape[1]
    assert 3 * n_exp < LANES and MOE_BM // SEG_ALIGN <= 2 * SEG_BITS[0]
    route = route.reshape(t, LANES)
    segs, gaps, blk_e, n_used, n_pad = _route(meta.reshape(t // TM, 8, LANES), n_exp, t)
    xb = _dispatch(h2.reshape(t, d), route, segs, gaps, n_pad, n_exp)
    yb = _expert_blocks(xb, blk_e, n_used, layer, *experts)
    x2 = _combine(yb, segs, x1.reshape(t, d), mods, route, b, l, n_ctx, n_exp, latent_only)
    return x2.reshape(b, -1, d)


def _pad_cols(w, n):
    return jnp.pad(w, ((0, 0), (0, n - w.shape[1])))


def _round_up(n, m):
    return -(-n // m) * m


def _prep_layer_params(pr):
    d = pr['w_in'].shape[0]
    na_w = d // 2
    rw_w = d // 4
    gd_w = d - na_w - rw_w
    na_cols = 3 * na_w
    rw_cols = 3 * rw_w + RW_DECAY_LORA + RW_ICLR_LORA + RW_GATE_LORA
    rw_pad = _round_up(rw_cols, LANES)
    gd_cols = pr['w_in'].shape[1] - na_cols - rw_cols
    gd_pad = _round_up(gd_cols, LANES)
    w_in = pr['w_in']
    out = dict(pr)
    out['w_in'] = jnp.concatenate([w_in[:, :na_cols],
                                   _pad_cols(w_in[:, na_cols:na_cols + rw_cols], rw_pad),
                                   _pad_cols(w_in[:, na_cols + rw_cols:], gd_pad)], axis=1).astype(BF16)
    out['rw_w'] = rw_pad
    out['gd_w'] = gd_pad
    n_heads = na_w // HEAD_DIM
    out['qkg'] = jnp.stack([jnp.tile(pr['na_q_gain'], n_heads) * HEAD_DIM ** -0.5,
                            jnp.tile(pr['na_k_gain'], n_heads)])
    head_of = np.arange(na_w) // HEAD_DIM
    out['avg'] = jnp.asarray((head_of[:, None] == head_of[None, :]) / HEAD_DIM, BF16)
    out['norm_mix_g'] = pr['norm_mix_g'].reshape(1, d)
    out['norm_ffn_g'] = pr['norm_ffn_g'].reshape(1, d)
    out['rw_mu_p'] = jnp.pad(pr['rw_mu'], (0, rw_pad - rw_cols)).reshape(1, rw_pad)
    o1, o2 = RW_DECAY_LORA, RW_DECAY_LORA + RW_ICLR_LORA
    out['rw_wup_p'] = jnp.pad(pr['rw_w_up'], ((0, 0), (0, LANES - o1), (0, 0)))
    out['rw_aup_p'] = jnp.pad(pr['rw_a_up'], ((0, 0), (o1, LANES - o2), (0, 0)))
    out['rw_gup_p'] = jnp.pad(pr['rw_g_up'], ((o2, LANES - o2 - RW_GATE_LORA), (0, 0)))
    head_of = np.arange(rw_w) // HEAD_DIM
    out['ones_bd'] = jnp.asarray(head_of[:, None] == head_of[None, :], BF16)
    out['gd_width'] = gd_w
    n_ab = pr['gd_A_log'].size
    out['gd_alog_p'] = jnp.pad(pr['gd_A_log'].reshape(-1), (0, LANES - n_ab)).reshape(1, LANES)
    out['gd_dtb_p'] = jnp.pad(pr['gd_dt_bias'].reshape(-1), (0, LANES - n_ab)).reshape(1, LANES)
    out['w_out_bf'] = pr['w_out'].astype(BF16)
    n_exp = pr['moe_router_w'].shape[1]
    router_w = _pad_cols(pr['moe_router_w'], LANES)
    out['router_w_hi'] = router_w.astype(BF16)
    out['router_w_lo'] = (router_w - out['router_w_hi'].astype(F32)).astype(BF16)
    out['router_b_p'] = jnp.concatenate([pr['moe_router_b'],
                                         jnp.full((LANES - n_exp,), MASKED, F32)]).reshape(1, LANES)
    return out


def _mixing_sublayer(xa, mods, pr, n_ctx):
    b, l, d = xa.shape
    na_w = d // 2
    p_na, p_rw, p_gd = _inproj(xa, mods, pr['norm_mix_g'], pr['w_in'], pr['qkg'], pr['avg'], n_ctx,
                               na_w, pr['rw_w'], pr['gd_w'])
    o_na = _na_attention(p_na, pr['na_bias'], n_ctx, na_w)
    rw2, ry1, rm, rn, bv, gate = _rwkv_prepare(p_rw, pr, n_ctx)
    ry = _scan(rw2, ry1, rm, rn, n_ctx)
    gy = _scan(*_gdn_prepare(p_gd, pr, n_ctx), n_ctx)
    return _outproj(xa, mods, o_na, ry, bv, gate, gy, p_gd, pr, n_ctx)


_LAYER_PARAMS = ('norm_mix_g', 'norm_ffn_g', 'w_in', 'w_out', 'na_q_gain', 'na_k_gain', 'na_rpb',
                 'rw_mu', 'rw_w0', 'rw_w_up', 'rw_a0', 'rw_a_up', 'rw_g_up', 'rw_k_k', 'rw_k_a', 'rw_r_k',
                 'rw_ln_w', 'rw_ln_b', 'gd_conv_w', 'gd_A_log', 'gd_dt_bias', 'gd_norm_g',
                 'moe_router_w', 'moe_router_b', 'moe_w_gu', 'moe_b_gu', 'moe_w_down', 'moe_b_down')


def kernel(x, c, ctx, c_ctx, ada_w, ada_b, norm_mix_g, norm_ffn_g, w_in, w_out, na_q_gain, na_k_gain, na_rpb, rw_mu, rw_w0, rw_w_up, rw_a0, rw_a_up, rw_g_up, rw_k_k, rw_k_a, rw_r_k, rw_ln_w, rw_ln_b, gd_conv_w, gd_A_log, gd_dt_bias, gd_norm_g, moe_router_w, moe_router_b, moe_w_gu, moe_b_gu, moe_w_down, moe_b_down):
    stacked = dict(zip(_LAYER_PARAMS, (norm_mix_g, norm_ffn_g, w_in, w_out, na_q_gain, na_k_gain, na_rpb,
                                       rw_mu, rw_w0, rw_w_up, rw_a0, rw_a_up, rw_g_up, rw_k_k, rw_k_a, rw_r_k,
                                       rw_ln_w, rw_ln_b, gd_conv_w, gd_A_log, gd_dt_bias, gd_norm_g,
                                       moe_router_w, moe_router_b, moe_w_gu, moe_b_gu, moe_w_down, moe_b_down)))
    b, s, d = x.shape
    n_ctx = ctx.shape[1]
    depth = ada_w.shape[0]
    assert b + 1 <= 8 and n_ctx % TM == 0 and s % TM == 0 and s % GRID_W == 0
    cvec = jnp.zeros((8, d), F32).at[:b].set(c).at[b].set(c_ctx)
    mods_all = _adaln(cvec, ada_w, ada_b).reshape(depth, 8, 6, d)[:, :b + 1]
    xa = jnp.concatenate([ctx, x], axis=1)
    experts = (moe_w_gu, moe_b_gu, moe_w_down, moe_b_down)
    for layer in range(depth):
        pr = _prep_layer_params({name: value[layer] for name, value in stacked.items()
                                 if not name.startswith('moe_w_') and not name.startswith('moe_b_')})
        pr['na_bias'] = _na_bias_tables(pr['na_rpb'], s // GRID_W)
        mods = mods_all[layer]
        x1, h2, route, meta = _mixing_sublayer(xa, mods, pr, n_ctx)
        xa = _channel_sublayer(x1, h2, route, meta, mods, layer, experts, n_ctx,
                               latent_only=(layer == depth - 1))
    return xa
```

```python
import functools
import math

import numpy as np
import jax
import jax.numpy as jnp
from jax import lax
from jax.experimental import pallas as pl
from jax.experimental.pallas import tpu as pltpu

F32 = jnp.float32
BF16 = jnp.bfloat16
HIGHEST = lax.Precision.HIGHEST

GRID_W = 64
NORM_EPS = 1e-6
HEAD_DIM = 64
NA_WIN_ROWS = 8
NA_WIN_COLS = 16
RW_DECAY_LORA = 32
RW_ICLR_LORA = 32
RW_GATE_LORA = 64
RW_GN_EPS = 64e-5
GD_CONV = 5
TOP_K = 4
SWIGLU_LIMIT = 7.0
SWIGLU_ALPHA = 1.702

LANES = 128
TM = 256
NA_QROWS = 4
NA_KROWS = NA_QROWS + NA_WIN_ROWS
CHUNK = 64
CHUNKS_PER_STEP = 4
MASKED = -1e30
VMEM_LIMIT = 48 * 1024 * 1024


def _cparams(sem):
    return pltpu.CompilerParams(dimension_semantics=sem, vmem_limit_bytes=VMEM_LIMIT)


def _dot(a, b, **kw):
    return jnp.dot(a, b, preferred_element_type=F32, **kw)


def _dot_split(a, b_bf16):
    hi = a.astype(BF16)
    lo = (a - hi.astype(F32)).astype(BF16)
    return _dot(hi, b_bf16) + _dot(lo, b_bf16)


def _dot_split3(a, b):
    a_hi = a.astype(BF16)
    a_lo = (a - a_hi.astype(F32)).astype(BF16)
    b_hi = b.astype(BF16)
    b_lo = (b - b_hi.astype(F32)).astype(BF16)
    return _dot(a_hi, b_hi) + (_dot(a_hi, b_lo) + _dot(a_lo, b_hi))


def _dot_nt(a, b, **kw):
    return lax.dot_general(a, b, (((1,), (1,)), ((), ())), preferred_element_type=F32, **kw)


def _dot_tn(a, b, **kw):
    return lax.dot_general(a, b, (((0,), (0,)), ((), ())), preferred_element_type=F32, **kw)


def _adaln_kernel(c_ref, w_ref, b_ref, o_ref):
    c = c_ref[...]
    s = c * jax.nn.sigmoid(c)
    o_ref[0] = _dot(s.astype(BF16), w_ref[0].astype(BF16)) + b_ref[0]


def _adaln(cvec, ada_w, ada_b):
    depth, d, n = ada_w.shape
    tn = n // 4
    return pl.pallas_call(
        _adaln_kernel,
        grid=(depth, n // tn),
        in_specs=[pl.BlockSpec((8, d), lambda l, j: (0, 0)),
                  pl.BlockSpec((1, d, tn), lambda l, j: (l, 0, j)),
                  pl.BlockSpec((1, 1, tn), lambda l, j: (l, 0, j))],
        out_specs=pl.BlockSpec((1, 8, tn), lambda l, j: (l, 0, j)),
        out_shape=jax.ShapeDtypeStruct((depth, 8, n), F32),
        compiler_params=_cparams(("arbitrary", "arbitrary")),
        name="adaln",
    )(cvec, ada_w, ada_b.reshape(depth, 1, n))


def _mod_index(n_ctx_tiles, n_batch):
    return lambda b, t: (jnp.where(t < n_ctx_tiles, n_batch, b), 0, 0)


def _rms(x):
    return x * lax.rsqrt(jnp.mean(x * x, axis=-1, keepdims=True) + NORM_EPS)


def _inproj_kernel(x_ref, mod_ref, g_ref, w_ref, qkg_ref, avg_ref, na_ref, rw_ref, gd_ref, *, na_w, rw_w):
    mod = mod_ref[0]
    h = _rms(x_ref[0]) * g_ref[...]
    hb = (h * (1.0 + mod[1:2]) + mod[0:1]).astype(BF16)
    pa = _dot(hb, w_ref[:, 0:3 * na_w])
    avg = avg_ref[...]
    qkg = qkg_ref[...]
    q = pa[:, 0:na_w]
    k = pa[:, na_w:2 * na_w]
    qn = q * lax.rsqrt(_dot(q * q, avg) + NORM_EPS) * qkg[0:1]
    kn = k * lax.rsqrt(_dot(k * k, avg) + NORM_EPS) * qkg[1:2]
    na_ref[0, :, 0:na_w] = qn.astype(BF16)
    na_ref[0, :, na_w:2 * na_w] = kn.astype(BF16)
    na_ref[0, :, 2 * na_w:3 * na_w] = pa[:, 2 * na_w:3 * na_w].astype(BF16)
    rw_ref[0] = _dot(hb, w_ref[:, 3 * na_w:3 * na_w + rw_w])
    gd_ref[0] = _dot(hb, w_ref[:, 3 * na_w + rw_w:])


def _inproj(xa, mods, g, w_all, qkg, avg, n_ctx, na_w, rw_w, gd_w):
    b, l, d = xa.shape
    nt = l // TM
    kern = functools.partial(_inproj_kernel, na_w=na_w, rw_w=rw_w)
    return pl.pallas_call(
        kern,
        grid=(b, nt),
        in_specs=[pl.BlockSpec((1, TM, d), lambda i, t: (i, t, 0)),
                  pl.BlockSpec((1, 6, d), _mod_index(n_ctx // TM, b)),
                  pl.BlockSpec((1, d), lambda i, t: (0, 0)),
                  pl.BlockSpec(w_all.shape, lambda i, t: (0, 0)),
                  pl.BlockSpec(qkg.shape, lambda i, t: (0, 0)),
                  pl.BlockSpec(avg.shape, lambda i, t: (0, 0))],
        out_specs=[pl.BlockSpec((1, TM, 3 * na_w), lambda i, t: (i, t, 0)),
                   pl.BlockSpec((1, TM, rw_w), lambda i, t: (i, t, 0)),
                   pl.BlockSpec((1, TM, gd_w), lambda i, t: (i, t, 0))],
        out_shape=[jax.ShapeDtypeStruct((b, l, 3 * na_w), BF16),
                   jax.ShapeDtypeStruct((b, l, rw_w), F32),
                   jax.ShapeDtypeStruct((b, l, gd_w), F32)],
        compiler_params=_cparams(("arbitrary", "arbitrary")),
        name="inproj",
    )(xa, mods, g, w_all, qkg, avg)


def _na_bias_tables(rpb, rows):
    n_heads = rpb.shape[0]
    nq, nk = NA_QROWS * GRID_W, NA_KROWS * GRID_W
    qc = np.arange(GRID_W)[:, None]
    kc = np.arange(GRID_W)[None, :]
    c0 = np.clip(qc - NA_WIN_COLS // 2, 0, GRID_W - NA_WIN_COLS)
    col_ok = (kc >= c0) & (kc < c0 + NA_WIN_COLS)
    dc = kc - qc + NA_WIN_COLS - 1
    pick_col = (dc[None] == np.arange(2 * NA_WIN_COLS - 1)[:, None, None]).astype(np.float32)
    by_col = jnp.einsum('hrd,dqk->hrqk', rpb, pick_col, precision=HIGHEST)
    tables = []
    for r0 in (0, NA_QROWS, rows - NA_QROWS):
        ks = int(np.clip(r0 - NA_WIN_ROWS // 2, 0, rows - NA_KROWS))
        qr = (r0 + np.arange(NA_QROWS))[:, None]
        kr = (ks + np.arange(NA_KROWS))[None, :]
        s0 = np.clip(qr - NA_WIN_ROWS // 2, 0, rows - NA_WIN_ROWS)
        row_ok = (kr >= s0) & (kr < s0 + NA_WIN_ROWS)
        dr = kr - qr + NA_WIN_ROWS - 1
        pick_row = ((dr[None] == np.arange(2 * NA_WIN_ROWS - 1)[:, None, None]) & row_ok[None]).astype(np.float32)
        bias = jnp.einsum('rab,hrqk->haqbk', pick_row, by_col, precision=HIGHEST)
        ok = row_ok[:, None, :, None] & col_ok[None, :, None, :]
        tables.append(jnp.where(ok[None], bias, MASKED).reshape(n_heads, nq, nk))
    tables.append(jnp.full((n_heads, nq, nk), MASKED, F32))
    return jnp.stack(tables).reshape(4, n_heads // 2, 2, nq, nk)


def _na_kernel(q_ref, k_ref, v_ref, bias_ref, o_ref, *, n_ctx, rows):
    j = pl.program_id(2)
    nk = NA_KROWS * GRID_W
    ks_row = jnp.clip((j - 1) * NA_QROWS - NA_WIN_ROWS // 2, 0, rows - NA_KROWS)
    kstart = pl.multiple_of(n_ctx + ks_row * GRID_W, GRID_W)
    q = q_ref[0]
    kl = k_ref[0, pl.ds(kstart, nk), :]
    vl = v_ref[0, pl.ds(kstart, nk), :]
    kc = k_ref[0, 0:n_ctx, :]
    vc = v_ref[0, 0:n_ctx, :]
    lane = lax.broadcasted_iota(jnp.int32, q.shape, 1)
    outs = []
    for h in range(2):
        in_head = (lane >= h * HEAD_DIM) & (lane < (h + 1) * HEAD_DIM)
        qh = jnp.where(in_head, q, jnp.zeros_like(q))
        s_loc = _dot_nt(qh, kl) + bias_ref[0, 0, h]
        s_ctx = _dot_nt(qh, kc)
        m = jnp.maximum(jnp.max(s_loc, axis=-1, keepdims=True), jnp.max(s_ctx, axis=-1, keepdims=True))
        p_loc = jnp.exp(s_loc - m)
        p_ctx = jnp.exp(s_ctx - m)
        den = jnp.sum(p_loc, axis=-1, keepdims=True) + jnp.sum(p_ctx, axis=-1, keepdims=True)
        o = _dot(p_loc.astype(BF16), vl) + _dot(p_ctx.astype(BF16), vc)
        outs.append(o / den)
    o_ref[0] = jnp.where(lane < HEAD_DIM, outs[0], outs[1]).astype(o_ref.dtype)


def _na_attention(p_na, bias, n_ctx, na_w):
    b, l, _ = p_na.shape
    rows = (l - n_ctx) // GRID_W
    nq = NA_QROWS * GRID_W
    assert n_ctx == nq and rows >= NA_KROWS and rows % NA_QROWS == 0
    n_pairs = na_w // LANES
    nblk = l // nq
    kern = functools.partial(_na_kernel, n_ctx=n_ctx, rows=rows)

    def bias_index(i, hp, j):
        pat = jnp.where(j == 0, 3, jnp.where(j == 1, 0, jnp.where(j == nblk - 1, 2, 1)))
        return (pat, hp, 0, 0, 0)

    return pl.pallas_call(
        kern,
        grid=(b, n_pairs, nblk),
        in_specs=[pl.BlockSpec((1, nq, LANES), lambda i, hp, j: (i, j, hp)),
                  pl.BlockSpec((1, l, LANES), lambda i, hp, j: (i, 0, n_pairs + hp)),
                  pl.BlockSpec((1, l, LANES), lambda i, hp, j: (i, 0, 2 * n_pairs + hp)),
                  pl.BlockSpec((1, 1, 2) + bias.shape[3:], bias_index)],
        out_specs=pl.BlockSpec((1, nq, LANES), lambda i, hp, j: (i, j, hp)),
        out_shape=jax.ShapeDtypeStruct((b, l, na_w), BF16),
        compiler_params=_cparams(("arbitrary", "arbitrary", "arbitrary")),
        name="na_attention",
    )(p_na, p_na, p_na, bias)


def _tri_masks(reverse):
    t = lax.broadcasted_iota(jnp.int32, (CHUNK, CHUNK), 0)
    s = lax.broadcasted_iota(jnp.int32, (CHUNK, CHUNK), 1)
    earlier_or_same = (s >= t) if reverse else (s <= t)
    earlier = (s > t) if reverse else (s < t)
    return earlier_or_same, earlier, t == s


def _each(fn, *lists):
    return [fn(*args) for args in zip(*lists)]


def _bf16(values):
    return [u.astype(BF16) for u in values]


def _head_lanes(width):
    lane = lax.broadcasted_iota(jnp.int32, (1, width), 1)
    return [(lane >= h * HEAD_DIM) & (lane < (h + 1) * HEAD_DIM) for h in range(width // HEAD_DIM)]


def _stack_heads(x, heads):
    return jnp.concatenate([jnp.where(in_head, x, 0.0) for in_head in heads], axis=0).astype(BF16)


def _tiled_masks(n_heads, reverse):
    t = lax.broadcasted_iota(jnp.int32, (CHUNK, n_heads * CHUNK), 0)
    s = lax.broadcasted_iota(jnp.int32, (CHUNK, n_heads * CHUNK), 1) % CHUNK
    earlier = (s > t) if reverse else (s < t)
    return earlier, earlier | (s == t), s == t


def _unit_tri_inverse(a, eye, heads):
    assert CHUNK == 64
    diag = lambda values: [_stack_heads(u, heads) for u in values]
    x1 = [-u for u in a]
    x1b, x1d = _bf16(x1), diag(x1)
    x2 = _each(_dot, x1b, x1d)
    x2b, x2d = _bf16(x2), diag(x2)
    x4 = _each(_dot, x2b, x2d)
    x3 = _each(_dot, x1b, x2d)
    x4b, x4d = _bf16(x4), diag(x4)
    x8 = _each(_dot, x4b, x4d)
    p1 = _each(lambda u1, u2, u3: eye + u1 + u2 + u3, x1, x2, x3)
    x8b, x8d = _bf16(x8), diag(x8)
    x12 = _each(_dot, x4b, x8d)
    x16 = _each(_dot, x8b, x8d)
    p2 = _each(lambda u4, u8, u12: eye + u4 + u8 + u12, x4, x8, x12)
    p12 = _each(_dot, _bf16(p1), diag(p2))
    x16b, x16d = _bf16(x16), diag(x16)
    x32 = _each(_dot, x16b, x16d)
    x48 = _each(_dot, x16b, diag(x32))
    p3 = _each(lambda u16, u32, u48: eye + u16 + u32 + u48, x16, x32, x48)
    return _each(_dot, _bf16(p12), diag(p3))


def _chunk_matrices(a_ka, a_kk, a_ra, a_rk, ks, rt, v, kd, ad, eye, heads):
    stack = lambda values: [_stack_heads(u, heads) for u in values]
    t_inv = _bf16(_unit_tri_inverse(a_ka, eye.astype(F32), heads))
    vs, a_rab, adb = stack(v), _bf16(a_ra), _bf16(ad)
    av = _each(_dot, _bf16(a_kk), vs)
    w1 = _each(_dot, t_inv, ks)
    u0 = _each(_dot, t_inv, stack(av))
    rkv = _each(_dot, _bf16(a_rk), vs)
    w2 = _each(lambda r, a, w: r - _dot(a, w), rt, a_rab, stack(w1))
    y1 = _each(lambda y, a, u: y - _dot(a, u), rkv, a_rab, stack(u0))
    mm = _each(_dot_tn, adb, _bf16(w1))
    nn = _each(lambda kd_, v_, ad_, u: _dot_tn(kd_, v_) - _dot_tn(ad_, u), _bf16(kd), _bf16(v), adb, _bf16(u0))
    return list(zip(w2, y1, mm, nn))


def _store_chunk(refs, d, c0, n_heads, w2, y1, mm, nn, p_end):
    w2_ref, y1_ref, m_ref, n_ref = refs
    rows = pl.ds(c0, CHUNK)
    eye = (lax.broadcasted_iota(jnp.int32, (HEAD_DIM, HEAD_DIM), 0)
           == lax.broadcasted_iota(jnp.int32, (HEAD_DIM, HEAD_DIM), 1))
    for h in range(n_heads):
        lo, hi = h * HEAD_DIM, (h + 1) * HEAD_DIM
        w2_ref[d, 0, h, rows, :] = w2[:, lo:hi]
        y1_ref[d, 0, h, rows, :] = y1[:, lo:hi]
        m_ref[d, 0, h, rows, :] = jnp.where(eye, p_end[:, lo:hi], 0.0) - mm[lo:hi, lo:hi]
        n_ref[d, 0, h, rows, :] = nn[lo:hi, lo:hi]


def _scan_out_specs(b, n_heads, l):
    shape = jax.ShapeDtypeStruct((2, b, n_heads, l, HEAD_DIM), F32)
    spec = pl.BlockSpec((2, 1, n_heads, TM, HEAD_DIM), lambda i, t: (0, i, 0, t, 0))
    return [shape] * 4, [spec] * 4


def _seq_edges(t, n_ctx_tiles, n_tiles):
    first = (t == 0) | (t == n_ctx_tiles)
    last = (t == n_ctx_tiles - 1) | (t == n_tiles - 1)
    return first, last


def _fill_halo(ext_ref, x_ref, prev_ref, next_ref, first, last):
    ext_ref[0:8, :] = jnp.where(first, 0.0, prev_ref[0])
    ext_ref[8:8 + TM, :] = x_ref[0]
    ext_ref[8 + TM:16 + TM, :] = jnp.where(last, 0.0, next_ref[0])


def _halo_specs(width, n_tiles):
    per = TM // 8
    return [pl.BlockSpec((1, TM, width), lambda i, t: (i, t, 0)),
            pl.BlockSpec((1, 8, width), lambda i, t: (i, jnp.maximum(t * per - 1, 0), 0)),
            pl.BlockSpec((1, 8, width), lambda i, t: (i, jnp.minimum((t + 1) * per, n_tiles * per - 1), 0))]


def _full_spec(a):
    return pl.BlockSpec(a.shape, lambda i, t: (0,) * a.ndim)


def _rwkv_prep_kernel(x_ref, prev_ref, next_ref, mu_ref, w0_ref, wup_ref, a0_ref, aup_ref, gup_ref,
                      kk_ref, ka_ref, rk_ref, ones_ref,
                      w2_ref, y1_ref, m_ref, n_ref, bv_ref, gate_ref,
                      ext_ref, tok_ref, *, n_ctx_tiles, n_tiles, width):
    t = pl.program_id(1)
    first, last = _seq_edges(t, n_ctx_tiles, n_tiles)
    _fill_halo(ext_ref, x_ref, prev_ref, next_ref, first, last)
    p = ext_ref[8:8 + TM, :]
    prev = ext_ref[7:7 + TM, :]
    nxt = ext_ref[9:9 + TM, :]
    ps = p + mu_ref[...] * (0.5 * (prev + nxt) - p)
    r = ps[:, 0:width]
    k = ps[:, width:2 * width]
    v = ps[:, 2 * width:3 * width]
    lo = ps[:, 3 * width:3 * width + LANES]
    ones_bd = ones_ref[...]
    gate_ref[0] = _dot(jax.nn.sigmoid(lo), gup_ref[...])
    kq = k * kk_ref[...]
    kk = kq * lax.rsqrt(_dot_split(kq * kq, ones_bd) + NORM_EPS)
    tanh_lo = jnp.tanh(lo)
    tok_ref[0] = r
    tok_ref[1] = kk
    tok_ref[2] = v
    k_sum = jnp.zeros_like(k)
    for d in range(2):
        z = w0_ref[d:d + 1, :] + _dot(tanh_lo, wup_ref[d])
        w_log = -jax.nn.softplus(-z) - 0.5
        a = jax.nn.sigmoid(a0_ref[d:d + 1, :] + _dot(lo, aup_ref[d]))
        k_dir = k * (1.0 + (a - 1.0) * ka_ref[...])
        k_sum = k_sum + k_dir
        tok_ref[3 + 3 * d] = -jnp.exp(w_log)
        tok_ref[4 + 3 * d] = kk * a
        tok_ref[5 + 3 * d] = k_dir
    bonus = _dot_split(r * k_sum * rk_ref[...], ones_bd)
    bv_ref[0] = bonus * v

    heads = _head_lanes(width)
    out_refs = (w2_ref, y1_ref, m_ref, n_ref)

    def chunk_body(c, carry):
        args = [[] for _ in range(9)]
        where = []
        for cc in range(CHUNKS_PER_STEP):
            c0 = pl.multiple_of((c * CHUNKS_PER_STEP + cc) * CHUNK, CHUNK)
            rows = pl.ds(c0, CHUNK)
            r_c, kk_c, v_c = tok_ref[0, rows, :], tok_ref[1, rows, :], tok_ref[2, rows, :]
            for d in range(2):
                logw, alpha, k_c = (tok_ref[3 + 3 * d, rows, :], tok_ref[4 + 3 * d, rows, :],
                                    tok_ref[5 + 3 * d, rows, :])
                incl, _, _ = _tri_masks(reverse=(d == 1))
                ci = _dot(incl.astype(F32), logw, precision=HIGHEST)
                e_ci, e_nci, e_ce = jnp.exp(ci), jnp.exp(-ci), jnp.exp(ci - logw)
                p_end = e_ci[0:1, :] if d == 1 else e_ci[CHUNK - 1:CHUNK, :]
                kt, rt, kh, ah = kk_c * e_ce, r_c * e_ci, k_c * e_nci, alpha * e_nci
                earlier, earlier_or_same, eye = _tiled_masks(len(heads), reverse=(d == 1))
                khs, ahs = _stack_heads(kh, heads), _stack_heads(ah, heads)
                ktb, rtb = kt.astype(BF16), rt.astype(BF16)
                a_ka = jnp.where(earlier, _dot_nt(ktb, ahs), 0.0)
                a_kk = jnp.where(earlier, _dot_nt(ktb, khs), 0.0)
                a_ra = jnp.where(earlier_or_same, _dot_nt(rtb, ahs), 0.0)
                a_rk = jnp.where(earlier_or_same, _dot_nt(rtb, khs), 0.0)
                values = (a_ka, a_kk, a_ra, a_rk, _stack_heads(kt, heads), rt, v_c, kh * p_end, ah * p_end)
                for slot, value in zip(args, values):
                    slot.append(value)
                where.append((d, c0, p_end))
        for (d, c0, p_end), res in zip(where, _chunk_matrices(*args, eye, heads)):
            _store_chunk(out_refs, d, c0, len(heads), *res, p_end)
        return carry

    lax.fori_loop(0, TM // CHUNK // CHUNKS_PER_STEP, chunk_body, 0)


def _rwkv_prepare(p_rw, pr, n_ctx):
    b, l, pw = p_rw.shape
    width = pr['rw_k_k'].shape[-1]
    n_heads = width // HEAD_DIM
    nt = l // TM
    kern = functools.partial(_rwkv_prep_kernel, n_ctx_tiles=n_ctx // TM, n_tiles=nt, width=width)
    params = [pr['rw_mu_p'], pr['rw_w0'], pr['rw_wup_p'], pr['rw_a0'], pr['rw_aup_p'], pr['rw_gup_p'],
              pr['rw_k_k'].reshape(1, width), pr['rw_k_a'].reshape(1, width), pr['rw_r_k'].reshape(1, width),
              pr['ones_bd']]
    scan_shapes, scan_specs = _scan_out_specs(b, n_heads, l)
    tok_spec = pl.BlockSpec((1, TM, width), lambda i, t: (i, t, 0))
    return pl.pallas_call(
        kern,
        grid=(b, nt),
        in_specs=_halo_specs(pw, nt) + [_full_spec(a) for a in params],
        out_specs=scan_specs + [tok_spec, tok_spec],
        out_shape=scan_shapes + [jax.ShapeDtypeStruct((b, l, width), F32)] * 2,
        scratch_shapes=[pltpu.VMEM((TM + 16, pw), F32), pltpu.VMEM((9, TM, width), F32)],
        compiler_params=_cparams(("arbitrary", "arbitrary")),
        name="rwkv_prepare",
    )(p_rw, p_rw, p_rw, *params)


def _scan_kernel(w2f_ref, y1f_ref, mf_ref, nf_ref, w2b_ref, y1b_ref, mb_ref, nb_ref, yf_ref, yb_ref, h_ref):
    @pl.when(pl.program_id(0) == 0)
    def _():
        h_ref[...] = jnp.zeros_like(h_ref)

    n_batch, n_heads = h_ref.shape[1], h_ref.shape[2]
    n_sub = yf_ref.shape[2] // CHUNK
    dirs = ((w2f_ref, y1f_ref, mf_ref, nf_ref, yf_ref), (w2b_ref, y1b_ref, mb_ref, nb_ref, yb_ref))
    chains = [(d, i, h) for d in range(2) for i in range(n_batch) for h in range(n_heads)]
    states = [h_ref[d, i, h] for d, i, h in chains]
    for j in range(n_sub):
        for c, (d, i, h) in enumerate(chains):
            w2_ref, y1_ref, m_ref, n_ref, y_ref = dirs[d]
            jj = j if d == 0 else n_sub - 1 - j
            rows = slice(jj * CHUNK, (jj + 1) * CHUNK)
            y_ref[i, h, rows, :] = _dot(w2_ref[0, i, h, rows, :], states[c]) + y1_ref[0, i, h, rows, :]
            states[c] = _dot_split3(m_ref[0, i, h, rows, :], states[c]) + n_ref[0, i, h, rows, :]
    for c, (d, i, h) in enumerate(chains):
        h_ref[d, i, h] = states[c]


def _scan(w2, y1, m, n, n_ctx):
    _, b, n_heads, l, _ = w2.shape
    nc = l // TM
    ncc = n_ctx // TM

    def bwd_chunk(i):
        return jnp.where(i < ncc, ncc - 1 - i, nc - 1 + ncc - i)

    blk = (1, b, n_heads, TM, HEAD_DIM)
    fwd = pl.BlockSpec(blk, lambda i: (0, 0, 0, i, 0))
    bwd = pl.BlockSpec(blk, lambda i: (1, 0, 0, bwd_chunk(i), 0))
    out_shape = jax.ShapeDtypeStruct((b, n_heads, l, HEAD_DIM), F32)
    return pl.pallas_call(
        _scan_kernel,
        grid=(nc,),
        in_specs=[fwd] * 4 + [bwd] * 4,
        out_specs=[pl.BlockSpec(blk[1:], lambda i: (0, 0, i, 0)),
                   pl.BlockSpec(blk[1:], lambda i: (0, 0, bwd_chunk(i), 0))],
        out_shape=[out_shape, out_shape],
        scratch_shapes=[pltpu.VMEM((2, b, n_heads, HEAD_DIM, HEAD_DIM), F32)],
        compiler_params=_cparams(("arbitrary",)),
        name="chunk_scan",
    )(w2, y1, m, n, w2, y1, m, n)


def _gdn_prep_kernel(x_ref, prev_ref, next_ref, cw_ref, alog_ref, dtb_ref, ones_ref,
                     w2_ref, y1_ref, m_ref, n_ref,
                     ext_ref, tok_ref, *, n_ctx_tiles, n_tiles, width):
    t = pl.program_id(1)
    first, last = _seq_edges(t, n_ctx_tiles, n_tiles)
    _fill_halo(ext_ref, x_ref, prev_ref, next_ref, first, last)
    half = GD_CONV // 2
    conv = jnp.zeros((TM, 3 * width), F32)
    for j in range(GD_CONV):
        conv = conv + cw_ref[j:j + 1, :] * ext_ref[8 + j - half:8 + j - half + TM, 0:3 * width]
    qkv = conv * jax.nn.sigmoid(conv)
    ones_bd = ones_ref[...]

    def l2n(u):
        return u * lax.rsqrt(_dot_split(u * u, ones_bd) + NORM_EPS)

    q = l2n(qkv[:, 0:width]) * HEAD_DIM ** -0.5
    k = l2n(qkv[:, width:2 * width])
    v = qkv[:, 2 * width:3 * width]
    ab = ext_ref[8:8 + TM, 4 * width:4 * width + LANES]
    g_all = -jnp.exp(alog_ref[...]) * jax.nn.softplus(ab + dtb_ref[...])
    beta_all = jax.nn.sigmoid(ab)
    heads = _head_lanes(width)
    n_heads = len(heads)
    tok_ref[0] = q
    tok_ref[1] = k
    tok_ref[2] = v
    for d in range(2):
        g = jnp.zeros((TM, width), F32)
        beta = jnp.zeros((TM, width), F32)
        for h, in_head in enumerate(heads):
            ja = d * n_heads + h
            jb = 2 * n_heads + ja
            g = jnp.where(in_head, g_all[:, ja:ja + 1], g)
            beta = jnp.where(in_head, beta_all[:, jb:jb + 1], beta)
        tok_ref[3 + 2 * d] = g
        tok_ref[4 + 2 * d] = k * beta
    out_refs = (w2_ref, y1_ref, m_ref, n_ref)

    def chunk_inputs(c0, d):
        rows = pl.ds(c0, CHUNK)
        q_c, k_c, v_c = tok_ref[0, rows, :], tok_ref[1, rows, :], tok_ref[2, rows, :]
        g, kb = tok_ref[3 + 2 * d, rows, :], tok_ref[4 + 2 * d, rows, :]
        incl, _, _ = _tri_masks(reverse=(d == 1))
        ci = _dot(incl.astype(F32), g, precision=HIGHEST)
        ce = ci - g
        ci_end = ci[0:1, :] if d == 1 else ci[CHUNK - 1:CHUNK, :]
        earlier, earlier_or_same, eye = _tiled_masks(n_heads, reverse=(d == 1))
        ci_row = jnp.sum(jnp.where(eye, ci, 0.0), axis=0, keepdims=True)
        ce_row = jnp.sum(jnp.where(eye, ce, 0.0), axis=0, keepdims=True)
        kbs = _stack_heads(kb, heads)
        kkb = _dot_nt(k_c.astype(BF16), kbs)
        qkb = _dot_nt(q_c.astype(BF16), kbs)
        a_ka = kkb * jnp.exp(jnp.where(earlier, ce - ce_row, MASKED))
        a_kk = kkb * jnp.exp(jnp.where(earlier, ce - ci_row, MASKED))
        a_ra = qkb * jnp.exp(jnp.where(earlier_or_same, ci - ce_row, MASKED))
        a_rk = qkb * jnp.exp(jnp.where(earlier_or_same, ci - ci_row, MASKED))
        values = (a_ka, a_kk, a_ra, a_rk, _stack_heads(k_c * jnp.exp(ce), heads), q_c * jnp.exp(ci), v_c,
                  kb * jnp.exp(ci_end - ci), kb * jnp.exp(ci_end - ce))
        return values, jnp.exp(ci_end), eye

    def chunk_body(c, carry):
        args = [[] for _ in range(9)]
        where = []
        for cc in range(CHUNKS_PER_STEP):
            c0 = pl.multiple_of((c * CHUNKS_PER_STEP + cc) * CHUNK, CHUNK)
            for d in range(2):
                values, p_end, eye = chunk_inputs(c0, d)
                for slot, value in zip(args, values):
                    slot.append(value)
                where.append((d, c0, p_end))
        for (d, c0, p_end), res in zip(where, _chunk_matrices(*args, eye, heads)):
            _store_chunk(out_refs, d, c0, n_heads, *res, p_end)
        return carry

    lax.fori_loop(0, TM // CHUNK // CHUNKS_PER_STEP, chunk_body, 0)


def _gdn_prepare(p_gd, pr, n_ctx):
    b, l, pw = p_gd.shape
    width = pr['gd_width']
    n_heads = width // HEAD_DIM
    nt = l // TM
    kern = functools.partial(_gdn_prep_kernel, n_ctx_tiles=n_ctx // TM, n_tiles=nt, width=width)
    params = [pr['gd_conv_w'], pr['gd_alog_p'], pr['gd_dtb_p'], pr['ones_bd']]
    scan_shapes, scan_specs = _scan_out_specs(b, n_heads, l)
    return pl.pallas_call(
        kern,
        grid=(b, nt),
        in_specs=_halo_specs(pw, nt) + [_full_spec(a) for a in params],
        out_specs=scan_specs,
        out_shape=scan_shapes,
        scratch_shapes=[pltpu.VMEM((TM + 16, pw), F32), pltpu.VMEM((7, TM, width), F32)],
        compiler_params=_cparams(("arbitrary", "arbitrary")),
        name="gdn_prepare",
    )(p_gd, p_gd, p_gd, *params)


def _outproj_kernel(x_ref, mod_ref, na_ref, ryf_ref, ryb_ref, bv_ref, gate_ref, gyf_ref, gyb_ref, z_ref,
                    lnw_ref, lnb_ref, gng_ref, wo_ref, g2_ref, rwh_ref, rwl_ref, rb_ref,
                    x1_ref, h2_ref, route_ref, meta_ref, *, na_w, rw_w):
    n_heads = ryf_ref.shape[1]
    rw_parts, gd_parts = [], []
    for h in range(n_heads):
        y = ryf_ref[0, h] + ryb_ref[0, h]
        yc = y - jnp.mean(y, axis=-1, keepdims=True)
        rw_parts.append(yc * lax.rsqrt(jnp.mean(yc * yc, axis=-1, keepdims=True) + RW_GN_EPS))
        gd_parts.append(_rms(gyf_ref[0, h] + gyb_ref[0, h]) * gng_ref[...])
    yn = jnp.concatenate(rw_parts, axis=-1)
    o_rw = (yn * lnw_ref[...] + lnb_ref[...] + bv_ref[0]) * gate_ref[0]
    z = z_ref[0]
    o_gd = jnp.concatenate(gd_parts, axis=-1) * (z * jax.nn.sigmoid(z))
    o = (_dot(na_ref[0], wo_ref[0:na_w, :])
         + _dot(o_rw.astype(BF16), wo_ref[na_w:na_w + rw_w, :])
         + _dot(o_gd.astype(BF16), wo_ref[na_w + rw_w:, :]))
    mod = mod_ref[0]
    x1 = x_ref[0] + mod[2:3] * o
    x1_ref[0] = x1
    h2 = _rms(x1) * g2_ref[...] * (1.0 + mod[4:5]) + mod[3:4]
    h2_ref[0] = h2
    h2_hi = h2.astype(BF16)
    h2_lo = (h2 - h2_hi.astype(F32)).astype(BF16)
    logits = (_dot(h2_hi, rwh_ref[...]) + (_dot(h2_hi, rwl_ref[...]) + _dot(h2_lo, rwh_ref[...]))) + rb_ref[...]
    lane = lax.broadcasted_iota(jnp.int32, logits.shape, 1)
    chosen, weights = [], []
    onehot = jnp.zeros(logits.shape, F32)
    for kk in range(TOP_K):
        m = jnp.max(logits, axis=-1, keepdims=True)
        idx = jnp.min(jnp.where(logits == m, lane, LANES), axis=-1, keepdims=True)
        chosen.append(idx)
        top_max = m if kk == 0 else top_max
        weights.append(jnp.exp(m - top_max))
        onehot = jnp.where(lane == idx, 1.0, onehot)
        logits = jnp.where(lane == idx, -jnp.inf, logits)
    denom = weights[0] + weights[1] + weights[2] + weights[3]
    tok_i = lax.broadcasted_iota(jnp.int32, (TM, TM), 0)
    tok_j = lax.broadcasted_iota(jnp.int32, (TM, TM), 1)
    before = jnp.where(tok_j < tok_i, 1.0, 0.0).astype(BF16)
    rank = _dot(before, onehot.astype(BF16))
    counts = jnp.sum(onehot, axis=0, keepdims=True)
    count8 = jnp.floor((counts + 7.0) * 0.125) * 8.0
    exp_i = lax.broadcasted_iota(jnp.int32, (LANES, LANES), 0)
    exp_j = lax.broadcasted_iota(jnp.int32, (LANES, LANES), 1)
    seg_start = _dot(jnp.broadcast_to(count8, (8, LANES)), jnp.where(exp_i < exp_j, 1.0, 0.0),
                     precision=HIGHEST)[0:1]
    slot_of = seg_start + rank
    route = jnp.zeros(logits.shape, F32)
    for kk in range(TOP_K):
        slot = jnp.sum(jnp.where(lane == chosen[kk], slot_of, 0.0), axis=-1, keepdims=True)
        route = jnp.where(lane == kk, weights[kk] / denom, route)
        route = jnp.where(lane == TOP_K + kk, slot, route)
    route_ref[0] = route
    row8 = lax.broadcasted_iota(jnp.int32, (8, LANES), 0)
    meta_ref[0, 0] = jnp.where(row8 == 0, count8, jnp.where(row8 == 1, seg_start, 0.0))


def _outproj(xa, mods, o_na, ry, bv, gate, gy, p_gd, pr, n_ctx):
    b, l, d = xa.shape
    nt = l // TM
    na_w = o_na.shape[-1]
    n_heads = ry[0].shape[1]
    rw_w = n_heads * HEAD_DIM
    kern = functools.partial(_outproj_kernel, na_w=na_w, rw_w=rw_w)
    tok = lambda w: pl.BlockSpec((1, TM, w), lambda i, t: (i, t, 0))
    head_major = pl.BlockSpec((1, n_heads, TM, HEAD_DIM), lambda i, t: (i, 0, t, 0))
    params = [pr['rw_ln_w'].reshape(1, rw_w), pr['rw_ln_b'].reshape(1, rw_w), pr['gd_norm_g'].reshape(1, HEAD_DIM),
              pr['w_out_bf'], pr['norm_ffn_g'], pr['router_w_hi'], pr['router_w_lo'], pr['router_b_p']]
    return pl.pallas_call(
        kern,
        grid=(b, nt),
        in_specs=[tok(d), pl.BlockSpec((1, 6, d), _mod_index(n_ctx // TM, b)), tok(na_w),
                  head_major, head_major, tok(rw_w), tok(rw_w), head_major, head_major,
                  pl.BlockSpec((1, TM, rw_w), lambda i, t: (i, t, 3))]
                 + [_full_spec(a) for a in params],
        out_specs=[tok(d), tok(d), tok(LANES), pl.BlockSpec((1, 1, 8, LANES), lambda i, t: (i, t, 0, 0))],
        out_shape=[jax.ShapeDtypeStruct((b, l, d), F32), jax.ShapeDtypeStruct((b, l, d), F32),
                   jax.ShapeDtypeStruct((b, l, LANES), F32), jax.ShapeDtypeStruct((b, nt, 8, LANES), F32)],
        compiler_params=_cparams(("arbitrary", "arbitrary")),
        name="outproj_router",
    )(xa, mods, o_na, ry[0], ry[1], bv, gate, gy[0], gy[1], p_gd, *params)


MOE_BM = 512
MOE_EXPERT_VMEM = 56 * 1024 * 1024
SEG_ALIGN = 8
SEG_BITS = (32, 16, 8, 4, 2, 1)


def _n_slots(n_exp):
    return _round_up(TOP_K * TM + n_exp * (SEG_ALIGN - 1), LANES)


def _segment_dma(segs_ref, buf_ref, hbm_ref, sem, n_exp, to_hbm, wait):
    def body(e, carry):
        start = segs_ref[0, 0, e]
        units = segs_ref[0, 0, n_exp + e]
        offset = segs_ref[0, 0, 2 * n_exp + e]
        for bit in SEG_BITS:
            done = (units & ~(2 * bit - 1)) * SEG_ALIGN
            rows = bit * SEG_ALIGN

            @pl.when((units & bit) != 0)
            def _():
                in_buf = buf_ref.at[pl.ds(pl.multiple_of(start + done, SEG_ALIGN), rows), :]
                in_hbm = hbm_ref.at[pl.ds(pl.multiple_of(offset + done, SEG_ALIGN), rows), :]
                copy = (pltpu.make_async_copy(in_buf, in_hbm, sem) if to_hbm
                        else pltpu.make_async_copy(in_hbm, in_buf, sem))
                if wait:
                    copy.wait()
                else:
                    copy.start()
        return carry
    lax.fori_loop(0, n_exp, body, 0)


def _segment_wait(segs_ref, buf_ref, hbm_ref, sem, n_exp, to_hbm):
    for b, bit in enumerate(SEG_BITS):
        rows = bit * SEG_ALIGN
        in_buf = buf_ref.at[pl.ds(0, rows), :]
        in_hbm = hbm_ref.at[pl.ds(0, rows), :]
        copy = (pltpu.make_async_copy(in_buf, in_hbm, sem) if to_hbm
                else pltpu.make_async_copy(in_hbm, in_buf, sem))

        def body(_, carry, copy=copy):
            copy.wait()
            return carry
        lax.fori_loop(0, segs_ref[0, 0, 3 * n_exp + b], body, 0)


def _slot_lanes(route, n_slots):
    lane = lax.broadcasted_iota(jnp.int32, (route.shape[0], n_slots), 1)
    slots = [route[:, TOP_K + k:TOP_K + k + 1].astype(jnp.int32) for k in range(TOP_K)]
    return lane, slots


def _dispatch_kernel(segs_ref, segs_prev_ref, gaps_ref, h_ref, route_ref, xb_ref, xs_ref, zero_ref, sem, gap_sem,
                     *, n_exp):
    i = pl.program_id(0)
    slot = i % 2

    @pl.when(i == 0)
    def _():
        zero_ref[...] = jnp.zeros_like(zero_ref)
        _segment_dma(gaps_ref, zero_ref, xb_ref, gap_sem, n_exp, to_hbm=True, wait=False)
        _segment_dma(gaps_ref, zero_ref, xb_ref, gap_sem, n_exp, to_hbm=True, wait=True)
        n_blocks = xb_ref.shape[0] // MOE_BM

        def block_copy(blk):
            return pltpu.make_async_copy(zero_ref, xb_ref.at[pl.ds(pl.multiple_of(blk * MOE_BM, MOE_BM), MOE_BM), :],
                                         gap_sem)

        def start(blk, carry):
            block_copy(blk).start()
            return carry

        def wait(blk, carry):
            block_copy(blk).wait()
            return carry

        lax.fori_loop(gaps_ref[0, 0, 3 * n_exp], n_blocks, start, 0)
        lax.fori_loop(gaps_ref[0, 0, 3 * n_exp], n_blocks, wait, 0)

    lane, slots = _slot_lanes(route_ref[...], xs_ref.shape[1])
    picked = lane == slots[0]
    for k in range(1, TOP_K):
        picked = picked | (lane == slots[k])
    onehot = jnp.where(picked, 1.0, 0.0).astype(BF16)
    xs_ref[slot] = _dot_tn(onehot, h_ref[...].astype(BF16))
    _segment_dma(segs_ref, xs_ref.at[slot], xb_ref, sem.at[slot], n_exp, to_hbm=True, wait=False)

    @pl.when(i > 0)
    def _():
        _segment_wait(segs_prev_ref, xs_ref.at[1 - slot], xb_ref, sem.at[1 - slot], n_exp, to_hbm=True)

    @pl.when(i == pl.num_programs(0) - 1)
    def _():
        _segment_wait(segs_ref, xs_ref.at[slot], xb_ref, sem.at[slot], n_exp, to_hbm=True)


def _seg_spec(index):
    return pl.BlockSpec((1, 1, LANES), index, memory_space=pltpu.SMEM)


def _dispatch(h2, route, segs, gaps, n_pad, n_exp):
    t, d = h2.shape
    nt = t // TM
    kern = functools.partial(_dispatch_kernel, n_exp=n_exp)
    return pl.pallas_call(
        kern,
        grid=(nt,),
        in_specs=[_seg_spec(lambda i: (i, 0, 0)),
                  _seg_spec(lambda i: (jnp.maximum(i - 1, 0), 0, 0)),
                  _seg_spec(lambda i: (0, 0, 0)),
                  pl.BlockSpec((TM, d), lambda i: (i, 0)),
                  pl.BlockSpec((TM, LANES), lambda i: (i, 0))],
        out_specs=pl.BlockSpec(memory_space=pl.ANY),
        out_shape=jax.ShapeDtypeStruct((n_pad, d), F32),
        scratch_shapes=[pltpu.VMEM((2, _n_slots(n_exp), d), F32), pltpu.VMEM((MOE_BM, d), F32),
                        pltpu.SemaphoreType.DMA((2,)), pltpu.SemaphoreType.DMA(())],
        compiler_params=_cparams(("arbitrary",)),
        name="moe_dispatch",
    )(segs, segs, gaps, h2, route)


def _expert_kernel(blk_e_ref, n_used_ref, x_ref, wgu_ref, bgu_ref, wd_ref, bd_ref, y_ref, wgu_bf, wd_bf):
    i = pl.program_id(0)
    used = i < n_used_ref[0]
    new_expert = (i == 0) | (blk_e_ref[i] != blk_e_ref[jnp.maximum(i - 1, 0)])

    @pl.when(used & new_expert)
    def _():
        wgu_bf[...] = wgu_ref[0, 0].astype(BF16)
        wd_bf[...] = wd_ref[0, 0].astype(BF16)

    @pl.when(used)
    def _():
        x = x_ref[...].astype(BF16)
        d_exp = wd_bf.shape[0]
        gu = _dot(x, wgu_bf[...]) + bgu_ref[0, 0]
        gate = jnp.minimum(gu[:, :d_exp], SWIGLU_LIMIT)
        lin = jnp.clip(gu[:, d_exp:], -SWIGLU_LIMIT, SWIGLU_LIMIT)
        act = gate * jax.nn.sigmoid(SWIGLU_ALPHA * gate) * (lin + 1.0)
        y_ref[...] = _dot(act.astype(BF16), wd_bf[...]) + bd_ref[0, 0]

    @pl.when(jnp.logical_not(used))
    def _():
        y_ref[...] = jnp.zeros_like(y_ref)


def _expert_blocks(xb, blk_e, n_used, layer, w_gu, b_gu, w_down, b_down):
    n_pad, d = xb.shape
    n_blocks = n_pad // MOE_BM
    depth, n_exp, _, d_gu = w_gu.shape
    grid_spec = pltpu.PrefetchScalarGridSpec(
        num_scalar_prefetch=2,
        grid=(n_blocks,),
        in_specs=[pl.BlockSpec((MOE_BM, d), lambda i, be, nu: (jnp.minimum(i, nu[0] - 1), 0)),
                  pl.BlockSpec((1, 1, d, d_gu), lambda i, be, nu: (layer, be[i], 0, 0)),
                  pl.BlockSpec((1, 1, 1, d_gu), lambda i, be, nu: (layer, be[i], 0, 0)),
                  pl.BlockSpec((1, 1, d_gu // 2, d), lambda i, be, nu: (layer, be[i], 0, 0)),
                  pl.BlockSpec((1, 1, 1, d), lambda i, be, nu: (layer, be[i], 0, 0))],
        out_specs=pl.BlockSpec((MOE_BM, d), lambda i, be, nu: (i, 0)),
        scratch_shapes=[pltpu.VMEM((d, d_gu), BF16), pltpu.VMEM((d_gu // 2, d), BF16)])
    return pl.pallas_call(
        _expert_kernel,
        grid_spec=grid_spec,
        out_shape=jax.ShapeDtypeStruct((n_pad, d), F32),
        compiler_params=pltpu.CompilerParams(dimension_semantics=("arbitrary",),
                                             vmem_limit_bytes=MOE_EXPERT_VMEM),
        name="moe_experts",
    )(blk_e, n_used, xb, w_gu, b_gu.reshape(depth, n_exp, 1, d_gu), w_down, b_down.reshape(depth, n_exp, 1, d))


def _combine_kernel(segs_ref, segs_next_ref, y_hbm, x_ref, mod_ref, route_ref, o_ref, ybuf, sem, *, n_exp):
    i = pl.program_id(0)
    slot = i % 2

    @pl.when(i == 0)
    def _():
        ybuf[...] = jnp.zeros_like(ybuf)
        _segment_dma(segs_ref, ybuf.at[0], y_hbm, sem.at[0], n_exp, to_hbm=False, wait=False)

    @pl.when(i + 1 < pl.num_programs(0))
    def _():
        _segment_dma(segs_next_ref, ybuf.at[1 - slot], y_hbm, sem.at[1 - slot], n_exp, to_hbm=False, wait=False)

    _segment_wait(segs_ref, ybuf.at[slot], y_hbm, sem.at[slot], n_exp, to_hbm=False)
    y = ybuf[slot].astype(BF16)
    route = route_ref[...]
    lane, slots = _slot_lanes(route, ybuf.shape[1])
    weights = jnp.zeros(lane.shape, F32)
    for k in range(TOP_K):
        weights = jnp.where(lane == slots[k], route[:, k:k + 1], weights)
    o_ref[...] = x_ref[...] + mod_ref[0, 5:6, :] * _dot(weights.astype(BF16), y)


def _combine(yb, segs, x1, mods, route, n_batch, l, n_ctx, n_exp, latent_only):
    t, d = x1.shape
    nt = t // TM
    per_sample = l // TM
    n_ctx_tiles = n_ctx // TM
    kern = functools.partial(_combine_kernel, n_exp=n_exp)

    def mod_index(i):
        return (jnp.where(i % per_sample < n_ctx_tiles, n_batch, i // per_sample), 0, 0)

    if latent_only:
        lat_tiles = per_sample - n_ctx_tiles
        out_rows = n_batch * lat_tiles * TM
        out_index = lambda i: ((i // per_sample) * lat_tiles + jnp.maximum(i % per_sample - n_ctx_tiles, 0), 0)
    else:
        out_rows = t
        out_index = lambda i: (i, 0)

    return pl.pallas_call(
        kern,
        grid=(nt,),
        in_specs=[_seg_spec(lambda i: (i, 0, 0)),
                  _seg_spec(lambda i: (jnp.minimum(i + 1, nt - 1), 0, 0)),
                  pl.BlockSpec(memory_space=pl.ANY),
                  pl.BlockSpec((TM, d), lambda i: (i, 0)),
                  pl.BlockSpec((1, 6, d), mod_index),
                  pl.BlockSpec((TM, LANES), lambda i: (i, 0))],
        out_specs=pl.BlockSpec((TM, d), out_index),
        out_shape=jax.ShapeDtypeStruct((out_rows, d), F32),
        scratch_shapes=[pltpu.VMEM((2, _n_slots(n_exp), d), F32), pltpu.SemaphoreType.DMA((2,))],
        compiler_params=_cparams(("arbitrary",)),
        name="moe_combine",
    )(segs, segs, yb, x1, mods, route)


def _seg_table(start, units, offset, extra=None):
    n = start.shape[0]
    cols = [start, units, offset] + ([] if extra is None else [extra])
    pad = jnp.zeros((n, LANES - sum(c.shape[1] for c in cols)), jnp.int32)
    return jnp.concatenate(cols + [pad], axis=1).reshape(n, 1, LANES)


def _route(meta, n_exp, n_tokens):
    nt = meta.shape[0]
    count8 = meta[:, 0, :n_exp].astype(jnp.int32)
    seg_start = meta[:, 1, :n_exp].astype(jnp.int32)
    per_expert = jnp.sum(count8, axis=0)
    padded = (per_expert + MOE_BM - 1) // MOE_BM * MOE_BM
    pad_end = jnp.cumsum(padded)
    pad_start = pad_end - padded
    offset = pad_start[None, :] + jnp.cumsum(count8, axis=0) - count8
    n_pad = _round_up(n_tokens * TOP_K + nt * n_exp * (SEG_ALIGN - 1), MOE_BM) + n_exp * MOE_BM
    block_row = jnp.arange(n_pad // MOE_BM, dtype=jnp.int32) * MOE_BM
    blk_e = jnp.minimum(jnp.sum(block_row[:, None] >= pad_end[None, :], axis=1), n_exp - 1).astype(jnp.int32)
    n_used = (pad_end[-1:] // MOE_BM).astype(jnp.int32)
    units = count8 // SEG_ALIGN
    copies = jnp.stack([jnp.sum((units // bit) % 2, axis=1) for bit in SEG_BITS], axis=1)
    segs = _seg_table(seg_start, units, offset, copies.astype(jnp.int32))
    gaps = _seg_table(jnp.zeros((1, n_exp), jnp.int32), ((padded - per_expert) // SEG_ALIGN)[None, :],
                      (pad_start + per_expert)[None, :], n_used[None, :])
    return segs, gaps, blk_e, n_used, n_pad


def _channel_sublayer(x1, h2, route, meta, mods, layer, experts, n_ctx, latent_only):
    b, l, d = x1.shape
    t = b * l
    n_exp = experts[0].shape[1]
    assert 3 * n_exp + len(SEG_BITS) <= LANES and MOE_BM // SEG_ALIGN <= 2 * SEG_BITS[0]
    route = route.reshape(t, LANES)
    segs, gaps, blk_e, n_used, n_pad = _route(meta.reshape(t // TM, 8, LANES), n_exp, t)
    xb = _dispatch(h2.reshape(t, d), route, segs, gaps, n_pad, n_exp)
    yb = _expert_blocks(xb, blk_e, n_used, layer, *experts)
    x2 = _combine(yb, segs, x1.reshape(t, d), mods, route, b, l, n_ctx, n_exp, latent_only)
    return x2.reshape(b, -1, d)


def _pad_cols(w, n):
    return jnp.pad(w, ((0, 0), (0, n - w.shape[1])))


def _round_up(n, m):
    return -(-n // m) * m


def _prep_layer_params(pr):
    d = pr['w_in'].shape[0]
    na_w = d // 2
    rw_w = d // 4
    gd_w = d - na_w - rw_w
    na_cols = 3 * na_w
    rw_cols = 3 * rw_w + RW_DECAY_LORA + RW_ICLR_LORA + RW_GATE_LORA
    rw_pad = _round_up(rw_cols, LANES)
    gd_cols = pr['w_in'].shape[1] - na_cols - rw_cols
    gd_pad = _round_up(gd_cols, LANES)
    w_in = pr['w_in']
    out = dict(pr)
    out['w_in'] = jnp.concatenate([w_in[:, :na_cols],
                                   _pad_cols(w_in[:, na_cols:na_cols + rw_cols], rw_pad),
                                   _pad_cols(w_in[:, na_cols + rw_cols:], gd_pad)], axis=1).astype(BF16)
    out['rw_w'] = rw_pad
    out['gd_w'] = gd_pad
    n_heads = na_w // HEAD_DIM
    out['qkg'] = jnp.stack([jnp.tile(pr['na_q_gain'], n_heads) * HEAD_DIM ** -0.5,
                            jnp.tile(pr['na_k_gain'], n_heads)])
    head_of = np.arange(na_w) // HEAD_DIM
    out['avg'] = jnp.asarray((head_of[:, None] == head_of[None, :]) / HEAD_DIM, BF16)
    out['norm_mix_g'] = pr['norm_mix_g'].reshape(1, d)
    out['norm_ffn_g'] = pr['norm_ffn_g'].reshape(1, d)
    out['rw_mu_p'] = jnp.pad(pr['rw_mu'], (0, rw_pad - rw_cols)).reshape(1, rw_pad)
    o1, o2 = RW_DECAY_LORA, RW_DECAY_LORA + RW_ICLR_LORA
    out['rw_wup_p'] = jnp.pad(pr['rw_w_up'], ((0, 0), (0, LANES - o1), (0, 0)))
    out['rw_aup_p'] = jnp.pad(pr['rw_a_up'], ((0, 0), (o1, LANES - o2), (0, 0)))
    out['rw_gup_p'] = jnp.pad(pr['rw_g_up'], ((o2, LANES - o2 - RW_GATE_LORA), (0, 0)))
    head_of = np.arange(rw_w) // HEAD_DIM
    out['ones_bd'] = jnp.asarray(head_of[:, None] == head_of[None, :], BF16)
    out['gd_width'] = gd_w
    n_ab = pr['gd_A_log'].size
    out['gd_alog_p'] = jnp.pad(pr['gd_A_log'].reshape(-1), (0, LANES - n_ab)).reshape(1, LANES)
    out['gd_dtb_p'] = jnp.pad(pr['gd_dt_bias'].reshape(-1), (0, LANES - n_ab)).reshape(1, LANES)
    out['w_out_bf'] = pr['w_out'].astype(BF16)
    n_exp = pr['moe_router_w'].shape[1]
    router_w = _pad_cols(pr['moe_router_w'], LANES)
    out['router_w_hi'] = router_w.astype(BF16)
    out['router_w_lo'] = (router_w - out['router_w_hi'].astype(F32)).astype(BF16)
    out['router_b_p'] = jnp.concatenate([pr['moe_router_b'],
                                         jnp.full((LANES - n_exp,), MASKED, F32)]).reshape(1, LANES)
    return out


def _mixing_sublayer(xa, mods, pr, n_ctx):
    b, l, d = xa.shape
    na_w = d // 2
    p_na, p_rw, p_gd = _inproj(xa, mods, pr['norm_mix_g'], pr['w_in'], pr['qkg'], pr['avg'], n_ctx,
                               na_w, pr['rw_w'], pr['gd_w'])
    o_na = _na_attention(p_na, pr['na_bias'], n_ctx, na_w)
    rw2, ry1, rm, rn, bv, gate = _rwkv_prepare(p_rw, pr, n_ctx)
    ry = _scan(rw2, ry1, rm, rn, n_ctx)
    gy = _scan(*_gdn_prepare(p_gd, pr, n_ctx), n_ctx)
    return _outproj(xa, mods, o_na, ry, bv, gate, gy, p_gd, pr, n_ctx)


_LAYER_PARAMS = ('norm_mix_g', 'norm_ffn_g', 'w_in', 'w_out', 'na_q_gain', 'na_k_gain', 'na_rpb',
                 'rw_mu', 'rw_w0', 'rw_w_up', 'rw_a0', 'rw_a_up', 'rw_g_up', 'rw_k_k', 'rw_k_a', 'rw_r_k',
                 'rw_ln_w', 'rw_ln_b', 'gd_conv_w', 'gd_A_log', 'gd_dt_bias', 'gd_norm_g',
                 'moe_router_w', 'moe_router_b', 'moe_w_gu', 'moe_b_gu', 'moe_w_down', 'moe_b_down')


def kernel(x, c, ctx, c_ctx, ada_w, ada_b, norm_mix_g, norm_ffn_g, w_in, w_out, na_q_gain, na_k_gain, na_rpb, rw_mu, rw_w0, rw_w_up, rw_a0, rw_a_up, rw_g_up, rw_k_k, rw_k_a, rw_r_k, rw_ln_w, rw_ln_b, gd_conv_w, gd_A_log, gd_dt_bias, gd_norm_g, moe_router_w, moe_router_b, moe_w_gu, moe_b_gu, moe_w_down, moe_b_down):
    stacked = dict(zip(_LAYER_PARAMS, (norm_mix_g, norm_ffn_g, w_in, w_out, na_q_gain, na_k_gain, na_rpb,
                                       rw_mu, rw_w0, rw_w_up, rw_a0, rw_a_up, rw_g_up, rw_k_k, rw_k_a, rw_r_k,
                                       rw_ln_w, rw_ln_b, gd_conv_w, gd_A_log, gd_dt_bias, gd_norm_g,
                                       moe_router_w, moe_router_b, moe_w_gu, moe_b_gu, moe_w_down, moe_b_down)))
    b, s, d = x.shape
    n_ctx = ctx.shape[1]
    depth = ada_w.shape[0]
    assert b + 1 <= 8 and n_ctx % TM == 0 and s % TM == 0 and s % GRID_W == 0
    cvec = jnp.zeros((8, d), F32).at[:b].set(c).at[b].set(c_ctx)
    mods_all = _adaln(cvec, ada_w, ada_b).reshape(depth, 8, 6, d)[:, :b + 1]
    xa = jnp.concatenate([ctx, x], axis=1)
    experts = (moe_w_gu, moe_b_gu, moe_w_down, moe_b_down)
    for layer in range(depth):
        pr = _prep_layer_params({name: value[layer] for name, value in stacked.items()
                                 if not name.startswith('moe_w_') and not name.startswith('moe_b_')})
        pr['na_bias'] = _na_bias_tables(pr['na_rpb'], s // GRID_W)
        mods = mods_all[layer]
        x1, h2, route, meta = _mixing_sublayer(xa, mods, pr, n_ctx)
        xa = _channel_sublayer(x1, h2, route, meta, mods, layer, experts, n_ctx,
                               latent_only=(layer == depth - 1))
    return xa
```

```python
import functools
import math

import numpy as np
import jax
import jax.numpy as jnp
from jax import lax
from jax.experimental import pallas as pl
from jax.experimental.pallas import tpu as pltpu

F32 = jnp.float32
BF16 = jnp.bfloat16
HIGHEST = lax.Precision.HIGHEST

GRID_W = 64
NORM_EPS = 1e-6
HEAD_DIM = 64
NA_WIN_ROWS = 8
NA_WIN_COLS = 16
RW_DECAY_LORA = 32
RW_ICLR_LORA = 32
RW_GATE_LORA = 64
RW_GN_EPS = 64e-5
GD_CONV = 5
TOP_K = 4
SWIGLU_LIMIT = 7.0
SWIGLU_ALPHA = 1.702

LANES = 128
TM = 256
NA_QROWS = 4
NA_KROWS = NA_QROWS + NA_WIN_ROWS
CHUNK = 64
CHUNKS_PER_STEP = 4
MASKED = -1e30
VMEM_LIMIT = 48 * 1024 * 1024


def _cparams(sem):
    return pltpu.CompilerParams(dimension_semantics=sem, vmem_limit_bytes=VMEM_LIMIT)


def _dot(a, b, **kw):
    return jnp.dot(a, b, preferred_element_type=F32, **kw)


def _dot_split(a, b_bf16):
    hi = a.astype(BF16)
    lo = (a - hi.astype(F32)).astype(BF16)
    return _dot(hi, b_bf16) + _dot(lo, b_bf16)


def _split_bf16(a):
    hi = a.astype(BF16)
    return hi, (a - hi.astype(F32)).astype(BF16)


def _dot_split3(a, b):
    a_hi = a.astype(BF16)
    a_lo = (a - a_hi.astype(F32)).astype(BF16)
    b_hi = b.astype(BF16)
    b_lo = (b - b_hi.astype(F32)).astype(BF16)
    return _dot(a_hi, b_hi) + (_dot(a_hi, b_lo) + _dot(a_lo, b_hi))


def _dot_nt(a, b, **kw):
    return lax.dot_general(a, b, (((1,), (1,)), ((), ())), preferred_element_type=F32, **kw)


def _dot_tn(a, b, **kw):
    return lax.dot_general(a, b, (((0,), (0,)), ((), ())), preferred_element_type=F32, **kw)


def _adaln_kernel(c_ref, w_ref, b_ref, o_ref):
    c = c_ref[...]
    s = c * jax.nn.sigmoid(c)
    o_ref[0] = _dot(s.astype(BF16), w_ref[0].astype(BF16)) + b_ref[0]


def _adaln(cvec, ada_w, ada_b):
    depth, d, n = ada_w.shape
    tn = n // 4
    return pl.pallas_call(
        _adaln_kernel,
        grid=(depth, n // tn),
        in_specs=[pl.BlockSpec((8, d), lambda l, j: (0, 0)),
                  pl.BlockSpec((1, d, tn), lambda l, j: (l, 0, j)),
                  pl.BlockSpec((1, 1, tn), lambda l, j: (l, 0, j))],
        out_specs=pl.BlockSpec((1, 8, tn), lambda l, j: (l, 0, j)),
        out_shape=jax.ShapeDtypeStruct((depth, 8, n), F32),
        compiler_params=_cparams(("arbitrary", "arbitrary")),
        name="adaln",
    )(cvec, ada_w, ada_b.reshape(depth, 1, n))


def _mod_index(n_ctx_tiles, n_batch):
    return lambda b, t: (jnp.where(t < n_ctx_tiles, n_batch, b), 0, 0)


def _rms(x):
    return x * lax.rsqrt(jnp.mean(x * x, axis=-1, keepdims=True) + NORM_EPS)


def _inproj_kernel(x_ref, mod_ref, g_ref, w_ref, qkg_ref, avg_ref, na_ref, rw_ref, gd_ref, *, na_w, rw_w):
    mod = mod_ref[0]
    h = _rms(x_ref[0]) * g_ref[...]
    hb = (h * (1.0 + mod[1:2]) + mod[0:1]).astype(BF16)
    pa = _dot(hb, w_ref[:, 0:3 * na_w])
    avg = avg_ref[...]
    qkg = qkg_ref[...]
    q = pa[:, 0:na_w]
    k = pa[:, na_w:2 * na_w]
    qn = q * lax.rsqrt(_dot(q * q, avg) + NORM_EPS) * qkg[0:1]
    kn = k * lax.rsqrt(_dot(k * k, avg) + NORM_EPS) * qkg[1:2]
    na_ref[0, :, 0:na_w] = qn.astype(BF16)
    na_ref[0, :, na_w:2 * na_w] = kn.astype(BF16)
    na_ref[0, :, 2 * na_w:3 * na_w] = pa[:, 2 * na_w:3 * na_w].astype(BF16)
    rw_ref[0] = _dot(hb, w_ref[:, 3 * na_w:3 * na_w + rw_w])
    gd_ref[0] = _dot(hb, w_ref[:, 3 * na_w + rw_w:])


def _inproj(xa, mods, g, w_all, qkg, avg, n_ctx, na_w, rw_w, gd_w):
    b, l, d = xa.shape
    nt = l // TM
    kern = functools.partial(_inproj_kernel, na_w=na_w, rw_w=rw_w)
    return pl.pallas_call(
        kern,
        grid=(b, nt),
        in_specs=[pl.BlockSpec((1, TM, d), lambda i, t: (i, t, 0)),
                  pl.BlockSpec((1, 6, d), _mod_index(n_ctx // TM, b)),
                  pl.BlockSpec((1, d), lambda i, t: (0, 0)),
                  pl.BlockSpec(w_all.shape, lambda i, t: (0, 0)),
                  pl.BlockSpec(qkg.shape, lambda i, t: (0, 0)),
                  pl.BlockSpec(avg.shape, lambda i, t: (0, 0))],
        out_specs=[pl.BlockSpec((1, TM, 3 * na_w), lambda i, t: (i, t, 0)),
                   pl.BlockSpec((1, TM, rw_w), lambda i, t: (i, t, 0)),
                   pl.BlockSpec((1, TM, gd_w), lambda i, t: (i, t, 0))],
        out_shape=[jax.ShapeDtypeStruct((b, l, 3 * na_w), BF16),
                   jax.ShapeDtypeStruct((b, l, rw_w), F32),
                   jax.ShapeDtypeStruct((b, l, gd_w), F32)],
        compiler_params=_cparams(("arbitrary", "arbitrary")),
        name="inproj",
    )(xa, mods, g, w_all, qkg, avg)


def _na_bias_tables(rpb, rows):
    n_heads = rpb.shape[0]
    nq, nk = NA_QROWS * GRID_W, NA_KROWS * GRID_W
    qc = np.arange(GRID_W)[:, None]
    kc = np.arange(GRID_W)[None, :]
    c0 = np.clip(qc - NA_WIN_COLS // 2, 0, GRID_W - NA_WIN_COLS)
    col_ok = (kc >= c0) & (kc < c0 + NA_WIN_COLS)
    dc = kc - qc + NA_WIN_COLS - 1
    pick_col = (dc[None] == np.arange(2 * NA_WIN_COLS - 1)[:, None, None]).astype(np.float32)
    by_col = jnp.einsum('hrd,dqk->hrqk', rpb, pick_col, precision=HIGHEST)
    tables = []
    for r0 in (0, NA_QROWS, rows - NA_QROWS):
        ks = int(np.clip(r0 - NA_WIN_ROWS // 2, 0, rows - NA_KROWS))
        qr = (r0 + np.arange(NA_QROWS))[:, None]
        kr = (ks + np.arange(NA_KROWS))[None, :]
        s0 = np.clip(qr - NA_WIN_ROWS // 2, 0, rows - NA_WIN_ROWS)
        row_ok = (kr >= s0) & (kr < s0 + NA_WIN_ROWS)
        dr = kr - qr + NA_WIN_ROWS - 1
        pick_row = ((dr[None] == np.arange(2 * NA_WIN_ROWS - 1)[:, None, None]) & row_ok[None]).astype(np.float32)
        bias = jnp.einsum('rab,hrqk->haqbk', pick_row, by_col, precision=HIGHEST)
        ok = row_ok[:, None, :, None] & col_ok[None, :, None, :]
        tables.append(jnp.where(ok[None], bias, MASKED).reshape(n_heads, nq, nk))
    tables.append(jnp.full((n_heads, nq, nk), MASKED, F32))
    return jnp.stack(tables).reshape(4, n_heads // 2, 2, nq, nk)


def _na_kernel(q_ref, k_ref, v_ref, bias_ref, o_ref, *, n_ctx, rows):
    j = pl.program_id(2)
    nk = NA_KROWS * GRID_W
    ks_row = jnp.clip((j - 1) * NA_QROWS - NA_WIN_ROWS // 2, 0, rows - NA_KROWS)
    kstart = pl.multiple_of(n_ctx + ks_row * GRID_W, GRID_W)
    q = q_ref[0]
    kl = k_ref[0, pl.ds(kstart, nk), :]
    vl = v_ref[0, pl.ds(kstart, nk), :]
    kc = k_ref[0, 0:n_ctx, :]
    vc = v_ref[0, 0:n_ctx, :]
    lane = lax.broadcasted_iota(jnp.int32, q.shape, 1)
    outs = []
    for h in range(2):
        in_head = (lane >= h * HEAD_DIM) & (lane < (h + 1) * HEAD_DIM)
        qh = jnp.where(in_head, q, jnp.zeros_like(q))
        s_loc = _dot_nt(qh, kl) + bias_ref[0, 0, h]
        s_ctx = _dot_nt(qh, kc)
        m = jnp.maximum(jnp.max(s_loc, axis=-1, keepdims=True), jnp.max(s_ctx, axis=-1, keepdims=True))
        p_loc = jnp.exp(s_loc - m)
        p_ctx = jnp.exp(s_ctx - m)
        den = jnp.sum(p_loc, axis=-1, keepdims=True) + jnp.sum(p_ctx, axis=-1, keepdims=True)
        o = _dot(p_loc.astype(BF16), vl) + _dot(p_ctx.astype(BF16), vc)
        outs.append(o / den)
    o_ref[0] = jnp.where(lane < HEAD_DIM, outs[0], outs[1]).astype(o_ref.dtype)


def _na_attention(p_na, bias, n_ctx, na_w):
    b, l, _ = p_na.shape
    rows = (l - n_ctx) // GRID_W
    nq = NA_QROWS * GRID_W
    assert n_ctx == nq and rows >= NA_KROWS and rows % NA_QROWS == 0
    n_pairs = na_w // LANES
    nblk = l // nq
    kern = functools.partial(_na_kernel, n_ctx=n_ctx, rows=rows)

    def bias_index(i, hp, j):
        pat = jnp.where(j == 0, 3, jnp.where(j == 1, 0, jnp.where(j == nblk - 1, 2, 1)))
        return (pat, hp, 0, 0, 0)

    return pl.pallas_call(
        kern,
        grid=(b, n_pairs, nblk),
        in_specs=[pl.BlockSpec((1, nq, LANES), lambda i, hp, j: (i, j, hp)),
                  pl.BlockSpec((1, l, LANES), lambda i, hp, j: (i, 0, n_pairs + hp)),
                  pl.BlockSpec((1, l, LANES), lambda i, hp, j: (i, 0, 2 * n_pairs + hp)),
                  pl.BlockSpec((1, 1, 2) + bias.shape[3:], bias_index)],
        out_specs=pl.BlockSpec((1, nq, LANES), lambda i, hp, j: (i, j, hp)),
        out_shape=jax.ShapeDtypeStruct((b, l, na_w), BF16),
        compiler_params=_cparams(("arbitrary", "arbitrary", "arbitrary")),
        name="na_attention",
    )(p_na, p_na, p_na, bias)


def _tri_masks(reverse):
    t = lax.broadcasted_iota(jnp.int32, (CHUNK, CHUNK), 0)
    s = lax.broadcasted_iota(jnp.int32, (CHUNK, CHUNK), 1)
    earlier_or_same = (s >= t) if reverse else (s <= t)
    earlier = (s > t) if reverse else (s < t)
    return earlier_or_same, earlier, t == s


def _each(fn, *lists):
    return [fn(*args) for args in zip(*lists)]


def _bf16(values):
    return [u.astype(BF16) for u in values]


def _head_lanes(width):
    lane = lax.broadcasted_iota(jnp.int32, (1, width), 1)
    return [(lane >= h * HEAD_DIM) & (lane < (h + 1) * HEAD_DIM) for h in range(width // HEAD_DIM)]


def _stack_heads(x, heads):
    return jnp.concatenate([jnp.where(in_head, x, 0.0) for in_head in heads], axis=0).astype(BF16)


def _tiled_masks(n_heads, reverse):
    t = lax.broadcasted_iota(jnp.int32, (CHUNK, n_heads * CHUNK), 0)
    s = lax.broadcasted_iota(jnp.int32, (CHUNK, n_heads * CHUNK), 1) % CHUNK
    earlier = (s > t) if reverse else (s < t)
    return earlier, earlier | (s == t), s == t


def _unit_tri_inverse(a, eye, heads):
    assert CHUNK == 64
    parts = lambda values: [_split_bf16(u) for u in values]
    diag = lambda pairs: [(_stack_heads(hi, heads), _stack_heads(lo, heads)) for hi, lo in pairs]
    mul = lambda u, w: _dot(u[0], w[0]) + (_dot(u[0], w[1]) + _dot(u[1], w[0]))
    x1 = [-u for u in a]
    x1b = parts(x1)
    x1d = diag(x1b)
    x2 = _each(mul, x1b, x1d)
    x2b = parts(x2)
    x2d = diag(x2b)
    x4 = _each(mul, x2b, x2d)
    x3 = _each(mul, x1b, x2d)
    x4b = parts(x4)
    x4d = diag(x4b)
    x8 = _each(mul, x4b, x4d)
    p1 = _each(lambda u1, u2, u3: eye + u1 + u2 + u3, x1, x2, x3)
    x8b = parts(x8)
    x8d = diag(x8b)
    x12 = _each(mul, x4b, x8d)
    x16 = _each(mul, x8b, x8d)
    p2 = _each(lambda u4, u8, u12: eye + u4 + u8 + u12, x4, x8, x12)
    p12 = _each(mul, parts(p1), diag(parts(p2)))
    x16b = parts(x16)
    x16d = diag(x16b)
    x32 = _each(mul, x16b, x16d)
    x48 = _each(mul, x16b, diag(parts(x32)))
    p3 = _each(lambda u16, u32, u48: eye + u16 + u32 + u48, x16, x32, x48)
    return _each(mul, parts(p12), diag(parts(p3)))


def _chunk_matrices(a_ka, a_kk, a_ra, a_rk, ks, rt, v, kd, ad, eye, heads):
    stack = lambda values: [_stack_heads(u, heads) for u in values]
    t_inv = [_split_bf16(u) for u in _unit_tri_inverse(a_ka, eye.astype(F32), heads)]
    vs, a_rab, adb = stack(v), _bf16(a_ra), _bf16(ad)
    av = _each(_dot, _bf16(a_kk), vs)
    w1 = _each(lambda t, k: _dot(t[0], k) + _dot(t[1], k), t_inv, ks)
    u0 = _each(lambda t, k: _dot(t[0], k) + _dot(t[1], k), t_inv, stack(av))
    rkv = _each(_dot, _bf16(a_rk), vs)
    w2 = _each(lambda r, a, w: r - _dot(a, w), rt, a_rab, stack(w1))
    y1 = _each(lambda y, a, u: y - _dot(a, u), rkv, a_rab, stack(u0))
    mm = _each(_dot_tn, adb, _bf16(w1))
    nn = _each(lambda kd_, v_, ad_, u: _dot_tn(kd_, v_) - _dot_tn(ad_, u), _bf16(kd), _bf16(v), adb, _bf16(u0))
    return list(zip(w2, y1, mm, nn))


def _store_chunk(refs, d, c0, n_heads, w2, y1, mm, nn, p_end):
    w2_ref, y1_ref, m_ref, n_ref = refs
    rows = pl.ds(c0, CHUNK)
    eye = (lax.broadcasted_iota(jnp.int32, (HEAD_DIM, HEAD_DIM), 0)
           == lax.broadcasted_iota(jnp.int32, (HEAD_DIM, HEAD_DIM), 1))
    for h in range(n_heads):
        lo, hi = h * HEAD_DIM, (h + 1) * HEAD_DIM
        w2_ref[d, 0, h, rows, :] = w2[:, lo:hi]
        y1_ref[d, 0, h, rows, :] = y1[:, lo:hi]
        m_ref[d, 0, h, rows, :] = jnp.where(eye, p_end[:, lo:hi], 0.0) - mm[lo:hi, lo:hi]
        n_ref[d, 0, h, rows, :] = nn[lo:hi, lo:hi]


def _scan_out_specs(b, n_heads, l):
    shape = jax.ShapeDtypeStruct((2, b, n_heads, l, HEAD_DIM), F32)
    spec = pl.BlockSpec((2, 1, n_heads, TM, HEAD_DIM), lambda i, t: (0, i, 0, t, 0))
    return [shape] * 4, [spec] * 4


def _seq_edges(t, n_ctx_tiles, n_tiles):
    first = (t == 0) | (t == n_ctx_tiles)
    last = (t == n_ctx_tiles - 1) | (t == n_tiles - 1)
    return first, last


def _fill_halo(ext_ref, x_ref, prev_ref, next_ref, first, last):
    ext_ref[0:8, :] = jnp.where(first, 0.0, prev_ref[0])
    ext_ref[8:8 + TM, :] = x_ref[0]
    ext_ref[8 + TM:16 + TM, :] = jnp.where(last, 0.0, next_ref[0])


def _halo_specs(width, n_tiles):
    per = TM // 8
    return [pl.BlockSpec((1, TM, width), lambda i, t: (i, t, 0)),
            pl.BlockSpec((1, 8, width), lambda i, t: (i, jnp.maximum(t * per - 1, 0), 0)),
            pl.BlockSpec((1, 8, width), lambda i, t: (i, jnp.minimum((t + 1) * per, n_tiles * per - 1), 0))]


def _full_spec(a):
    return pl.BlockSpec(a.shape, lambda i, t: (0,) * a.ndim)


def _rwkv_prep_kernel(x_ref, prev_ref, next_ref, mu_ref, w0_ref, wup_ref, a0_ref, aup_ref, gup_ref,
                      kk_ref, ka_ref, rk_ref, ones_ref,
                      w2_ref, y1_ref, m_ref, n_ref, bv_ref, gate_ref,
                      ext_ref, tok_ref, *, n_ctx_tiles, n_tiles, width):
    t = pl.program_id(1)
    first, last = _seq_edges(t, n_ctx_tiles, n_tiles)
    _fill_halo(ext_ref, x_ref, prev_ref, next_ref, first, last)
    p = ext_ref[8:8 + TM, :]
    prev = ext_ref[7:7 + TM, :]
    nxt = ext_ref[9:9 + TM, :]
    ps = p + mu_ref[...] * (0.5 * (prev + nxt) - p)
    r = ps[:, 0:width]
    k = ps[:, width:2 * width]
    v = ps[:, 2 * width:3 * width]
    lo = ps[:, 3 * width:3 * width + LANES]
    ones_bd = ones_ref[...]
    gate_ref[0] = _dot(jax.nn.sigmoid(lo), gup_ref[...])
    kq = k * kk_ref[...]
    kk = kq * lax.rsqrt(_dot_split(kq * kq, ones_bd) + NORM_EPS)
    tanh_lo = jnp.tanh(lo)
    tok_ref[0] = r
    tok_ref[1] = kk
    tok_ref[2] = v
    k_sum = jnp.zeros_like(k)
    for d in range(2):
        z = w0_ref[d:d + 1, :] + _dot(tanh_lo, wup_ref[d])
        w_log = -jax.nn.softplus(-z) - 0.5
        a = jax.nn.sigmoid(a0_ref[d:d + 1, :] + _dot(lo, aup_ref[d]))
        k_dir = k * (1.0 + (a - 1.0) * ka_ref[...])
        k_sum = k_sum + k_dir
        tok_ref[3 + 3 * d] = -jnp.exp(w_log)
        tok_ref[4 + 3 * d] = kk * a
        tok_ref[5 + 3 * d] = k_dir
    bonus = _dot_split(r * k_sum * rk_ref[...], ones_bd)
    bv_ref[0] = bonus * v

    heads = _head_lanes(width)
    out_refs = (w2_ref, y1_ref, m_ref, n_ref)

    def chunk_body(c, carry):
        args = [[] for _ in range(9)]
        where = []
        for cc in range(CHUNKS_PER_STEP):
            c0 = pl.multiple_of((c * CHUNKS_PER_STEP + cc) * CHUNK, CHUNK)
            rows = pl.ds(c0, CHUNK)
            r_c, kk_c, v_c = tok_ref[0, rows, :], tok_ref[1, rows, :], tok_ref[2, rows, :]
            for d in range(2):
                logw, alpha, k_c = (tok_ref[3 + 3 * d, rows, :], tok_ref[4 + 3 * d, rows, :],
                                    tok_ref[5 + 3 * d, rows, :])
                incl, _, _ = _tri_masks(reverse=(d == 1))
                ci = _dot(incl.astype(F32), logw, precision=HIGHEST)
                e_ci, e_nci, e_ce = jnp.exp(ci), jnp.exp(-ci), jnp.exp(ci - logw)
                p_end = e_ci[0:1, :] if d == 1 else e_ci[CHUNK - 1:CHUNK, :]
                kt, rt, kh, ah = kk_c * e_ce, r_c * e_ci, k_c * e_nci, alpha * e_nci
                earlier, earlier_or_same, eye = _tiled_masks(len(heads), reverse=(d == 1))
                khs, ahs = _stack_heads(kh, heads), _stack_heads(ah, heads)
                ktb, rtb = kt.astype(BF16), rt.astype(BF16)
                a_ka = jnp.where(earlier, _dot_nt(ktb, ahs), 0.0)
                a_kk = jnp.where(earlier, _dot_nt(ktb, khs), 0.0)
                a_ra = jnp.where(earlier_or_same, _dot_nt(rtb, ahs), 0.0)
                a_rk = jnp.where(earlier_or_same, _dot_nt(rtb, khs), 0.0)
                values = (a_ka, a_kk, a_ra, a_rk, _stack_heads(kt, heads), rt, v_c, kh * p_end, ah * p_end)
                for slot, value in zip(args, values):
                    slot.append(value)
                where.append((d, c0, p_end))
        for (d, c0, p_end), res in zip(where, _chunk_matrices(*args, eye, heads)):
            _store_chunk(out_refs, d, c0, len(heads), *res, p_end)
        return carry

    lax.fori_loop(0, TM // CHUNK // CHUNKS_PER_STEP, chunk_body, 0)


def _rwkv_prepare(p_rw, pr, n_ctx):
    b, l, pw = p_rw.shape
    width = pr['rw_k_k'].shape[-1]
    n_heads = width // HEAD_DIM
    nt = l // TM
    kern = functools.partial(_rwkv_prep_kernel, n_ctx_tiles=n_ctx // TM, n_tiles=nt, width=width)
    params = [pr['rw_mu_p'], pr['rw_w0'], pr['rw_wup_p'], pr['rw_a0'], pr['rw_aup_p'], pr['rw_gup_p'],
              pr['rw_k_k'].reshape(1, width), pr['rw_k_a'].reshape(1, width), pr['rw_r_k'].reshape(1, width),
              pr['ones_bd']]
    scan_shapes, scan_specs = _scan_out_specs(b, n_heads, l)
    tok_spec = pl.BlockSpec((1, TM, width), lambda i, t: (i, t, 0))
    return pl.pallas_call(
        kern,
        grid=(b, nt),
        in_specs=_halo_specs(pw, nt) + [_full_spec(a) for a in params],
        out_specs=scan_specs + [tok_spec, tok_spec],
        out_shape=scan_shapes + [jax.ShapeDtypeStruct((b, l, width), F32)] * 2,
        scratch_shapes=[pltpu.VMEM((TM + 16, pw), F32), pltpu.VMEM((9, TM, width), F32)],
        compiler_params=_cparams(("arbitrary", "arbitrary")),
        name="rwkv_prepare",
    )(p_rw, p_rw, p_rw, *params)


def _scan_kernel(w2f_ref, y1f_ref, mf_ref, nf_ref, w2b_ref, y1b_ref, mb_ref, nb_ref, yf_ref, yb_ref, h_ref):
    @pl.when(pl.program_id(0) == 0)
    def _():
        h_ref[...] = jnp.zeros_like(h_ref)

    n_batch, n_heads = h_ref.shape[1], h_ref.shape[2]
    n_sub = yf_ref.shape[2] // CHUNK
    dirs = ((w2f_ref, y1f_ref, mf_ref, nf_ref, yf_ref), (w2b_ref, y1b_ref, mb_ref, nb_ref, yb_ref))
    chains = [(d, i, h) for d in range(2) for i in range(n_batch) for h in range(n_heads)]
    states = [h_ref[d, i, h] for d, i, h in chains]
    for j in range(n_sub):
        for c, (d, i, h) in enumerate(chains):
            w2_ref, y1_ref, m_ref, n_ref, y_ref = dirs[d]
            jj = j if d == 0 else n_sub - 1 - j
            rows = slice(jj * CHUNK, (jj + 1) * CHUNK)
            y_ref[i, h, rows, :] = _dot(w2_ref[0, i, h, rows, :], states[c]) + y1_ref[0, i, h, rows, :]
            states[c] = _dot_split3(m_ref[0, i, h, rows, :], states[c]) + n_ref[0, i, h, rows, :]
    for c, (d, i, h) in enumerate(chains):
        h_ref[d, i, h] = states[c]


def _scan(w2, y1, m, n, n_ctx):
    _, b, n_heads, l, _ = w2.shape
    nc = l // TM
    ncc = n_ctx // TM

    def bwd_chunk(i):
        return jnp.where(i < ncc, ncc - 1 - i, nc - 1 + ncc - i)

    blk = (1, b, n_heads, TM, HEAD_DIM)
    fwd = pl.BlockSpec(blk, lambda i: (0, 0, 0, i, 0))
    bwd = pl.BlockSpec(blk, lambda i: (1, 0, 0, bwd_chunk(i), 0))
    out_shape = jax.ShapeDtypeStruct((b, n_heads, l, HEAD_DIM), F32)
    return pl.pallas_call(
        _scan_kernel,
        grid=(nc,),
        in_specs=[fwd] * 4 + [bwd] * 4,
        out_specs=[pl.BlockSpec(blk[1:], lambda i: (0, 0, i, 0)),
                   pl.BlockSpec(blk[1:], lambda i: (0, 0, bwd_chunk(i), 0))],
        out_shape=[out_shape, out_shape],
        scratch_shapes=[pltpu.VMEM((2, b, n_heads, HEAD_DIM, HEAD_DIM), F32)],
        compiler_params=_cparams(("arbitrary",)),
        name="chunk_scan",
    )(w2, y1, m, n, w2, y1, m, n)


def _gdn_prep_kernel(x_ref, prev_ref, next_ref, cw_ref, alog_ref, dtb_ref, ones_ref,
                     w2_ref, y1_ref, m_ref, n_ref,
                     ext_ref, tok_ref, *, n_ctx_tiles, n_tiles, width):
    t = pl.program_id(1)
    first, last = _seq_edges(t, n_ctx_tiles, n_tiles)
    _fill_halo(ext_ref, x_ref, prev_ref, next_ref, first, last)
    half = GD_CONV // 2
    conv = jnp.zeros((TM, 3 * width), F32)
    for j in range(GD_CONV):
        conv = conv + cw_ref[j:j + 1, :] * ext_ref[8 + j - half:8 + j - half + TM, 0:3 * width]
    qkv = conv * jax.nn.sigmoid(conv)
    ones_bd = ones_ref[...]

    def l2n(u):
        return u * lax.rsqrt(_dot_split(u * u, ones_bd) + NORM_EPS)

    q = l2n(qkv[:, 0:width]) * HEAD_DIM ** -0.5
    k = l2n(qkv[:, width:2 * width])
    v = qkv[:, 2 * width:3 * width]
    ab = ext_ref[8:8 + TM, 4 * width:4 * width + LANES]
    g_all = -jnp.exp(alog_ref[...]) * jax.nn.softplus(ab + dtb_ref[...])
    beta_all = jax.nn.sigmoid(ab)
    heads = _head_lanes(width)
    n_heads = len(heads)
    tok_ref[0] = q
    tok_ref[1] = k
    tok_ref[2] = v
    for d in range(2):
        g = jnp.zeros((TM, width), F32)
        beta = jnp.zeros((TM, width), F32)
        for h, in_head in enumerate(heads):
            ja = d * n_heads + h
            jb = 2 * n_heads + ja
            g = jnp.where(in_head, g_all[:, ja:ja + 1], g)
            beta = jnp.where(in_head, beta_all[:, jb:jb + 1], beta)
        tok_ref[3 + 2 * d] = g
        tok_ref[4 + 2 * d] = k * beta
    out_refs = (w2_ref, y1_ref, m_ref, n_ref)

    def chunk_inputs(c0, d):
        rows = pl.ds(c0, CHUNK)
        q_c, k_c, v_c = tok_ref[0, rows, :], tok_ref[1, rows, :], tok_ref[2, rows, :]
        g, kb = tok_ref[3 + 2 * d, rows, :], tok_ref[4 + 2 * d, rows, :]
        incl, _, _ = _tri_masks(reverse=(d == 1))
        ci = _dot(incl.astype(F32), g, precision=HIGHEST)
        ce = ci - g
        ci_end = ci[0:1, :] if d == 1 else ci[CHUNK - 1:CHUNK, :]
        earlier, earlier_or_same, eye = _tiled_masks(n_heads, reverse=(d == 1))
        ci_row = jnp.sum(jnp.where(eye, ci, 0.0), axis=0, keepdims=True)
        ce_row = jnp.sum(jnp.where(eye, ce, 0.0), axis=0, keepdims=True)
        kbs = _stack_heads(kb, heads)
        kkb = _dot_nt(k_c.astype(BF16), kbs)
        qkb = _dot_nt(q_c.astype(BF16), kbs)
        a_ka = kkb * jnp.exp(jnp.where(earlier, ce - ce_row, MASKED))
        a_kk = kkb * jnp.exp(jnp.where(earlier, ce - ci_row, MASKED))
        a_ra = qkb * jnp.exp(jnp.where(earlier_or_same, ci - ce_row, MASKED))
        a_rk = qkb * jnp.exp(jnp.where(earlier_or_same, ci - ci_row, MASKED))
        values = (a_ka, a_kk, a_ra, a_rk, _stack_heads(k_c * jnp.exp(ce), heads), q_c * jnp.exp(ci), v_c,
                  kb * jnp.exp(ci_end - ci), kb * jnp.exp(ci_end - ce))
        return values, jnp.exp(ci_end), eye

    def chunk_body(c, carry):
        args = [[] for _ in range(9)]
        where = []
        for cc in range(CHUNKS_PER_STEP):
            c0 = pl.multiple_of((c * CHUNKS_PER_STEP + cc) * CHUNK, CHUNK)
            for d in range(2):
                values, p_end, eye = chunk_inputs(c0, d)
                for slot, value in zip(args, values):
                    slot.append(value)
                where.append((d, c0, p_end))
        for (d, c0, p_end), res in zip(where, _chunk_matrices(*args, eye, heads)):
            _store_chunk(out_refs, d, c0, n_heads, *res, p_end)
        return carry

    lax.fori_loop(0, TM // CHUNK // CHUNKS_PER_STEP, chunk_body, 0)


def _gdn_prepare(p_gd, pr, n_ctx):
    b, l, pw = p_gd.shape
    width = pr['gd_width']
    n_heads = width // HEAD_DIM
    nt = l // TM
    kern = functools.partial(_gdn_prep_kernel, n_ctx_tiles=n_ctx // TM, n_tiles=nt, width=width)
    params = [pr['gd_conv_w'], pr['gd_alog_p'], pr['gd_dtb_p'], pr['ones_bd']]
    scan_shapes, scan_specs = _scan_out_specs(b, n_heads, l)
    return pl.pallas_call(
        kern,
        grid=(b, nt),
        in_specs=_halo_specs(pw, nt) + [_full_spec(a) for a in params],
        out_specs=scan_specs,
        out_shape=scan_shapes,
        scratch_shapes=[pltpu.VMEM((TM + 16, pw), F32), pltpu.VMEM((7, TM, width), F32)],
        compiler_params=_cparams(("arbitrary", "arbitrary")),
        name="gdn_prepare",
    )(p_gd, p_gd, p_gd, *params)


def _outproj_kernel(x_ref, mod_ref, na_ref, ryf_ref, ryb_ref, bv_ref, gate_ref, gyf_ref, gyb_ref, z_ref,
                    lnw_ref, lnb_ref, gng_ref, wo_ref, g2_ref, rwh_ref, rwl_ref, rb_ref,
                    x1_ref, h2_ref, route_ref, meta_ref, *, na_w, rw_w):
    n_heads = ryf_ref.shape[1]
    rw_parts, gd_parts = [], []
    for h in range(n_heads):
        y = ryf_ref[0, h] + ryb_ref[0, h]
        yc = y - jnp.mean(y, axis=-1, keepdims=True)
        rw_parts.append(yc * lax.rsqrt(jnp.mean(yc * yc, axis=-1, keepdims=True) + RW_GN_EPS))
        gd_parts.append(_rms(gyf_ref[0, h] + gyb_ref[0, h]) * gng_ref[...])
    yn = jnp.concatenate(rw_parts, axis=-1)
    o_rw = (yn * lnw_ref[...] + lnb_ref[...] + bv_ref[0]) * gate_ref[0]
    z = z_ref[0]
    o_gd = jnp.concatenate(gd_parts, axis=-1) * (z * jax.nn.sigmoid(z))
    o = (_dot(na_ref[0], wo_ref[0:na_w, :])
         + _dot(o_rw.astype(BF16), wo_ref[na_w:na_w + rw_w, :])
         + _dot(o_gd.astype(BF16), wo_ref[na_w + rw_w:, :]))
    mod = mod_ref[0]
    x1 = x_ref[0] + mod[2:3] * o
    x1_ref[0] = x1
    h2 = _rms(x1) * g2_ref[...] * (1.0 + mod[4:5]) + mod[3:4]
    h2_ref[0] = h2
    h2_hi = h2.astype(BF16)
    h2_lo = (h2 - h2_hi.astype(F32)).astype(BF16)
    logits = (_dot(h2_hi, rwh_ref[...]) + (_dot(h2_hi, rwl_ref[...]) + _dot(h2_lo, rwh_ref[...]))) + rb_ref[...]
    lane = lax.broadcasted_iota(jnp.int32, logits.shape, 1)
    chosen, weights = [], []
    onehot = jnp.zeros(logits.shape, F32)
    for kk in range(TOP_K):
        m = jnp.max(logits, axis=-1, keepdims=True)
        idx = jnp.min(jnp.where(logits == m, lane, LANES), axis=-1, keepdims=True)
        chosen.append(idx)
        top_max = m if kk == 0 else top_max
        weights.append(jnp.exp(m - top_max))
        onehot = jnp.where(lane == idx, 1.0, onehot)
        logits = jnp.where(lane == idx, -jnp.inf, logits)
    denom = weights[0] + weights[1] + weights[2] + weights[3]
    tok_i = lax.broadcasted_iota(jnp.int32, (TM, TM), 0)
    tok_j = lax.broadcasted_iota(jnp.int32, (TM, TM), 1)
    before = jnp.where(tok_j < tok_i, 1.0, 0.0).astype(BF16)
    rank = _dot(before, onehot.astype(BF16))
    counts = jnp.sum(onehot, axis=0, keepdims=True)
    count8 = jnp.floor((counts + 7.0) * 0.125) * 8.0
    exp_i = lax.broadcasted_iota(jnp.int32, (LANES, LANES), 0)
    exp_j = lax.broadcasted_iota(jnp.int32, (LANES, LANES), 1)
    seg_start = _dot(jnp.broadcast_to(count8, (8, LANES)), jnp.where(exp_i < exp_j, 1.0, 0.0),
                     precision=HIGHEST)[0:1]
    slot_of = seg_start + rank
    route = jnp.zeros(logits.shape, F32)
    for kk in range(TOP_K):
        slot = jnp.sum(jnp.where(lane == chosen[kk], slot_of, 0.0), axis=-1, keepdims=True)
        route = jnp.where(lane == kk, weights[kk] / denom, route)
        route = jnp.where(lane == TOP_K + kk, slot, route)
    route_ref[0] = route
    row8 = lax.broadcasted_iota(jnp.int32, (8, LANES), 0)
    meta_ref[0, 0] = jnp.where(row8 == 0, count8, jnp.where(row8 == 1, seg_start, 0.0))


def _outproj(xa, mods, o_na, ry, bv, gate, gy, p_gd, pr, n_ctx):
    b, l, d = xa.shape
    nt = l // TM
    na_w = o_na.shape[-1]
    n_heads = ry[0].shape[1]
    rw_w = n_heads * HEAD_DIM
    kern = functools.partial(_outproj_kernel, na_w=na_w, rw_w=rw_w)
    tok = lambda w: pl.BlockSpec((1, TM, w), lambda i, t: (i, t, 0))
    head_major = pl.BlockSpec((1, n_heads, TM, HEAD_DIM), lambda i, t: (i, 0, t, 0))
    params = [pr['rw_ln_w'].reshape(1, rw_w), pr['rw_ln_b'].reshape(1, rw_w), pr['gd_norm_g'].reshape(1, HEAD_DIM),
              pr['w_out_bf'], pr['norm_ffn_g'], pr['router_w_hi'], pr['router_w_lo'], pr['router_b_p']]
    return pl.pallas_call(
        kern,
        grid=(b, nt),
        in_specs=[tok(d), pl.BlockSpec((1, 6, d), _mod_index(n_ctx // TM, b)), tok(na_w),
                  head_major, head_major, tok(rw_w), tok(rw_w), head_major, head_major,
                  pl.BlockSpec((1, TM, rw_w), lambda i, t: (i, t, 3))]
                 + [_full_spec(a) for a in params],
        out_specs=[tok(d), tok(d), tok(LANES), pl.BlockSpec((1, 1, 8, LANES), lambda i, t: (i, t, 0, 0))],
        out_shape=[jax.ShapeDtypeStruct((b, l, d), F32), jax.ShapeDtypeStruct((b, l, d), F32),
                   jax.ShapeDtypeStruct((b, l, LANES), F32), jax.ShapeDtypeStruct((b, nt, 8, LANES), F32)],
        compiler_params=_cparams(("arbitrary", "arbitrary")),
        name="outproj_router",
    )(xa, mods, o_na, ry[0], ry[1], bv, gate, gy[0], gy[1], p_gd, *params)


MOE_BM = 512
MOE_EXPERT_VMEM = 56 * 1024 * 1024
SEG_ALIGN = 8
SEG_BITS = (32, 16, 8, 4, 2, 1)


def _n_slots(n_exp):
    return _round_up(TOP_K * TM + n_exp * (SEG_ALIGN - 1), LANES)


def _segment_dma(segs_ref, buf_ref, hbm_ref, sem, n_exp, to_hbm, wait):
    def body(e, carry):
        start = segs_ref[0, 0, e]
        units = segs_ref[0, 0, n_exp + e]
        offset = segs_ref[0, 0, 2 * n_exp + e]
        for bit in SEG_BITS:
            done = (units & ~(2 * bit - 1)) * SEG_ALIGN
            rows = bit * SEG_ALIGN

            @pl.when((units & bit) != 0)
            def _():
                in_buf = buf_ref.at[pl.ds(pl.multiple_of(start + done, SEG_ALIGN), rows), :]
                in_hbm = hbm_ref.at[pl.ds(pl.multiple_of(offset + done, SEG_ALIGN), rows), :]
                copy = (pltpu.make_async_copy(in_buf, in_hbm, sem) if to_hbm
                        else pltpu.make_async_copy(in_hbm, in_buf, sem))
                if wait:
                    copy.wait()
                else:
                    copy.start()
        return carry
    lax.fori_loop(0, n_exp, body, 0)


def _segment_wait(segs_ref, buf_ref, hbm_ref, sem, n_exp, to_hbm):
    for b, bit in enumerate(SEG_BITS):
        rows = bit * SEG_ALIGN
        in_buf = buf_ref.at[pl.ds(0, rows), :]
        in_hbm = hbm_ref.at[pl.ds(0, rows), :]
        copy = (pltpu.make_async_copy(in_buf, in_hbm, sem) if to_hbm
                else pltpu.make_async_copy(in_hbm, in_buf, sem))

        def body(_, carry, copy=copy):
            copy.wait()
            return carry
        lax.fori_loop(0, segs_ref[0, 0, 3 * n_exp + b], body, 0)


def _slot_lanes(route, n_slots):
    lane = lax.broadcasted_iota(jnp.int32, (route.shape[0], n_slots), 1)
    slots = [route[:, TOP_K + k:TOP_K + k + 1].astype(jnp.int32) for k in range(TOP_K)]
    return lane, slots


def _dispatch_kernel(segs_ref, segs_prev_ref, gaps_ref, h_ref, route_ref, xb_ref, xs_ref, zero_ref, sem, gap_sem,
                     *, n_exp):
    i = pl.program_id(0)
    slot = i % 2

    @pl.when(i == 0)
    def _():
        zero_ref[...] = jnp.zeros_like(zero_ref)
        _segment_dma(gaps_ref, zero_ref, xb_ref, gap_sem, n_exp, to_hbm=True, wait=False)
        _segment_dma(gaps_ref, zero_ref, xb_ref, gap_sem, n_exp, to_hbm=True, wait=True)
        n_blocks = xb_ref.shape[0] // MOE_BM

        def block_copy(blk):
            return pltpu.make_async_copy(zero_ref, xb_ref.at[pl.ds(pl.multiple_of(blk * MOE_BM, MOE_BM), MOE_BM), :],
                                         gap_sem)

        def start(blk, carry):
            block_copy(blk).start()
            return carry

        def wait(blk, carry):
            block_copy(blk).wait()
            return carry

        lax.fori_loop(gaps_ref[0, 0, 3 * n_exp], n_blocks, start, 0)
        lax.fori_loop(gaps_ref[0, 0, 3 * n_exp], n_blocks, wait, 0)

    lane, slots = _slot_lanes(route_ref[...], xs_ref.shape[1])
    picked = lane == slots[0]
    for k in range(1, TOP_K):
        picked = picked | (lane == slots[k])
    onehot = jnp.where(picked, 1.0, 0.0).astype(BF16)
    xs_ref[slot] = _dot_tn(onehot, h_ref[...].astype(BF16))
    _segment_dma(segs_ref, xs_ref.at[slot], xb_ref, sem.at[slot], n_exp, to_hbm=True, wait=False)

    @pl.when(i > 0)
    def _():
        _segment_wait(segs_prev_ref, xs_ref.at[1 - slot], xb_ref, sem.at[1 - slot], n_exp, to_hbm=True)

    @pl.when(i == pl.num_programs(0) - 1)
    def _():
        _segment_wait(segs_ref, xs_ref.at[slot], xb_ref, sem.at[slot], n_exp, to_hbm=True)


def _seg_spec(index):
    return pl.BlockSpec((1, 1, LANES), index, memory_space=pltpu.SMEM)


def _dispatch(h2, route, segs, gaps, n_pad, n_exp):
    t, d = h2.shape
    nt = t // TM
    kern = functools.partial(_dispatch_kernel, n_exp=n_exp)
    return pl.pallas_call(
        kern,
        grid=(nt,),
        in_specs=[_seg_spec(lambda i: (i, 0, 0)),
                  _seg_spec(lambda i: (jnp.maximum(i - 1, 0), 0, 0)),
                  _seg_spec(lambda i: (0, 0, 0)),
                  pl.BlockSpec((TM, d), lambda i: (i, 0)),
                  pl.BlockSpec((TM, LANES), lambda i: (i, 0))],
        out_specs=pl.BlockSpec(memory_space=pl.ANY),
        out_shape=jax.ShapeDtypeStruct((n_pad, d), F32),
        scratch_shapes=[pltpu.VMEM((2, _n_slots(n_exp), d), F32), pltpu.VMEM((MOE_BM, d), F32),
                        pltpu.SemaphoreType.DMA((2,)), pltpu.SemaphoreType.DMA(())],
        compiler_params=_cparams(("arbitrary",)),
        name="moe_dispatch",
    )(segs, segs, gaps, h2, route)


def _expert_kernel(blk_e_ref, n_used_ref, x_ref, wgu_ref, bgu_ref, wd_ref, bd_ref, y_ref, wgu_bf, wd_bf):
    i = pl.program_id(0)
    used = i < n_used_ref[0]
    new_expert = (i == 0) | (blk_e_ref[i] != blk_e_ref[jnp.maximum(i - 1, 0)])

    @pl.when(used & new_expert)
    def _():
        wgu_bf[...] = wgu_ref[0, 0].astype(BF16)
        wd_bf[...] = wd_ref[0, 0].astype(BF16)

    @pl.when(used)
    def _():
        x = x_ref[...].astype(BF16)
        d_exp = wd_bf.shape[0]
        gu = _dot(x, wgu_bf[...]) + bgu_ref[0, 0]
        gate = jnp.minimum(gu[:, :d_exp], SWIGLU_LIMIT)
        lin = jnp.clip(gu[:, d_exp:], -SWIGLU_LIMIT, SWIGLU_LIMIT)
        act = gate * jax.nn.sigmoid(SWIGLU_ALPHA * gate) * (lin + 1.0)
        y_ref[...] = _dot(act.astype(BF16), wd_bf[...]) + bd_ref[0, 0]

    @pl.when(jnp.logical_not(used))
    def _():
        y_ref[...] = jnp.zeros_like(y_ref)


def _expert_blocks(xb, blk_e, n_used, layer, w_gu, b_gu, w_down, b_down):
    n_pad, d = xb.shape
    n_blocks = n_pad // MOE_BM
    depth, n_exp, _, d_gu = w_gu.shape
    grid_spec = pltpu.PrefetchScalarGridSpec(
        num_scalar_prefetch=2,
        grid=(n_blocks,),
        in_specs=[pl.BlockSpec((MOE_BM, d), lambda i, be, nu: (jnp.minimum(i, nu[0] - 1), 0)),
                  pl.BlockSpec((1, 1, d, d_gu), lambda i, be, nu: (layer, be[i], 0, 0)),
                  pl.BlockSpec((1, 1, 1, d_gu), lambda i, be, nu: (layer, be[i], 0, 0)),
                  pl.BlockSpec((1, 1, d_gu // 2, d), lambda i, be, nu: (layer, be[i], 0, 0)),
                  pl.BlockSpec((1, 1, 1, d), lambda i, be, nu: (layer, be[i], 0, 0))],
        out_specs=pl.BlockSpec((MOE_BM, d), lambda i, be, nu: (i, 0)),
        scratch_shapes=[pltpu.VMEM((d, d_gu), BF16), pltpu.VMEM((d_gu // 2, d), BF16)])
    return pl.pallas_call(
        _expert_kernel,
        grid_spec=grid_spec,
        out_shape=jax.ShapeDtypeStruct((n_pad, d), F32),
        compiler_params=pltpu.CompilerParams(dimension_semantics=("arbitrary",),
                                             vmem_limit_bytes=MOE_EXPERT_VMEM),
        name="moe_experts",
    )(blk_e, n_used, xb, w_gu, b_gu.reshape(depth, n_exp, 1, d_gu), w_down, b_down.reshape(depth, n_exp, 1, d))


def _combine_kernel(segs_ref, segs_next_ref, y_hbm, x_ref, mod_ref, route_ref, o_ref, ybuf, sem, *, n_exp):
    i = pl.program_id(0)
    slot = i % 2

    @pl.when(i == 0)
    def _():
        ybuf[...] = jnp.zeros_like(ybuf)
        _segment_dma(segs_ref, ybuf.at[0], y_hbm, sem.at[0], n_exp, to_hbm=False, wait=False)

    @pl.when(i + 1 < pl.num_programs(0))
    def _():
        _segment_dma(segs_next_ref, ybuf.at[1 - slot], y_hbm, sem.at[1 - slot], n_exp, to_hbm=False, wait=False)

    _segment_wait(segs_ref, ybuf.at[slot], y_hbm, sem.at[slot], n_exp, to_hbm=False)
    y = ybuf[slot].astype(BF16)
    route = route_ref[...]
    lane, slots = _slot_lanes(route, ybuf.shape[1])
    weights = jnp.zeros(lane.shape, F32)
    for k in range(TOP_K):
        weights = jnp.where(lane == slots[k], route[:, k:k + 1], weights)
    o_ref[...] = x_ref[...] + mod_ref[0, 5:6, :] * _dot(weights.astype(BF16), y)


def _combine(yb, segs, x1, mods, route, n_batch, l, n_ctx, n_exp, latent_only):
    t, d = x1.shape
    nt = t // TM
    per_sample = l // TM
    n_ctx_tiles = n_ctx // TM
    kern = functools.partial(_combine_kernel, n_exp=n_exp)

    def mod_index(i):
        return (jnp.where(i % per_sample < n_ctx_tiles, n_batch, i // per_sample), 0, 0)

    if latent_only:
        lat_tiles = per_sample - n_ctx_tiles
        out_rows = n_batch * lat_tiles * TM
        out_index = lambda i: ((i // per_sample) * lat_tiles + jnp.maximum(i % per_sample - n_ctx_tiles, 0), 0)
    else:
        out_rows = t
        out_index = lambda i: (i, 0)

    return pl.pallas_call(
        kern,
        grid=(nt,),
        in_specs=[_seg_spec(lambda i: (i, 0, 0)),
                  _seg_spec(lambda i: (jnp.minimum(i + 1, nt - 1), 0, 0)),
                  pl.BlockSpec(memory_space=pl.ANY),
                  pl.BlockSpec((TM, d), lambda i: (i, 0)),
                  pl.BlockSpec((1, 6, d), mod_index),
                  pl.BlockSpec((TM, LANES), lambda i: (i, 0))],
        out_specs=pl.BlockSpec((TM, d), out_index),
        out_shape=jax.ShapeDtypeStruct((out_rows, d), F32),
        scratch_shapes=[pltpu.VMEM((2, _n_slots(n_exp), d), F32), pltpu.SemaphoreType.DMA((2,))],
        compiler_params=_cparams(("arbitrary",)),
        name="moe_combine",
    )(segs, segs, yb, x1, mods, route)


def _seg_table(start, units, offset, extra=None):
    n = start.shape[0]
    cols = [start, units, offset] + ([] if extra is None else [extra])
    pad = jnp.zeros((n, LANES - sum(c.shape[1] for c in cols)), jnp.int32)
    return jnp.concatenate(cols + [pad], axis=1).reshape(n, 1, LANES)


def _route(meta, n_exp, n_tokens):
    nt = meta.shape[0]
    count8 = meta[:, 0, :n_exp].astype(jnp.int32)
    seg_start = meta[:, 1, :n_exp].astype(jnp.int32)
    per_expert = jnp.sum(count8, axis=0)
    padded = (per_expert + MOE_BM - 1) // MOE_BM * MOE_BM
    pad_end = jnp.cumsum(padded)
    pad_start = pad_end - padded
    offset = pad_start[None, :] + jnp.cumsum(count8, axis=0) - count8
    n_pad = _round_up(n_tokens * TOP_K + nt * n_exp * (SEG_ALIGN - 1), MOE_BM) + n_exp * MOE_BM
    block_row = jnp.arange(n_pad // MOE_BM, dtype=jnp.int32) * MOE_BM
    blk_e = jnp.minimum(jnp.sum(block_row[:, None] >= pad_end[None, :], axis=1), n_exp - 1).astype(jnp.int32)
    n_used = (pad_end[-1:] // MOE_BM).astype(jnp.int32)
    units = count8 // SEG_ALIGN
    copies = jnp.stack([jnp.sum((units // bit) % 2, axis=1) for bit in SEG_BITS], axis=1)
    segs = _seg_table(seg_start, units, offset, copies.astype(jnp.int32))
    gaps = _seg_table(jnp.zeros((1, n_exp), jnp.int32), ((padded - per_expert) // SEG_ALIGN)[None, :],
                      (pad_start + per_expert)[None, :], n_used[None, :])
    return segs, gaps, blk_e, n_used, n_pad


def _channel_sublayer(x1, h2, route, meta, mods, layer, experts, n_ctx, latent_only):
    b, l, d = x1.shape
    t = b * l
    n_exp = experts[0].shape[1]
    assert 3 * n_exp + len(SEG_BITS) <= LANES and MOE_BM // SEG_ALIGN <= 2 * SEG_BITS[0]
    route = route.reshape(t, LANES)
    segs, gaps, blk_e, n_used, n_pad = _route(meta.reshape(t // TM, 8, LANES), n_exp, t)
    xb = _dispatch(h2.reshape(t, d), route, segs, gaps, n_pad, n_exp)
    yb = _expert_blocks(xb, blk_e, n_used, layer, *experts)
    x2 = _combine(yb, segs, x1.reshape(t, d), mods, route, b, l, n_ctx, n_exp, latent_only)
    return x2.reshape(b, -1, d)


def _pad_cols(w, n):
    return jnp.pad(w, ((0, 0), (0, n - w.shape[1])))


def _round_up(n, m):
    return -(-n // m) * m


def _prep_layer_params(pr):
    d = pr['w_in'].shape[0]
    na_w = d // 2
    rw_w = d // 4
    gd_w = d - na_w - rw_w
    na_cols = 3 * na_w
    rw_cols = 3 * rw_w + RW_DECAY_LORA + RW_ICLR_LORA + RW_GATE_LORA
    rw_pad = _round_up(rw_cols, LANES)
    gd_cols = pr['w_in'].shape[1] - na_cols - rw_cols
    gd_pad = _round_up(gd_cols, LANES)
    w_in = pr['w_in']
    out = dict(pr)
    out['w_in'] = jnp.concatenate([w_in[:, :na_cols],
                                   _pad_cols(w_in[:, na_cols:na_cols + rw_cols], rw_pad),
                                   _pad_cols(w_in[:, na_cols + rw_cols:], gd_pad)], axis=1).astype(BF16)
    out['rw_w'] = rw_pad
    out['gd_w'] = gd_pad
    n_heads = na_w // HEAD_DIM
    out['qkg'] = jnp.stack([jnp.tile(pr['na_q_gain'], n_heads) * HEAD_DIM ** -0.5,
                            jnp.tile(pr['na_k_gain'], n_heads)])
    head_of = np.arange(na_w) // HEAD_DIM
    out['avg'] = jnp.asarray((head_of[:, None] == head_of[None, :]) / HEAD_DIM, BF16)
    out['norm_mix_g'] = pr['norm_mix_g'].reshape(1, d)
    out['norm_ffn_g'] = pr['norm_ffn_g'].reshape(1, d)
    out['rw_mu_p'] = jnp.pad(pr['rw_mu'], (0, rw_pad - rw_cols)).reshape(1, rw_pad)
    o1, o2 = RW_DECAY_LORA, RW_DECAY_LORA + RW_ICLR_LORA
    out['rw_wup_p'] = jnp.pad(pr['rw_w_up'], ((0, 0), (0, LANES - o1), (0, 0)))
    out['rw_aup_p'] = jnp.pad(pr['rw_a_up'], ((0, 0), (o1, LANES - o2), (0, 0)))
    out['rw_gup_p'] = jnp.pad(pr['rw_g_up'], ((o2, LANES - o2 - RW_GATE_LORA), (0, 0)))
    head_of = np.arange(rw_w) // HEAD_DIM
    out['ones_bd'] = jnp.asarray(head_of[:, None] == head_of[None, :], BF16)
    out['gd_width'] = gd_w
    n_ab = pr['gd_A_log'].size
    out['gd_alog_p'] = jnp.pad(pr['gd_A_log'].reshape(-1), (0, LANES - n_ab)).reshape(1, LANES)
    out['gd_dtb_p'] = jnp.pad(pr['gd_dt_bias'].reshape(-1), (0, LANES - n_ab)).reshape(1, LANES)
    out['w_out_bf'] = pr['w_out'].astype(BF16)
    n_exp = pr['moe_router_w'].shape[1]
    router_w = _pad_cols(pr['moe_router_w'], LANES)
    out['router_w_hi'] = router_w.astype(BF16)
    out['router_w_lo'] = (router_w - out['router_w_hi'].astype(F32)).astype(BF16)
    out['router_b_p'] = jnp.concatenate([pr['moe_router_b'],
                                         jnp.full((LANES - n_exp,), MASKED, F32)]).reshape(1, LANES)
    return out


def _mixing_sublayer(xa, mods, pr, n_ctx):
    b, l, d = xa.shape
    na_w = d // 2
    p_na, p_rw, p_gd = _inproj(xa, mods, pr['norm_mix_g'], pr['w_in'], pr['qkg'], pr['avg'], n_ctx,
                               na_w, pr['rw_w'], pr['gd_w'])
    o_na = _na_attention(p_na, pr['na_bias'], n_ctx, na_w)
    rw2, ry1, rm, rn, bv, gate = _rwkv_prepare(p_rw, pr, n_ctx)
    ry = _scan(rw2, ry1, rm, rn, n_ctx)
    gy = _scan(*_gdn_prepare(p_gd, pr, n_ctx), n_ctx)
    return _outproj(xa, mods, o_na, ry, bv, gate, gy, p_gd, pr, n_ctx)


_LAYER_PARAMS = ('norm_mix_g', 'norm_ffn_g', 'w_in', 'w_out', 'na_q_gain', 'na_k_gain', 'na_rpb',
                 'rw_mu', 'rw_w0', 'rw_w_up', 'rw_a0', 'rw_a_up', 'rw_g_up', 'rw_k_k', 'rw_k_a', 'rw_r_k',
                 'rw_ln_w', 'rw_ln_b', 'gd_conv_w', 'gd_A_log', 'gd_dt_bias', 'gd_norm_g',
                 'moe_router_w', 'moe_router_b', 'moe_w_gu', 'moe_b_gu', 'moe_w_down', 'moe_b_down')


def kernel(x, c, ctx, c_ctx, ada_w, ada_b, norm_mix_g, norm_ffn_g, w_in, w_out, na_q_gain, na_k_gain, na_rpb, rw_mu, rw_w0, rw_w_up, rw_a0, rw_a_up, rw_g_up, rw_k_k, rw_k_a, rw_r_k, rw_ln_w, rw_ln_b, gd_conv_w, gd_A_log, gd_dt_bias, gd_norm_g, moe_router_w, moe_router_b, moe_w_gu, moe_b_gu, moe_w_down, moe_b_down):
    stacked = dict(zip(_LAYER_PARAMS, (norm_mix_g, norm_ffn_g, w_in, w_out, na_q_gain, na_k_gain, na_rpb,
                                       rw_mu, rw_w0, rw_w_up, rw_a0, rw_a_up, rw_g_up, rw_k_k, rw_k_a, rw_r_k,
                                       rw_ln_w, rw_ln_b, gd_conv_w, gd_A_log, gd_dt_bias, gd_norm_g,
                                       moe_router_w, moe_router_b, moe_w_gu, moe_b_gu, moe_w_down, moe_b_down)))
    b, s, d = x.shape
    n_ctx = ctx.shape[1]
    depth = ada_w.shape[0]
    assert b + 1 <= 8 and n_ctx % TM == 0 and s % TM == 0 and s % GRID_W == 0
    cvec = jnp.zeros((8, d), F32).at[:b].set(c).at[b].set(c_ctx)
    mods_all = _adaln(cvec, ada_w, ada_b).reshape(depth, 8, 6, d)[:, :b + 1]
    xa = jnp.concatenate([ctx, x], axis=1)
    experts = (moe_w_gu, moe_b_gu, moe_w_down, moe_b_down)
    for layer in range(depth):
        pr = _prep_layer_params({name: value[layer] for name, value in stacked.items()
                                 if not name.startswith('moe_w_') and not name.startswith('moe_b_')})
        pr['na_bias'] = _na_bias_tables(pr['na_rpb'], s // GRID_W)
        mods = mods_all[layer]
        x1, h2, route, meta = _mixing_sublayer(xa, mods, pr, n_ctx)
        xa = _channel_sublayer(x1, h2, route, meta, mods, layer, experts, n_ctx,
                               latent_only=(layer == depth - 1))
    return xa
```

```python
import functools
import math

import numpy as np
import jax
import jax.numpy as jnp
from jax import lax
from jax.experimental import pallas as pl
from jax.experimental.pallas import tpu as pltpu

F32 = jnp.float32
BF16 = jnp.bfloat16
HIGHEST = lax.Precision.HIGHEST

GRID_W = 64
NORM_EPS = 1e-6
HEAD_DIM = 64
NA_WIN_ROWS = 8
NA_WIN_COLS = 16
RW_DECAY_LORA = 32
RW_ICLR_LORA = 32
RW_GATE_LORA = 64
RW_GN_EPS = 64e-5
GD_CONV = 5
TOP_K = 4
SWIGLU_LIMIT = 7.0
SWIGLU_ALPHA = 1.702

LANES = 128
TM = 256
NA_QROWS = 4
NA_KROWS = NA_QROWS + NA_WIN_ROWS
CHUNK = 64
CHUNKS_PER_STEP = 4
MASKED = -1e30
VMEM_LIMIT = 48 * 1024 * 1024


def _cparams(sem):
    return pltpu.CompilerParams(dimension_semantics=sem, vmem_limit_bytes=VMEM_LIMIT)


def _dot(a, b, **kw):
    return jnp.dot(a, b, preferred_element_type=F32, **kw)


def _dot_split(a, b_bf16):
    hi = a.astype(BF16)
    lo = (a - hi.astype(F32)).astype(BF16)
    return _dot(hi, b_bf16) + _dot(lo, b_bf16)


def _split_bf16(a):
    hi = a.astype(BF16)
    return hi, (a - hi.astype(F32)).astype(BF16)


def _dot_split3(a, b):
    a_hi = a.astype(BF16)
    a_lo = (a - a_hi.astype(F32)).astype(BF16)
    b_hi = b.astype(BF16)
    b_lo = (b - b_hi.astype(F32)).astype(BF16)
    return _dot(a_hi, b_hi) + (_dot(a_hi, b_lo) + _dot(a_lo, b_hi))


def _dot_nt(a, b, **kw):
    return lax.dot_general(a, b, (((1,), (1,)), ((), ())), preferred_element_type=F32, **kw)


def _dot_tn(a, b, **kw):
    return lax.dot_general(a, b, (((0,), (0,)), ((), ())), preferred_element_type=F32, **kw)


def _adaln_kernel(c_ref, w_ref, b_ref, o_ref):
    c = c_ref[...]
    s = c * jax.nn.sigmoid(c)
    o_ref[0] = _dot(s.astype(BF16), w_ref[0].astype(BF16)) + b_ref[0]


def _adaln(cvec, ada_w, ada_b):
    depth, d, n = ada_w.shape
    tn = n // 4
    return pl.pallas_call(
        _adaln_kernel,
        grid=(depth, n // tn),
        in_specs=[pl.BlockSpec((8, d), lambda l, j: (0, 0)),
                  pl.BlockSpec((1, d, tn), lambda l, j: (l, 0, j)),
                  pl.BlockSpec((1, 1, tn), lambda l, j: (l, 0, j))],
        out_specs=pl.BlockSpec((1, 8, tn), lambda l, j: (l, 0, j)),
        out_shape=jax.ShapeDtypeStruct((depth, 8, n), F32),
        compiler_params=_cparams(("arbitrary", "arbitrary")),
        name="adaln",
    )(cvec, ada_w, ada_b.reshape(depth, 1, n))


def _mod_index(n_ctx_tiles, n_batch):
    return lambda b, t: (jnp.where(t < n_ctx_tiles, n_batch, b), 0, 0)


def _rms(x):
    return x * lax.rsqrt(jnp.mean(x * x, axis=-1, keepdims=True) + NORM_EPS)


def _inproj_kernel(x_ref, mod_ref, g_ref, w_ref, qkg_ref, avg_ref, na_ref, rw_ref, gd_ref, *, na_w, rw_w):
    mod = mod_ref[0]
    h = _rms(x_ref[0]) * g_ref[...]
    hb = (h * (1.0 + mod[1:2]) + mod[0:1]).astype(BF16)
    pa = _dot(hb, w_ref[:, 0:3 * na_w])
    avg = avg_ref[...]
    qkg = qkg_ref[...]
    q = pa[:, 0:na_w]
    k = pa[:, na_w:2 * na_w]
    qn = q * lax.rsqrt(_dot(q * q, avg) + NORM_EPS) * qkg[0:1]
    kn = k * lax.rsqrt(_dot(k * k, avg) + NORM_EPS) * qkg[1:2]
    na_ref[0, :, 0:na_w] = qn.astype(BF16)
    na_ref[0, :, na_w:2 * na_w] = kn.astype(BF16)
    na_ref[0, :, 2 * na_w:3 * na_w] = pa[:, 2 * na_w:3 * na_w].astype(BF16)
    rw_ref[0] = _dot(hb, w_ref[:, 3 * na_w:3 * na_w + rw_w])
    gd_ref[0] = _dot(hb, w_ref[:, 3 * na_w + rw_w:])


def _inproj(xa, mods, g, w_all, qkg, avg, n_ctx, na_w, rw_w, gd_w):
    b, l, d = xa.shape
    nt = l // TM
    kern = functools.partial(_inproj_kernel, na_w=na_w, rw_w=rw_w)
    return pl.pallas_call(
        kern,
        grid=(b, nt),
        in_specs=[pl.BlockSpec((1, TM, d), lambda i, t: (i, t, 0)),
                  pl.BlockSpec((1, 6, d), _mod_index(n_ctx // TM, b)),
                  pl.BlockSpec((1, d), lambda i, t: (0, 0)),
                  pl.BlockSpec(w_all.shape, lambda i, t: (0, 0)),
                  pl.BlockSpec(qkg.shape, lambda i, t: (0, 0)),
                  pl.BlockSpec(avg.shape, lambda i, t: (0, 0))],
        out_specs=[pl.BlockSpec((1, TM, 3 * na_w), lambda i, t: (i, t, 0)),
                   pl.BlockSpec((1, TM, rw_w), lambda i, t: (i, t, 0)),
                   pl.BlockSpec((1, TM, gd_w), lambda i, t: (i, t, 0))],
        out_shape=[jax.ShapeDtypeStruct((b, l, 3 * na_w), BF16),
                   jax.ShapeDtypeStruct((b, l, rw_w), F32),
                   jax.ShapeDtypeStruct((b, l, gd_w), F32)],
        compiler_params=_cparams(("arbitrary", "arbitrary")),
        name="inproj",
    )(xa, mods, g, w_all, qkg, avg)


def _na_bias_tables(rpb, rows):
    n_heads = rpb.shape[0]
    nq, nk = NA_QROWS * GRID_W, NA_KROWS * GRID_W
    qc = np.arange(GRID_W)[:, None]
    kc = np.arange(GRID_W)[None, :]
    c0 = np.clip(qc - NA_WIN_COLS // 2, 0, GRID_W - NA_WIN_COLS)
    col_ok = (kc >= c0) & (kc < c0 + NA_WIN_COLS)
    dc = kc - qc + NA_WIN_COLS - 1
    pick_col = (dc[None] == np.arange(2 * NA_WIN_COLS - 1)[:, None, None]).astype(np.float32)
    by_col = jnp.einsum('hrd,dqk->hrqk', rpb, pick_col, precision=HIGHEST)
    tables = []
    for r0 in (0, NA_QROWS, rows - NA_QROWS):
        ks = int(np.clip(r0 - NA_WIN_ROWS // 2, 0, rows - NA_KROWS))
        qr = (r0 + np.arange(NA_QROWS))[:, None]
        kr = (ks + np.arange(NA_KROWS))[None, :]
        s0 = np.clip(qr - NA_WIN_ROWS // 2, 0, rows - NA_WIN_ROWS)
        row_ok = (kr >= s0) & (kr < s0 + NA_WIN_ROWS)
        dr = kr - qr + NA_WIN_ROWS - 1
        pick_row = ((dr[None] == np.arange(2 * NA_WIN_ROWS - 1)[:, None, None]) & row_ok[None]).astype(np.float32)
        bias = jnp.einsum('rab,hrqk->haqbk', pick_row, by_col, precision=HIGHEST)
        ok = row_ok[:, None, :, None] & col_ok[None, :, None, :]
        tables.append(jnp.where(ok[None], bias, MASKED).reshape(n_heads, nq, nk))
    tables.append(jnp.full((n_heads, nq, nk), MASKED, F32))
    return jnp.stack(tables).reshape(4, n_heads // 2, 2, nq, nk)


def _na_kernel(q_ref, k_ref, v_ref, bias_ref, o_ref, *, n_ctx, rows):
    j = pl.program_id(2)
    nk = NA_KROWS * GRID_W
    ks_row = jnp.clip((j - 1) * NA_QROWS - NA_WIN_ROWS // 2, 0, rows - NA_KROWS)
    kstart = pl.multiple_of(n_ctx + ks_row * GRID_W, GRID_W)
    q = q_ref[0]
    kl = k_ref[0, pl.ds(kstart, nk), :]
    vl = v_ref[0, pl.ds(kstart, nk), :]
    kc = k_ref[0, 0:n_ctx, :]
    vc = v_ref[0, 0:n_ctx, :]
    lane = lax.broadcasted_iota(jnp.int32, q.shape, 1)
    outs = []
    for h in range(2):
        in_head = (lane >= h * HEAD_DIM) & (lane < (h + 1) * HEAD_DIM)
        qh = jnp.where(in_head, q, jnp.zeros_like(q))
        s_loc = _dot_nt(qh, kl) + bias_ref[0, 0, h]
        s_ctx = _dot_nt(qh, kc)
        m = jnp.maximum(jnp.max(s_loc, axis=-1, keepdims=True), jnp.max(s_ctx, axis=-1, keepdims=True))
        p_loc = jnp.exp(s_loc - m)
        p_ctx = jnp.exp(s_ctx - m)
        den = jnp.sum(p_loc, axis=-1, keepdims=True) + jnp.sum(p_ctx, axis=-1, keepdims=True)
        o = _dot(p_loc.astype(BF16), vl) + _dot(p_ctx.astype(BF16), vc)
        outs.append(o / den)
    o_ref[0] = jnp.where(lane < HEAD_DIM, outs[0], outs[1]).astype(o_ref.dtype)


def _na_attention(p_na, bias, n_ctx, na_w):
    b, l, _ = p_na.shape
    rows = (l - n_ctx) // GRID_W
    nq = NA_QROWS * GRID_W
    assert n_ctx == nq and rows >= NA_KROWS and rows % NA_QROWS == 0
    n_pairs = na_w // LANES
    nblk = l // nq
    kern = functools.partial(_na_kernel, n_ctx=n_ctx, rows=rows)

    def bias_index(i, hp, j):
        pat = jnp.where(j == 0, 3, jnp.where(j == 1, 0, jnp.where(j == nblk - 1, 2, 1)))
        return (pat, hp, 0, 0, 0)

    return pl.pallas_call(
        kern,
        grid=(b, n_pairs, nblk),
        in_specs=[pl.BlockSpec((1, nq, LANES), lambda i, hp, j: (i, j, hp)),
                  pl.BlockSpec((1, l, LANES), lambda i, hp, j: (i, 0, n_pairs + hp)),
                  pl.BlockSpec((1, l, LANES), lambda i, hp, j: (i, 0, 2 * n_pairs + hp)),
                  pl.BlockSpec((1, 1, 2) + bias.shape[3:], bias_index)],
        out_specs=pl.BlockSpec((1, nq, LANES), lambda i, hp, j: (i, j, hp)),
        out_shape=jax.ShapeDtypeStruct((b, l, na_w), BF16),
        compiler_params=_cparams(("arbitrary", "arbitrary", "arbitrary")),
        name="na_attention",
    )(p_na, p_na, p_na, bias)


def _tri_masks(reverse):
    t = lax.broadcasted_iota(jnp.int32, (CHUNK, CHUNK), 0)
    s = lax.broadcasted_iota(jnp.int32, (CHUNK, CHUNK), 1)
    earlier_or_same = (s >= t) if reverse else (s <= t)
    earlier = (s > t) if reverse else (s < t)
    return earlier_or_same, earlier, t == s


def _each(fn, *lists):
    return [fn(*args) for args in zip(*lists)]


def _bf16(values):
    return [u.astype(BF16) for u in values]


def _head_lanes(width):
    lane = lax.broadcasted_iota(jnp.int32, (1, width), 1)
    return [(lane >= h * HEAD_DIM) & (lane < (h + 1) * HEAD_DIM) for h in range(width // HEAD_DIM)]


def _stack_heads(x, heads):
    return jnp.concatenate([jnp.where(in_head, x, 0.0) for in_head in heads], axis=0).astype(BF16)


def _tiled_masks(n_heads, reverse):
    t = lax.broadcasted_iota(jnp.int32, (CHUNK, n_heads * CHUNK), 0)
    s = lax.broadcasted_iota(jnp.int32, (CHUNK, n_heads * CHUNK), 1) % CHUNK
    earlier = (s > t) if reverse else (s < t)
    return earlier, earlier | (s == t), s == t


def _bf16_terms(a, split):
    return _split_bf16(a) if split else (a.astype(BF16),)


def _dot_terms(u, w):
    out = _dot(u[0], w[0])
    if len(u) > 1:
        out = out + (_dot(u[0], w[1]) + _dot(u[1], w[0]))
    return out


def _unit_tri_inverse(a, eye, heads, split):
    assert CHUNK == 64
    parts = lambda values: [_bf16_terms(u, split) for u in values]
    diag = lambda pairs: [tuple(_stack_heads(term, heads) for term in pair) for pair in pairs]
    mul = _dot_terms
    x1 = [-u for u in a]
    x1b = parts(x1)
    x1d = diag(x1b)
    x2 = _each(mul, x1b, x1d)
    x2b = parts(x2)
    x2d = diag(x2b)
    x4 = _each(mul, x2b, x2d)
    x3 = _each(mul, x1b, x2d)
    x4b = parts(x4)
    x4d = diag(x4b)
    x8 = _each(mul, x4b, x4d)
    p1 = _each(lambda u1, u2, u3: eye + u1 + u2 + u3, x1, x2, x3)
    x8b = parts(x8)
    x8d = diag(x8b)
    x12 = _each(mul, x4b, x8d)
    x16 = _each(mul, x8b, x8d)
    p2 = _each(lambda u4, u8, u12: eye + u4 + u8 + u12, x4, x8, x12)
    p12 = _each(mul, parts(p1), diag(parts(p2)))
    x16b = parts(x16)
    x16d = diag(x16b)
    x32 = _each(mul, x16b, x16d)
    x48 = _each(mul, x16b, diag(parts(x32)))
    p3 = _each(lambda u16, u32, u48: eye + u16 + u32 + u48, x16, x32, x48)
    return _each(mul, parts(p12), diag(parts(p3)))


def _chunk_matrices(a_ka, a_kk, a_ra, a_rk, ks, rt, v, kd, ad, eye, heads, split):
    stack = lambda values: [_stack_heads(u, heads) for u in values]
    t_inv = [_bf16_terms(u, split) for u in _unit_tri_inverse(a_ka, eye.astype(F32), heads, split)]
    apply_inverse = lambda t, k: sum(_dot(term, k) for term in t[1:]) + _dot(t[0], k)
    vs, a_rab, adb = stack(v), _bf16(a_ra), _bf16(ad)
    av = _each(_dot, _bf16(a_kk), vs)
    w1 = _each(apply_inverse, t_inv, ks)
    u0 = _each(apply_inverse, t_inv, stack(av))
    rkv = _each(_dot, _bf16(a_rk), vs)
    w2 = _each(lambda r, a, w: r - _dot(a, w), rt, a_rab, stack(w1))
    y1 = _each(lambda y, a, u: y - _dot(a, u), rkv, a_rab, stack(u0))
    mm = _each(_dot_tn, adb, _bf16(w1))
    nn = _each(lambda kd_, v_, ad_, u: _dot_tn(kd_, v_) - _dot_tn(ad_, u), _bf16(kd), _bf16(v), adb, _bf16(u0))
    return list(zip(w2, y1, mm, nn))


def _store_chunk(refs, d, c0, n_heads, w2, y1, mm, nn, p_end):
    w2_ref, y1_ref, m_ref, n_ref = refs
    rows = pl.ds(c0, CHUNK)
    eye = (lax.broadcasted_iota(jnp.int32, (HEAD_DIM, HEAD_DIM), 0)
           == lax.broadcasted_iota(jnp.int32, (HEAD_DIM, HEAD_DIM), 1))
    for h in range(n_heads):
        lo, hi = h * HEAD_DIM, (h + 1) * HEAD_DIM
        w2_ref[d, 0, h, rows, :] = w2[:, lo:hi]
        y1_ref[d, 0, h, rows, :] = y1[:, lo:hi]
        m_ref[d, 0, h, rows, :] = jnp.where(eye, p_end[:, lo:hi], 0.0) - mm[lo:hi, lo:hi]
        n_ref[d, 0, h, rows, :] = nn[lo:hi, lo:hi]


def _scan_out_specs(b, n_heads, l):
    shape = jax.ShapeDtypeStruct((2, b, n_heads, l, HEAD_DIM), F32)
    spec = pl.BlockSpec((2, 1, n_heads, TM, HEAD_DIM), lambda i, t: (0, i, 0, t, 0))
    return [shape] * 4, [spec] * 4


def _seq_edges(t, n_ctx_tiles, n_tiles):
    first = (t == 0) | (t == n_ctx_tiles)
    last = (t == n_ctx_tiles - 1) | (t == n_tiles - 1)
    return first, last


def _fill_halo(ext_ref, x_ref, prev_ref, next_ref, first, last):
    ext_ref[0:8, :] = jnp.where(first, 0.0, prev_ref[0])
    ext_ref[8:8 + TM, :] = x_ref[0]
    ext_ref[8 + TM:16 + TM, :] = jnp.where(last, 0.0, next_ref[0])


def _halo_specs(width, n_tiles):
    per = TM // 8
    return [pl.BlockSpec((1, TM, width), lambda i, t: (i, t, 0)),
            pl.BlockSpec((1, 8, width), lambda i, t: (i, jnp.maximum(t * per - 1, 0), 0)),
            pl.BlockSpec((1, 8, width), lambda i, t: (i, jnp.minimum((t + 1) * per, n_tiles * per - 1), 0))]


def _full_spec(a):
    return pl.BlockSpec(a.shape, lambda i, t: (0,) * a.ndim)


def _rwkv_prep_kernel(x_ref, prev_ref, next_ref, mu_ref, w0_ref, wup_ref, a0_ref, aup_ref, gup_ref,
                      kk_ref, ka_ref, rk_ref, ones_ref,
                      w2_ref, y1_ref, m_ref, n_ref, bv_ref, gate_ref,
                      ext_ref, tok_ref, *, n_ctx_tiles, n_tiles, width):
    t = pl.program_id(1)
    first, last = _seq_edges(t, n_ctx_tiles, n_tiles)
    _fill_halo(ext_ref, x_ref, prev_ref, next_ref, first, last)
    p = ext_ref[8:8 + TM, :]
    prev = ext_ref[7:7 + TM, :]
    nxt = ext_ref[9:9 + TM, :]
    ps = p + mu_ref[...] * (0.5 * (prev + nxt) - p)
    r = ps[:, 0:width]
    k = ps[:, width:2 * width]
    v = ps[:, 2 * width:3 * width]
    lo = ps[:, 3 * width:3 * width + LANES]
    ones_bd = ones_ref[...]
    gate_ref[0] = _dot(jax.nn.sigmoid(lo), gup_ref[...])
    kq = k * kk_ref[...]
    kk = kq * lax.rsqrt(_dot_split(kq * kq, ones_bd) + NORM_EPS)
    tanh_lo = jnp.tanh(lo)
    tok_ref[0] = r
    tok_ref[1] = kk
    tok_ref[2] = v
    k_sum = jnp.zeros_like(k)
    for d in range(2):
        z = w0_ref[d:d + 1, :] + _dot(tanh_lo, wup_ref[d])
        w_log = -jax.nn.softplus(-z) - 0.5
        a = jax.nn.sigmoid(a0_ref[d:d + 1, :] + _dot(lo, aup_ref[d]))
        k_dir = k * (1.0 + (a - 1.0) * ka_ref[...])
        k_sum = k_sum + k_dir
        tok_ref[3 + 3 * d] = -jnp.exp(w_log)
        tok_ref[4 + 3 * d] = kk * a
        tok_ref[5 + 3 * d] = k_dir
    bonus = _dot_split(r * k_sum * rk_ref[...], ones_bd)
    bv_ref[0] = bonus * v

    heads = _head_lanes(width)
    out_refs = (w2_ref, y1_ref, m_ref, n_ref)

    def chunk_body(c, carry):
        args = [[] for _ in range(9)]
        where = []
        for cc in range(CHUNKS_PER_STEP):
            c0 = pl.multiple_of((c * CHUNKS_PER_STEP + cc) * CHUNK, CHUNK)
            rows = pl.ds(c0, CHUNK)
            r_c, kk_c, v_c = tok_ref[0, rows, :], tok_ref[1, rows, :], tok_ref[2, rows, :]
            for d in range(2):
                logw, alpha, k_c = (tok_ref[3 + 3 * d, rows, :], tok_ref[4 + 3 * d, rows, :],
                                    tok_ref[5 + 3 * d, rows, :])
                incl, _, _ = _tri_masks(reverse=(d == 1))
                ci = _dot(incl.astype(F32), logw, precision=HIGHEST)
                e_ci, e_nci, e_ce = jnp.exp(ci), jnp.exp(-ci), jnp.exp(ci - logw)
                p_end = e_ci[0:1, :] if d == 1 else e_ci[CHUNK - 1:CHUNK, :]
                kt, rt, kh, ah = kk_c * e_ce, r_c * e_ci, k_c * e_nci, alpha * e_nci
                earlier, earlier_or_same, eye = _tiled_masks(len(heads), reverse=(d == 1))
                khs, ahs = _stack_heads(kh, heads), _stack_heads(ah, heads)
                ktb, rtb = kt.astype(BF16), rt.astype(BF16)
                a_ka = jnp.where(earlier, _dot_nt(ktb, ahs), 0.0)
                a_kk = jnp.where(earlier, _dot_nt(ktb, khs), 0.0)
                a_ra = jnp.where(earlier_or_same, _dot_nt(rtb, ahs), 0.0)
                a_rk = jnp.where(earlier_or_same, _dot_nt(rtb, khs), 0.0)
                values = (a_ka, a_kk, a_ra, a_rk, _stack_heads(kt, heads), rt, v_c, kh * p_end, ah * p_end)
                for slot, value in zip(args, values):
                    slot.append(value)
                where.append((d, c0, p_end))
        for (d, c0, p_end), res in zip(where, _chunk_matrices(*args, eye, heads, split=False)):
            _store_chunk(out_refs, d, c0, len(heads), *res, p_end)
        return carry

    lax.fori_loop(0, TM // CHUNK // CHUNKS_PER_STEP, chunk_body, 0)


def _rwkv_prepare(p_rw, pr, n_ctx):
    b, l, pw = p_rw.shape
    width = pr['rw_k_k'].shape[-1]
    n_heads = width // HEAD_DIM
    nt = l // TM
    kern = functools.partial(_rwkv_prep_kernel, n_ctx_tiles=n_ctx // TM, n_tiles=nt, width=width)
    params = [pr['rw_mu_p'], pr['rw_w0'], pr['rw_wup_p'], pr['rw_a0'], pr['rw_aup_p'], pr['rw_gup_p'],
              pr['rw_k_k'].reshape(1, width), pr['rw_k_a'].reshape(1, width), pr['rw_r_k'].reshape(1, width),
              pr['ones_bd']]
    scan_shapes, scan_specs = _scan_out_specs(b, n_heads, l)
    tok_spec = pl.BlockSpec((1, TM, width), lambda i, t: (i, t, 0))
    return pl.pallas_call(
        kern,
        grid=(b, nt),
        in_specs=_halo_specs(pw, nt) + [_full_spec(a) for a in params],
        out_specs=scan_specs + [tok_spec, tok_spec],
        out_shape=scan_shapes + [jax.ShapeDtypeStruct((b, l, width), F32)] * 2,
        scratch_shapes=[pltpu.VMEM((TM + 16, pw), F32), pltpu.VMEM((9, TM, width), F32)],
        compiler_params=_cparams(("arbitrary", "arbitrary")),
        name="rwkv_prepare",
    )(p_rw, p_rw, p_rw, *params)


def _scan_kernel(w2f_ref, y1f_ref, mf_ref, nf_ref, w2b_ref, y1b_ref, mb_ref, nb_ref, yf_ref, yb_ref, h_ref):
    @pl.when(pl.program_id(0) == 0)
    def _():
        h_ref[...] = jnp.zeros_like(h_ref)

    n_batch, n_heads = h_ref.shape[1], h_ref.shape[2]
    n_sub = yf_ref.shape[2] // CHUNK
    dirs = ((w2f_ref, y1f_ref, mf_ref, nf_ref, yf_ref), (w2b_ref, y1b_ref, mb_ref, nb_ref, yb_ref))
    chains = [(d, i, h) for d in range(2) for i in range(n_batch) for h in range(n_heads)]
    states = [h_ref[d, i, h] for d, i, h in chains]
    for j in range(n_sub):
        for c, (d, i, h) in enumerate(chains):
            w2_ref, y1_ref, m_ref, n_ref, y_ref = dirs[d]
            jj = j if d == 0 else n_sub - 1 - j
            rows = slice(jj * CHUNK, (jj + 1) * CHUNK)
            y_ref[i, h, rows, :] = _dot(w2_ref[0, i, h, rows, :], states[c]) + y1_ref[0, i, h, rows, :]
            states[c] = _dot_split3(m_ref[0, i, h, rows, :], states[c]) + n_ref[0, i, h, rows, :]
    for c, (d, i, h) in enumerate(chains):
        h_ref[d, i, h] = states[c]


def _scan(w2, y1, m, n, n_ctx):
    _, b, n_heads, l, _ = w2.shape
    nc = l // TM
    ncc = n_ctx // TM

    def bwd_chunk(i):
        return jnp.where(i < ncc, ncc - 1 - i, nc - 1 + ncc - i)

    blk = (1, b, n_heads, TM, HEAD_DIM)
    fwd = pl.BlockSpec(blk, lambda i: (0, 0, 0, i, 0))
    bwd = pl.BlockSpec(blk, lambda i: (1, 0, 0, bwd_chunk(i), 0))
    out_shape = jax.ShapeDtypeStruct((b, n_heads, l, HEAD_DIM), F32)
    return pl.pallas_call(
        _scan_kernel,
        grid=(nc,),
        in_specs=[fwd] * 4 + [bwd] * 4,
        out_specs=[pl.BlockSpec(blk[1:], lambda i: (0, 0, i, 0)),
                   pl.BlockSpec(blk[1:], lambda i: (0, 0, bwd_chunk(i), 0))],
        out_shape=[out_shape, out_shape],
        scratch_shapes=[pltpu.VMEM((2, b, n_heads, HEAD_DIM, HEAD_DIM), F32)],
        compiler_params=_cparams(("arbitrary",)),
        name="chunk_scan",
    )(w2, y1, m, n, w2, y1, m, n)


def _gdn_prep_kernel(x_ref, prev_ref, next_ref, cw_ref, alog_ref, dtb_ref, ones_ref,
                     w2_ref, y1_ref, m_ref, n_ref,
                     ext_ref, tok_ref, *, n_ctx_tiles, n_tiles, width):
    t = pl.program_id(1)
    first, last = _seq_edges(t, n_ctx_tiles, n_tiles)
    _fill_halo(ext_ref, x_ref, prev_ref, next_ref, first, last)
    half = GD_CONV // 2
    conv = jnp.zeros((TM, 3 * width), F32)
    for j in range(GD_CONV):
        conv = conv + cw_ref[j:j + 1, :] * ext_ref[8 + j - half:8 + j - half + TM, 0:3 * width]
    qkv = conv * jax.nn.sigmoid(conv)
    ones_bd = ones_ref[...]

    def l2n(u):
        return u * lax.rsqrt(_dot_split(u * u, ones_bd) + NORM_EPS)

    q = l2n(qkv[:, 0:width]) * HEAD_DIM ** -0.5
    k = l2n(qkv[:, width:2 * width])
    v = qkv[:, 2 * width:3 * width]
    ab = ext_ref[8:8 + TM, 4 * width:4 * width + LANES]
    g_all = -jnp.exp(alog_ref[...]) * jax.nn.softplus(ab + dtb_ref[...])
    beta_all = jax.nn.sigmoid(ab)
    heads = _head_lanes(width)
    n_heads = len(heads)
    tok_ref[0] = q
    tok_ref[1] = k
    tok_ref[2] = v
    for d in range(2):
        g = jnp.zeros((TM, width), F32)
        beta = jnp.zeros((TM, width), F32)
        for h, in_head in enumerate(heads):
            ja = d * n_heads + h
            jb = 2 * n_heads + ja
            g = jnp.where(in_head, g_all[:, ja:ja + 1], g)
            beta = jnp.where(in_head, beta_all[:, jb:jb + 1], beta)
        tok_ref[3 + 2 * d] = g
        tok_ref[4 + 2 * d] = k * beta
    out_refs = (w2_ref, y1_ref, m_ref, n_ref)

    def chunk_inputs(c0, d):
        rows = pl.ds(c0, CHUNK)
        q_c, k_c, v_c = tok_ref[0, rows, :], tok_ref[1, rows, :], tok_ref[2, rows, :]
        g, kb = tok_ref[3 + 2 * d, rows, :], tok_ref[4 + 2 * d, rows, :]
        incl, _, _ = _tri_masks(reverse=(d == 1))
        ci = _dot(incl.astype(F32), g, precision=HIGHEST)
        ce = ci - g
        ci_end = ci[0:1, :] if d == 1 else ci[CHUNK - 1:CHUNK, :]
        earlier, earlier_or_same, eye = _tiled_masks(n_heads, reverse=(d == 1))
        ci_row = jnp.sum(jnp.where(eye, ci, 0.0), axis=0, keepdims=True)
        ce_row = jnp.sum(jnp.where(eye, ce, 0.0), axis=0, keepdims=True)
        kbs = _stack_heads(kb, heads)
        kkb = _dot_nt(k_c.astype(BF16), kbs)
        qkb = _dot_nt(q_c.astype(BF16), kbs)
        a_ka = kkb * jnp.exp(jnp.where(earlier, ce - ce_row, MASKED))
        a_kk = kkb * jnp.exp(jnp.where(earlier, ce - ci_row, MASKED))
        a_ra = qkb * jnp.exp(jnp.where(earlier_or_same, ci - ce_row, MASKED))
        a_rk = qkb * jnp.exp(jnp.where(earlier_or_same, ci - ci_row, MASKED))
        values = (a_ka, a_kk, a_ra, a_rk, _stack_heads(k_c * jnp.exp(ce), heads), q_c * jnp.exp(ci), v_c,
                  kb * jnp.exp(ci_end - ci), kb * jnp.exp(ci_end - ce))
        return values, jnp.exp(ci_end), eye

    def chunk_body(c, carry):
        args = [[] for _ in range(9)]
        where = []
        for cc in range(CHUNKS_PER_STEP):
            c0 = pl.multiple_of((c * CHUNKS_PER_STEP + cc) * CHUNK, CHUNK)
            for d in range(2):
                values, p_end, eye = chunk_inputs(c0, d)
                for slot, value in zip(args, values):
                    slot.append(value)
                where.append((d, c0, p_end))
        for (d, c0, p_end), res in zip(where, _chunk_matrices(*args, eye, heads, split=True)):
            _store_chunk(out_refs, d, c0, n_heads, *res, p_end)
        return carry

    lax.fori_loop(0, TM // CHUNK // CHUNKS_PER_STEP, chunk_body, 0)


def _gdn_prepare(p_gd, pr, n_ctx):
    b, l, pw = p_gd.shape
    width = pr['gd_width']
    n_heads = width // HEAD_DIM
    nt = l // TM
    kern = functools.partial(_gdn_prep_kernel, n_ctx_tiles=n_ctx // TM, n_tiles=nt, width=width)
    params = [pr['gd_conv_w'], pr['gd_alog_p'], pr['gd_dtb_p'], pr['ones_bd']]
    scan_shapes, scan_specs = _scan_out_specs(b, n_heads, l)
    return pl.pallas_call(
        kern,
        grid=(b, nt),
        in_specs=_halo_specs(pw, nt) + [_full_spec(a) for a in params],
        out_specs=scan_specs,
        out_shape=scan_shapes,
        scratch_shapes=[pltpu.VMEM((TM + 16, pw), F32), pltpu.VMEM((7, TM, width), F32)],
        compiler_params=_cparams(("arbitrary", "arbitrary")),
        name="gdn_prepare",
    )(p_gd, p_gd, p_gd, *params)


def _outproj_kernel(x_ref, mod_ref, na_ref, ryf_ref, ryb_ref, bv_ref, gate_ref, gyf_ref, gyb_ref, z_ref,
                    lnw_ref, lnb_ref, gng_ref, wo_ref, g2_ref, rwh_ref, rwl_ref, rb_ref,
                    x1_ref, h2_ref, route_ref, meta_ref, *, na_w, rw_w):
    n_heads = ryf_ref.shape[1]
    rw_parts, gd_parts = [], []
    for h in range(n_heads):
        y = ryf_ref[0, h] + ryb_ref[0, h]
        yc = y - jnp.mean(y, axis=-1, keepdims=True)
        rw_parts.append(yc * lax.rsqrt(jnp.mean(yc * yc, axis=-1, keepdims=True) + RW_GN_EPS))
        gd_parts.append(_rms(gyf_ref[0, h] + gyb_ref[0, h]) * gng_ref[...])
    yn = jnp.concatenate(rw_parts, axis=-1)
    o_rw = (yn * lnw_ref[...] + lnb_ref[...] + bv_ref[0]) * gate_ref[0]
    z = z_ref[0]
    o_gd = jnp.concatenate(gd_parts, axis=-1) * (z * jax.nn.sigmoid(z))
    o = (_dot(na_ref[0], wo_ref[0:na_w, :])
         + _dot(o_rw.astype(BF16), wo_ref[na_w:na_w + rw_w, :])
         + _dot(o_gd.astype(BF16), wo_ref[na_w + rw_w:, :]))
    mod = mod_ref[0]
    x1 = x_ref[0] + mod[2:3] * o
    x1_ref[0] = x1
    h2 = _rms(x1) * g2_ref[...] * (1.0 + mod[4:5]) + mod[3:4]
    h2_ref[0] = h2
    h2_hi = h2.astype(BF16)
    h2_lo = (h2 - h2_hi.astype(F32)).astype(BF16)
    logits = (_dot(h2_hi, rwh_ref[...]) + (_dot(h2_hi, rwl_ref[...]) + _dot(h2_lo, rwh_ref[...]))) + rb_ref[...]
    lane = lax.broadcasted_iota(jnp.int32, logits.shape, 1)
    chosen, weights = [], []
    onehot = jnp.zeros(logits.shape, F32)
    for kk in range(TOP_K):
        m = jnp.max(logits, axis=-1, keepdims=True)
        idx = jnp.min(jnp.where(logits == m, lane, LANES), axis=-1, keepdims=True)
        chosen.append(idx)
        top_max = m if kk == 0 else top_max
        weights.append(jnp.exp(m - top_max))
        onehot = jnp.where(lane == idx, 1.0, onehot)
        logits = jnp.where(lane == idx, -jnp.inf, logits)
    denom = weights[0] + weights[1] + weights[2] + weights[3]
    tok_i = lax.broadcasted_iota(jnp.int32, (TM, TM), 0)
    tok_j = lax.broadcasted_iota(jnp.int32, (TM, TM), 1)
    before = jnp.where(tok_j < tok_i, 1.0, 0.0).astype(BF16)
    rank = _dot(before, onehot.astype(BF16))
    counts = jnp.sum(onehot, axis=0, keepdims=True)
    count8 = jnp.floor((counts + 7.0) * 0.125) * 8.0
    exp_i = lax.broadcasted_iota(jnp.int32, (LANES, LANES), 0)
    exp_j = lax.broadcasted_iota(jnp.int32, (LANES, LANES), 1)
    seg_start = _dot(jnp.broadcast_to(count8, (8, LANES)), jnp.where(exp_i < exp_j, 1.0, 0.0),
                     precision=HIGHEST)[0:1]
    slot_of = seg_start + rank
    route = jnp.zeros(logits.shape, F32)
    for kk in range(TOP_K):
        slot = jnp.sum(jnp.where(lane == chosen[kk], slot_of, 0.0), axis=-1, keepdims=True)
        route = jnp.where(lane == kk, weights[kk] / denom, route)
        route = jnp.where(lane == TOP_K + kk, slot, route)
    route_ref[0] = route
    row8 = lax.broadcasted_iota(jnp.int32, (8, LANES), 0)
    meta_ref[0, 0] = jnp.where(row8 == 0, count8, jnp.where(row8 == 1, seg_start, 0.0))


def _outproj(xa, mods, o_na, ry, bv, gate, gy, p_gd, pr, n_ctx):
    b, l, d = xa.shape
    nt = l // TM
    na_w = o_na.shape[-1]
    n_heads = ry[0].shape[1]
    rw_w = n_heads * HEAD_DIM
    kern = functools.partial(_outproj_kernel, na_w=na_w, rw_w=rw_w)
    tok = lambda w: pl.BlockSpec((1, TM, w), lambda i, t: (i, t, 0))
    head_major = pl.BlockSpec((1, n_heads, TM, HEAD_DIM), lambda i, t: (i, 0, t, 0))
    params = [pr['rw_ln_w'].reshape(1, rw_w), pr['rw_ln_b'].reshape(1, rw_w), pr['gd_norm_g'].reshape(1, HEAD_DIM),
              pr['w_out_bf'], pr['norm_ffn_g'], pr['router_w_hi'], pr['router_w_lo'], pr['router_b_p']]
    return pl.pallas_call(
        kern,
        grid=(b, nt),
        in_specs=[tok(d), pl.BlockSpec((1, 6, d), _mod_index(n_ctx // TM, b)), tok(na_w),
                  head_major, head_major, tok(rw_w), tok(rw_w), head_major, head_major,
                  pl.BlockSpec((1, TM, rw_w), lambda i, t: (i, t, 3))]
                 + [_full_spec(a) for a in params],
        out_specs=[tok(d), tok(d), tok(LANES), pl.BlockSpec((1, 1, 8, LANES), lambda i, t: (i, t, 0, 0))],
        out_shape=[jax.ShapeDtypeStruct((b, l, d), F32), jax.ShapeDtypeStruct((b, l, d), F32),
                   jax.ShapeDtypeStruct((b, l, LANES), F32), jax.ShapeDtypeStruct((b, nt, 8, LANES), F32)],
        compiler_params=_cparams(("arbitrary", "arbitrary")),
        name="outproj_router",
    )(xa, mods, o_na, ry[0], ry[1], bv, gate, gy[0], gy[1], p_gd, *params)


MOE_BM = 512
MOE_EXPERT_VMEM = 56 * 1024 * 1024
SEG_ALIGN = 8
SEG_BITS = (32, 16, 8, 4, 2, 1)


def _n_slots(n_exp):
    return _round_up(TOP_K * TM + n_exp * (SEG_ALIGN - 1), LANES)


def _segment_dma(segs_ref, buf_ref, hbm_ref, sem, n_exp, to_hbm, wait):
    def body(e, carry):
        start = segs_ref[0, 0, e]
        units = segs_ref[0, 0, n_exp + e]
        offset = segs_ref[0, 0, 2 * n_exp + e]
        for bit in SEG_BITS:
            done = (units & ~(2 * bit - 1)) * SEG_ALIGN
            rows = bit * SEG_ALIGN

            @pl.when((units & bit) != 0)
            def _():
                in_buf = buf_ref.at[pl.ds(pl.multiple_of(start + done, SEG_ALIGN), rows), :]
                in_hbm = hbm_ref.at[pl.ds(pl.multiple_of(offset + done, SEG_ALIGN), rows), :]
                copy = (pltpu.make_async_copy(in_buf, in_hbm, sem) if to_hbm
                        else pltpu.make_async_copy(in_hbm, in_buf, sem))
                if wait:
                    copy.wait()
                else:
                    copy.start()
        return carry
    lax.fori_loop(0, n_exp, body, 0)


def _segment_wait(segs_ref, buf_ref, hbm_ref, sem, n_exp, to_hbm):
    for b, bit in enumerate(SEG_BITS):
        rows = bit * SEG_ALIGN
        in_buf = buf_ref.at[pl.ds(0, rows), :]
        in_hbm = hbm_ref.at[pl.ds(0, rows), :]
        copy = (pltpu.make_async_copy(in_buf, in_hbm, sem) if to_hbm
                else pltpu.make_async_copy(in_hbm, in_buf, sem))

        def body(_, carry, copy=copy):
            copy.wait()
            return carry
        lax.fori_loop(0, segs_ref[0, 0, 3 * n_exp + b], body, 0)


def _slot_lanes(route, n_slots):
    lane = lax.broadcasted_iota(jnp.int32, (route.shape[0], n_slots), 1)
    slots = [route[:, TOP_K + k:TOP_K + k + 1].astype(jnp.int32) for k in range(TOP_K)]
    return lane, slots


def _dispatch_kernel(segs_ref, segs_prev_ref, gaps_ref, h_ref, route_ref, xb_ref, xs_ref, zero_ref, sem, gap_sem,
                     *, n_exp):
    i = pl.program_id(0)
    slot = i % 2

    @pl.when(i == 0)
    def _():
        zero_ref[...] = jnp.zeros_like(zero_ref)
        _segment_dma(gaps_ref, zero_ref, xb_ref, gap_sem, n_exp, to_hbm=True, wait=False)
        _segment_dma(gaps_ref, zero_ref, xb_ref, gap_sem, n_exp, to_hbm=True, wait=True)
        n_blocks = xb_ref.shape[0] // MOE_BM

        def block_copy(blk):
            return pltpu.make_async_copy(zero_ref, xb_ref.at[pl.ds(pl.multiple_of(blk * MOE_BM, MOE_BM), MOE_BM), :],
                                         gap_sem)

        def start(blk, carry):
            block_copy(blk).start()
            return carry

        def wait(blk, carry):
            block_copy(blk).wait()
            return carry

        lax.fori_loop(gaps_ref[0, 0, 3 * n_exp], n_blocks, start, 0)
        lax.fori_loop(gaps_ref[0, 0, 3 * n_exp], n_blocks, wait, 0)

    lane, slots = _slot_lanes(route_ref[...], xs_ref.shape[1])
    picked = lane == slots[0]
    for k in range(1, TOP_K):
        picked = picked | (lane == slots[k])
    onehot = jnp.where(picked, 1.0, 0.0).astype(BF16)
    xs_ref[slot] = _dot_tn(onehot, h_ref[...].astype(BF16))
    _segment_dma(segs_ref, xs_ref.at[slot], xb_ref, sem.at[slot], n_exp, to_hbm=True, wait=False)

    @pl.when(i > 0)
    def _():
        _segment_wait(segs_prev_ref, xs_ref.at[1 - slot], xb_ref, sem.at[1 - slot], n_exp, to_hbm=True)

    @pl.when(i == pl.num_programs(0) - 1)
    def _():
        _segment_wait(segs_ref, xs_ref.at[slot], xb_ref, sem.at[slot], n_exp, to_hbm=True)


def _seg_spec(index):
    return pl.BlockSpec((1, 1, LANES), index, memory_space=pltpu.SMEM)


def _dispatch(h2, route, segs, gaps, n_pad, n_exp):
    t, d = h2.shape
    nt = t // TM
    kern = functools.partial(_dispatch_kernel, n_exp=n_exp)
    return pl.pallas_call(
        kern,
        grid=(nt,),
        in_specs=[_seg_spec(lambda i: (i, 0, 0)),
                  _seg_spec(lambda i: (jnp.maximum(i - 1, 0), 0, 0)),
                  _seg_spec(lambda i: (0, 0, 0)),
                  pl.BlockSpec((TM, d), lambda i: (i, 0)),
                  pl.BlockSpec((TM, LANES), lambda i: (i, 0))],
        out_specs=pl.BlockSpec(memory_space=pl.ANY),
        out_shape=jax.ShapeDtypeStruct((n_pad, d), F32),
        scratch_shapes=[pltpu.VMEM((2, _n_slots(n_exp), d), F32), pltpu.VMEM((MOE_BM, d), F32),
                        pltpu.SemaphoreType.DMA((2,)), pltpu.SemaphoreType.DMA(())],
        compiler_params=_cparams(("arbitrary",)),
        name="moe_dispatch",
    )(segs, segs, gaps, h2, route)


def _expert_kernel(blk_e_ref, n_used_ref, x_ref, wgu_ref, bgu_ref, wd_ref, bd_ref, y_ref, wgu_bf, wd_bf):
    i = pl.program_id(0)
    used = i < n_used_ref[0]
    new_expert = (i == 0) | (blk_e_ref[i] != blk_e_ref[jnp.maximum(i - 1, 0)])

    @pl.when(used & new_expert)
    def _():
        wgu_bf[...] = wgu_ref[0, 0].astype(BF16)
        wd_bf[...] = wd_ref[0, 0].astype(BF16)

    @pl.when(used)
    def _():
        x = x_ref[...].astype(BF16)
        d_exp = wd_bf.shape[0]
        gu = _dot(x, wgu_bf[...]) + bgu_ref[0, 0]
        gate = jnp.minimum(gu[:, :d_exp], SWIGLU_LIMIT)
        lin = jnp.clip(gu[:, d_exp:], -SWIGLU_LIMIT, SWIGLU_LIMIT)
        act = gate * jax.nn.sigmoid(SWIGLU_ALPHA * gate) * (lin + 1.0)
        y_ref[...] = _dot(act.astype(BF16), wd_bf[...]) + bd_ref[0, 0]

    @pl.when(jnp.logical_not(used))
    def _():
        y_ref[...] = jnp.zeros_like(y_ref)


def _expert_blocks(xb, blk_e, n_used, layer, w_gu, b_gu, w_down, b_down):
    n_pad, d = xb.shape
    n_blocks = n_pad // MOE_BM
    depth, n_exp, _, d_gu = w_gu.shape
    grid_spec = pltpu.PrefetchScalarGridSpec(
        num_scalar_prefetch=2,
        grid=(n_blocks,),
        in_specs=[pl.BlockSpec((MOE_BM, d), lambda i, be, nu: (jnp.minimum(i, nu[0] - 1), 0)),
                  pl.BlockSpec((1, 1, d, d_gu), lambda i, be, nu: (layer, be[i], 0, 0)),
                  pl.BlockSpec((1, 1, 1, d_gu), lambda i, be, nu: (layer, be[i], 0, 0)),
                  pl.BlockSpec((1, 1, d_gu // 2, d), lambda i, be, nu: (layer, be[i], 0, 0)),
                  pl.BlockSpec((1, 1, 1, d), lambda i, be, nu: (layer, be[i], 0, 0))],
        out_specs=pl.BlockSpec((MOE_BM, d), lambda i, be, nu: (i, 0)),
        scratch_shapes=[pltpu.VMEM((d, d_gu), BF16), pltpu.VMEM((d_gu // 2, d), BF16)])
    return pl.pallas_call(
        _expert_kernel,
        grid_spec=grid_spec,
        out_shape=jax.ShapeDtypeStruct((n_pad, d), F32),
        compiler_params=pltpu.CompilerParams(dimension_semantics=("arbitrary",),
                                             vmem_limit_bytes=MOE_EXPERT_VMEM),
        name="moe_experts",
    )(blk_e, n_used, xb, w_gu, b_gu.reshape(depth, n_exp, 1, d_gu), w_down, b_down.reshape(depth, n_exp, 1, d))


def _combine_kernel(segs_ref, segs_next_ref, y_hbm, x_ref, mod_ref, route_ref, o_ref, ybuf, sem, *, n_exp):
    i = pl.program_id(0)
    slot = i % 2

    @pl.when(i == 0)
    def _():
        ybuf[...] = jnp.zeros_like(ybuf)
        _segment_dma(segs_ref, ybuf.at[0], y_hbm, sem.at[0], n_exp, to_hbm=False, wait=False)

    @pl.when(i + 1 < pl.num_programs(0))
    def _():
        _segment_dma(segs_next_ref, ybuf.at[1 - slot], y_hbm, sem.at[1 - slot], n_exp, to_hbm=False, wait=False)

    _segment_wait(segs_ref, ybuf.at[slot], y_hbm, sem.at[slot], n_exp, to_hbm=False)
    y = ybuf[slot].astype(BF16)
    route = route_ref[...]
    lane, slots = _slot_lanes(route, ybuf.shape[1])
    weights = jnp.zeros(lane.shape, F32)
    for k in range(TOP_K):
        weights = jnp.where(lane == slots[k], route[:, k:k + 1], weights)
    o_ref[...] = x_ref[...] + mod_ref[0, 5:6, :] * _dot(weights.astype(BF16), y)


def _combine(yb, segs, x1, mods, route, n_batch, l, n_ctx, n_exp, latent_only):
    t, d = x1.shape
    nt = t // TM
    per_sample = l // TM
    n_ctx_tiles = n_ctx // TM
    kern = functools.partial(_combine_kernel, n_exp=n_exp)

    def mod_index(i):
        return (jnp.where(i % per_sample < n_ctx_tiles, n_batch, i // per_sample), 0, 0)

    if latent_only:
        lat_tiles = per_sample - n_ctx_tiles
        out_rows = n_batch * lat_tiles * TM
        out_index = lambda i: ((i // per_sample) * lat_tiles + jnp.maximum(i % per_sample - n_ctx_tiles, 0), 0)
    else:
        out_rows = t
        out_index = lambda i: (i, 0)

    return pl.pallas_call(
        kern,
        grid=(nt,),
        in_specs=[_seg_spec(lambda i: (i, 0, 0)),
                  _seg_spec(lambda i: (jnp.minimum(i + 1, nt - 1), 0, 0)),
                  pl.BlockSpec(memory_space=pl.ANY),
                  pl.BlockSpec((TM, d), lambda i: (i, 0)),
                  pl.BlockSpec((1, 6, d), mod_index),
                  pl.BlockSpec((TM, LANES), lambda i: (i, 0))],
        out_specs=pl.BlockSpec((TM, d), out_index),
        out_shape=jax.ShapeDtypeStruct((out_rows, d), F32),
        scratch_shapes=[pltpu.VMEM((2, _n_slots(n_exp), d), F32), pltpu.SemaphoreType.DMA((2,))],
        compiler_params=_cparams(("arbitrary",)),
        name="moe_combine",
    )(segs, segs, yb, x1, mods, route)


def _seg_table(start, units, offset, extra=None):
    n = start.shape[0]
    cols = [start, units, offset] + ([] if extra is None else [extra])
    pad = jnp.zeros((n, LANES - sum(c.shape[1] for c in cols)), jnp.int32)
    return jnp.concatenate(cols + [pad], axis=1).reshape(n, 1, LANES)


def _route(meta, n_exp, n_tokens):
    nt = meta.shape[0]
    count8 = meta[:, 0, :n_exp].astype(jnp.int32)
    seg_start = meta[:, 1, :n_exp].astype(jnp.int32)
    per_expert = jnp.sum(count8, axis=0)
    padded = (per_expert + MOE_BM - 1) // MOE_BM * MOE_BM
    pad_end = jnp.cumsum(padded)
    pad_start = pad_end - padded
    offset = pad_start[None, :] + jnp.cumsum(count8, axis=0) - count8
    n_pad = _round_up(n_tokens * TOP_K + nt * n_exp * (SEG_ALIGN - 1), MOE_BM) + n_exp * MOE_BM
    block_row = jnp.arange(n_pad // MOE_BM, dtype=jnp.int32) * MOE_BM
    blk_e = jnp.minimum(jnp.sum(block_row[:, None] >= pad_end[None, :], axis=1), n_exp - 1).astype(jnp.int32)
    n_used = (pad_end[-1:] // MOE_BM).astype(jnp.int32)
    units = count8 // SEG_ALIGN
    copies = jnp.stack([jnp.sum((units // bit) % 2, axis=1) for bit in SEG_BITS], axis=1)
    segs = _seg_table(seg_start, units, offset, copies.astype(jnp.int32))
    gaps = _seg_table(jnp.zeros((1, n_exp), jnp.int32), ((padded - per_expert) // SEG_ALIGN)[None, :],
                      (pad_start + per_expert)[None, :], n_used[None, :])
    return segs, gaps, blk_e, n_used, n_pad


def _channel_sublayer(x1, h2, route, meta, mods, layer, experts, n_ctx, latent_only):
    b, l, d = x1.shape
    t = b * l
    n_exp = experts[0].shape[1]
    assert 3 * n_exp + len(SEG_BITS) <= LANES and MOE_BM // SEG_ALIGN <= 2 * SEG_BITS[0]
    route = route.reshape(t, LANES)
    segs, gaps, blk_e, n_used, n_pad = _route(meta.reshape(t // TM, 8, LANES), n_exp, t)
    xb = _dispatch(h2.reshape(t, d), route, segs, gaps, n_pad, n_exp)
    yb = _expert_blocks(xb, blk_e, n_used, layer, *experts)
    x2 = _combine(yb, segs, x1.reshape(t, d), mods, route, b, l, n_ctx, n_exp, latent_only)
    return x2.reshape(b, -1, d)


def _pad_cols(w, n):
    return jnp.pad(w, ((0, 0), (0, n - w.shape[1])))


def _round_up(n, m):
    return -(-n // m) * m


def _prep_layer_params(pr):
    d = pr['w_in'].shape[0]
    na_w = d // 2
    rw_w = d // 4
    gd_w = d - na_w - rw_w
    na_cols = 3 * na_w
    rw_cols = 3 * rw_w + RW_DECAY_LORA + RW_ICLR_LORA + RW_GATE_LORA
    rw_pad = _round_up(rw_cols, LANES)
    gd_cols = pr['w_in'].shape[1] - na_cols - rw_cols
    gd_pad = _round_up(gd_cols, LANES)
    w_in = pr['w_in']
    out = dict(pr)
    out['w_in'] = jnp.concatenate([w_in[:, :na_cols],
                                   _pad_cols(w_in[:, na_cols:na_cols + rw_cols], rw_pad),
                                   _pad_cols(w_in[:, na_cols + rw_cols:], gd_pad)], axis=1).astype(BF16)
    out['rw_w'] = rw_pad
    out['gd_w'] = gd_pad
    n_heads = na_w // HEAD_DIM
    out['qkg'] = jnp.stack([jnp.tile(pr['na_q_gain'], n_heads) * HEAD_DIM ** -0.5,
                            jnp.tile(pr['na_k_gain'], n_heads)])
    head_of = np.arange(na_w) // HEAD_DIM
    out['avg'] = jnp.asarray((head_of[:, None] == head_of[None, :]) / HEAD_DIM, BF16)
    out['norm_mix_g'] = pr['norm_mix_g'].reshape(1, d)
    out['norm_ffn_g'] = pr['norm_ffn_g'].reshape(1, d)
    out['rw_mu_p'] = jnp.pad(pr['rw_mu'], (0, rw_pad - rw_cols)).reshape(1, rw_pad)
    o1, o2 = RW_DECAY_LORA, RW_DECAY_LORA + RW_ICLR_LORA
    out['rw_wup_p'] = jnp.pad(pr['rw_w_up'], ((0, 0), (0, LANES - o1), (0, 0)))
    out['rw_aup_p'] = jnp.pad(pr['rw_a_up'], ((0, 0), (o1, LANES - o2), (0, 0)))
    out['rw_gup_p'] = jnp.pad(pr['rw_g_up'], ((o2, LANES - o2 - RW_GATE_LORA), (0, 0)))
    head_of = np.arange(rw_w) // HEAD_DIM
    out['ones_bd'] = jnp.asarray(head_of[:, None] == head_of[None, :], BF16)
    out['gd_width'] = gd_w
    n_ab = pr['gd_A_log'].size
    out['gd_alog_p'] = jnp.pad(pr['gd_A_log'].reshape(-1), (0, LANES - n_ab)).reshape(1, LANES)
    out['gd_dtb_p'] = jnp.pad(pr['gd_dt_bias'].reshape(-1), (0, LANES - n_ab)).reshape(1, LANES)
    out['w_out_bf'] = pr['w_out'].astype(BF16)
    n_exp = pr['moe_router_w'].shape[1]
    router_w = _pad_cols(pr['moe_router_w'], LANES)
    out['router_w_hi'] = router_w.astype(BF16)
    out['router_w_lo'] = (router_w - out['router_w_hi'].astype(F32)).astype(BF16)
    out['router_b_p'] = jnp.concatenate([pr['moe_router_b'],
                                         jnp.full((LANES - n_exp,), MASKED, F32)]).reshape(1, LANES)
    return out


def _mixing_sublayer(xa, mods, pr, n_ctx):
    b, l, d = xa.shape
    na_w = d // 2
    p_na, p_rw, p_gd = _inproj(xa, mods, pr['norm_mix_g'], pr['w_in'], pr['qkg'], pr['avg'], n_ctx,
                               na_w, pr['rw_w'], pr['gd_w'])
    o_na = _na_attention(p_na, pr['na_bias'], n_ctx, na_w)
    rw2, ry1, rm, rn, bv, gate = _rwkv_prepare(p_rw, pr, n_ctx)
    ry = _scan(rw2, ry1, rm, rn, n_ctx)
    gy = _scan(*_gdn_prepare(p_gd, pr, n_ctx), n_ctx)
    return _outproj(xa, mods, o_na, ry, bv, gate, gy, p_gd, pr, n_ctx)


_LAYER_PARAMS = ('norm_mix_g', 'norm_ffn_g', 'w_in', 'w_out', 'na_q_gain', 'na_k_gain', 'na_rpb',
                 'rw_mu', 'rw_w0', 'rw_w_up', 'rw_a0', 'rw_a_up', 'rw_g_up', 'rw_k_k', 'rw_k_a', 'rw_r_k',
                 'rw_ln_w', 'rw_ln_b', 'gd_conv_w', 'gd_A_log', 'gd_dt_bias', 'gd_norm_g',
                 'moe_router_w', 'moe_router_b', 'moe_w_gu', 'moe_b_gu', 'moe_w_down', 'moe_b_down')


def kernel(x, c, ctx, c_ctx, ada_w, ada_b, norm_mix_g, norm_ffn_g, w_in, w_out, na_q_gain, na_k_gain, na_rpb, rw_mu, rw_w0, rw_w_up, rw_a0, rw_a_up, rw_g_up, rw_k_k, rw_k_a, rw_r_k, rw_ln_w, rw_ln_b, gd_conv_w, gd_A_log, gd_dt_bias, gd_norm_g, moe_router_w, moe_router_b, moe_w_gu, moe_b_gu, moe_w_down, moe_b_down):
    stacked = dict(zip(_LAYER_PARAMS, (norm_mix_g, norm_ffn_g, w_in, w_out, na_q_gain, na_k_gain, na_rpb,
                                       rw_mu, rw_w0, rw_w_up, rw_a0, rw_a_up, rw_g_up, rw_k_k, rw_k_a, rw_r_k,
                                       rw_ln_w, rw_ln_b, gd_conv_w, gd_A_log, gd_dt_bias, gd_norm_g,
                                       moe_router_w, moe_router_b, moe_w_gu, moe_b_gu, moe_w_down, moe_b_down)))
    b, s, d = x.shape
    n_ctx = ctx.shape[1]
    depth = ada_w.shape[0]
    assert b + 1 <= 8 and n_ctx % TM == 0 and s % TM == 0 and s % GRID_W == 0
    cvec = jnp.zeros((8, d), F32).at[:b].set(c).at[b].set(c_ctx)
    mods_all = _adaln(cvec, ada_w, ada_b).reshape(depth, 8, 6, d)[:, :b + 1]
    xa = jnp.concatenate([ctx, x], axis=1)
    experts = (moe_w_gu, moe_b_gu, moe_w_down, moe_b_down)
    for layer in range(depth):
        pr = _prep_layer_params({name: value[layer] for name, value in stacked.items()
                                 if not name.startswith('moe_w_') and not name.startswith('moe_b_')})
        pr['na_bias'] = _na_bias_tables(pr['na_rpb'], s // GRID_W)
        mods = mods_all[layer]
        x1, h2, route, meta = _mixing_sublayer(xa, mods, pr, n_ctx)
        xa = _channel_sublayer(x1, h2, route, meta, mods, layer, experts, n_ctx,
                               latent_only=(layer == depth - 1))
    return xa
```
